```python
import math
import jax, jax.numpy as jnp
from jax import lax
import numpy as np

D_MODEL = 1024
BATCH = 8
SEQ = 2048
DEPTH = 1
DEC_BATCH = 128
DEC_SEQ = 4
PAST_LEN = 2048
PAGE_SIZE = 128

HEAD_DIM = 64
ATTN_GROUPS = ((128, 1), (512, 4), (2048, 16))
HEADS_PER_GROUP = 4
N_ATTN_HEADS = HEADS_PER_GROUP * len(ATTN_GROUPS)
ATTN_WIDTH = N_ATTN_HEADS * HEAD_DIM
SSM_WIDTH = D_MODEL - ATTN_WIDTH
SSM_GROUP = 16
SSM_GROUPS = SSM_WIDTH // SSM_GROUP
SSM_STATE = 64
IN_WIDTH = 3 * ATTN_WIDTH + SSM_WIDTH
ROT_DIM = HEAD_DIM // 4
ROPE_THETA = 500000.0
N_MEM = 256
X_HEADS = 4
X_HEAD_DIM = D_MODEL // X_HEADS
D_FF = 2816
CONV_W = 3
EPS = 1e-6

kernel_name = "hymba_s5_longnet_convffn_step"


def rmsnorm(x, g):
    xf = x.astype(jnp.float32)
    y = xf * lax.rsqrt(jnp.mean(xf * xf, axis=-1, keepdims=True) + EPS)
    return (y * g.astype(jnp.float32)).astype(x.dtype)


def rope_partial(x, pos):
    half = ROT_DIM // 2
    inv = jnp.power(jnp.float32(ROPE_THETA), -jnp.arange(half, dtype=jnp.float32) * 2.0 / ROT_DIM)
    ang = pos.astype(jnp.float32)[:, None] * inv[None, :]
    cos = jnp.cos(ang)[None, :, None, :]
    sin = jnp.sin(ang)[None, :, None, :]
    xr = x[..., :ROT_DIM].astype(jnp.float32)
    x1, x2 = xr[..., :half], xr[..., half:]
    rot = jnp.concatenate([x1 * cos - x2 * sin, x2 * cos + x1 * sin], axis=-1)
    return jnp.concatenate([rot.astype(x.dtype), x[..., ROT_DIM:]], axis=-1)


def dilated_attn_prompt(q, k, v, window, dil):
    B, L, H, E = q.shape
    S = L // dil
    blk = window // dil
    nb = -(-S // blk)
    Sp = nb * blk

    def to_blocks(t):
        t = t.reshape(B, S, dil, H, E)
        t = jnp.pad(t, ((0, 0), (0, Sp - S), (0, 0), (0, 0), (0, 0)))
        return t.reshape(B, nb, blk, dil, H, E)

    def band(t):
        tp = jnp.pad(t, ((0, 0), (1, 0), (0, 0), (0, 0), (0, 0), (0, 0)))
        return jnp.concatenate([tp[:, :-1], tp[:, 1:]], axis=2)

    qb = to_blocks(q)
    kb = band(to_blocks(k))
    vb = band(to_blocks(v))
    s = jnp.einsum('bnqrhe,bnkrhe->bnrhqk', qb, kb, preferred_element_type=jnp.float32)
    qi = jnp.arange(nb)[:, None, None] * blk + jnp.arange(blk)[None, :, None]
    ki = (jnp.arange(nb)[:, None, None] - 1) * blk + jnp.arange(2 * blk)[None, None, :]
    delta = qi - ki
    mask = (delta >= 0) & (delta <= blk) & (ki >= 0)
    s = jnp.where(mask[None, :, None, None], s, -jnp.inf)
    lse = jax.nn.logsumexp(s, axis=-1)
    p = jnp.exp(s - lse[..., None])
    o = jnp.einsum('bnrhqk,bnkrhe->bnqrhe', p.astype(v.dtype), vb)
    o = o.reshape(B, Sp, dil, H, E)[:, :S].reshape(B, L, H, E)
    lse = jnp.transpose(lse, (0, 1, 4, 2, 3)).reshape(B, Sp, dil, H)[:, :S].reshape(B, L, H)
    return o, lse


def dilated_attn_sample(q, k_all, v_all, window, dil, n_buf):
    T = q.shape[1]
    n_keys = window // dil + 1
    idx = n_buf + jnp.arange(T)[:, None] - dil * jnp.arange(n_keys)[None, :]
    valid = idx >= 0
    idx = jnp.maximum(idx, 0)
    kg = k_all[:, idx]
    vg = v_all[:, idx]
    s = jnp.einsum('bthe,btjhe->bthj', q, kg, preferred_element_type=jnp.float32)
    s = jnp.where(valid[None, :, None, :], s, -jnp.inf)
    lse = jax.nn.logsumexp(s, axis=-1)
    p = jnp.exp(s - lse[..., None])
    o = jnp.einsum('bthj,btjhe->bthe', p.astype(vg.dtype), vg)
    return o, lse


def s5_scan(u, s0_re, s0_im, p):
    uf = u.astype(jnp.float32)
    lam_re = p['ssm_lam_re'].astype(jnp.float32)
    lam_im = p['ssm_lam_im'].astype(jnp.float32)
    dt = jnp.exp(p['ssm_log_dt'].astype(jnp.float32))[:, None]
    mag = jnp.exp(lam_re * dt)
    a_re = mag * jnp.cos(lam_im * dt)
    a_im = mag * jnp.sin(lam_im * dt)
    den = lam_re * lam_re + lam_im * lam_im
    nr, ni = a_re - 1.0, a_im
    c_re = (nr * lam_re + ni * lam_im) / den
    c_im = (ni * lam_re - nr * lam_im) / den
    bu_re = jnp.einsum('blgh,gph->blgp', uf, p['ssm_b_re'].astype(jnp.float32))
    bu_im = jnp.einsum('blgh,gph->blgp', uf, p['ssm_b_im'].astype(jnp.float32))
    b_re = c_re * bu_re - c_im * bu_im
    b_im = c_re * bu_im + c_im * bu_re
    s0r = s0_re.astype(jnp.float32)
    s0i = s0_im.astype(jnp.float32)
    b_re = b_re.at[:, 0].add(a_re * s0r - a_im * s0i)
    b_im = b_im.at[:, 0].add(a_re * s0i + a_im * s0r)
    ar = jnp.broadcast_to(a_re, b_re.shape)
    ai = jnp.broadcast_to(a_im, b_im.shape)

    def combine(e1, e2):
        a1r, a1i, b1r, b1i = e1
        a2r, a2i, b2r, b2i = e2
        return (a2r * a1r - a2i * a1i, a2r * a1i + a2i * a1r,
                a2r * b1r - a2i * b1i + b2r, a2r * b1i + a2i * b1r + b2i)

    _, _, h_re, h_im = lax.associative_scan(combine, (ar, ai, b_re, b_im), axis=1)
    y = (jnp.einsum('blgp,ghp->blgh', h_re, p['ssm_c_re'].astype(jnp.float32))
         - jnp.einsum('blgp,ghp->blgh', h_im, p['ssm_c_im'].astype(jnp.float32))
         + p['ssm_d'].astype(jnp.float32) * uf)
    return y.astype(u.dtype), h_re[:, -1], h_im[:, -1]


def parallel_mixer(xn, pos, past_kv, s0_re, s0_im, p):
    B, L, _ = xn.shape
    z = xn @ p['w_in']
    q = z[..., :ATTN_WIDTH].reshape(B, L, N_ATTN_HEADS, HEAD_DIM)
    k = z[..., ATTN_WIDTH:2 * ATTN_WIDTH].reshape(B, L, N_ATTN_HEADS, HEAD_DIM)
    v = z[..., 2 * ATTN_WIDTH:3 * ATTN_WIDTH].reshape(B, L, N_ATTN_HEADS, HEAD_DIM)
    u = z[..., 3 * ATTN_WIDTH:].reshape(B, L, SSM_GROUPS, SSM_GROUP)
    q = rope_partial(rmsnorm(q, p['q_norm']), pos) * (HEAD_DIM ** -0.5)
    k = rope_partial(rmsnorm(k, p['k_norm']), pos)
    outs, lses, new_kv = [], [], []
    for g, (win, dil) in enumerate(ATTN_GROUPS):
        hs = slice(g * HEADS_PER_GROUP, (g + 1) * HEADS_PER_GROUP)
        qg, kg, vg = q[:, :, hs], k[:, :, hs], v[:, :, hs]
        if past_kv is None:
            o, lse = dilated_attn_prompt(qg, kg, vg, win, dil)
            keep = min(win, L)
            new_kv.append(jnp.stack([kg[:, L - keep:], vg[:, L - keep:]], axis=2))
        else:
            buf = past_kv[g]
            k_all = jnp.concatenate([buf[:, :, 0], kg], axis=1)
            v_all = jnp.concatenate([buf[:, :, 1], vg], axis=1)
            o, lse = dilated_attn_sample(qg, k_all, v_all, win, dil, buf.shape[1])
            new_kv.append(jnp.stack([kg, vg], axis=2))
        outs.append(o)
        lses.append(lse)
    alpha = jax.nn.softmax(jnp.stack(lses, axis=2), axis=2)
    attn = (jnp.stack(outs, axis=2) * alpha[..., None].astype(xn.dtype)).reshape(B, L, ATTN_WIDTH)
    y, h_re, h_im = s5_scan(u, s0_re, s0_im, p)
    zs = jax.nn.gelu(y.reshape(B, L, SSM_WIDTH))
    ssm_out = zs * jax.nn.sigmoid(zs @ p['w_glu'] + p['b_glu'])
    mixed = jnp.concatenate([rmsnorm(attn, p['attn_out_norm']), rmsnorm(ssm_out, p['ssm_out_norm'])], axis=-1)
    return mixed @ p['w_out'], new_kv, h_re, h_im


def memory_kv(mem, p):
    B = mem.shape[0]
    kv = (rmsnorm(mem, p['norm_mem']) @ p['w_xkv']).reshape(B, N_MEM, 2, X_HEADS, X_HEAD_DIM)
    return jnp.stack([rmsnorm(kv[:, :, 0], p['xk_norm']), kv[:, :, 1]], axis=2)


def cross_attention(hn, mkv, p):
    B, L, _ = hn.shape
    q = rmsnorm((hn @ p['w_xq']).reshape(B, L, X_HEADS, X_HEAD_DIM), p['xq_norm']) * (X_HEAD_DIM ** -0.5)
    s = jnp.einsum('blhe,bmhe->bhlm', q, mkv[:, :, 0], preferred_element_type=jnp.float32)
    pr = jax.nn.softmax(s, axis=-1)
    o = jnp.einsum('bhlm,bmhe->blhe', pr.astype(hn.dtype), mkv[:, :, 1]).reshape(B, L, D_MODEL)
    return o @ p['w_xo']


def conv_ffn(hn, conv_buf, p):
    up = hn @ p['w_up']
    L = up.shape[1]
    padded = jnp.concatenate([conv_buf.astype(up.dtype), up], axis=1)
    cw = p['conv_w']
    c = p['conv_b'] + sum(cw[i] * padded[:, i:i + L] for i in range(CONV_W))
    gate, val = jnp.split(c, 2, axis=-1)
    return (jax.nn.silu(gate) * val) @ p['w_down'], padded[:, -(CONV_W - 1):]


def decoder_layer(x, pos, past_kv, s0_re, s0_im, conv_buf, mkv, p):
    mix, new_kv, h_re, h_im = parallel_mixer(rmsnorm(x, p['norm_mix']), pos, past_kv, s0_re, s0_im, p)
    h = x + mix
    h = h + cross_attention(rmsnorm(h, p['norm_x']), mkv, p)
    f, new_conv = conv_ffn(rmsnorm(h, p['norm_ffn']), conv_buf, p)
    return h + f, new_kv, h_re, h_im, new_conv


def setup_inputs(seed: int = 0) -> dict:
    key = jax.random.key(seed)
    ks = iter(jax.random.split(key, 64))
    f32 = jnp.float32

    def nrm(shape, scale):
        return scale * jax.random.normal(next(ks), shape, f32)

    def gain(shape):
        return 1.0 + 0.02 * jax.random.normal(next(ks), shape, f32)

    bufs = [min(w, PAST_LEN) for w, _ in ATTN_GROUPS]
    G, P, H = SSM_GROUPS, SSM_STATE, SSM_GROUP
    lam_im = jnp.pi * jnp.broadcast_to(jnp.arange(P, dtype=f32), (DEPTH, G, P))
    return {
        'x_prompt': nrm((BATCH, SEQ, D_MODEL), 1.0),
        'x_sample': nrm((DEC_BATCH, DEC_SEQ, D_MODEL), 1.0),
        'mem_prompt': nrm((BATCH, N_MEM, D_MODEL), 1.0),
        'cache_kv_dil1': nrm((DEPTH, DEC_BATCH, bufs[0], 2, HEADS_PER_GROUP, HEAD_DIM), 1.0),
        'cache_kv_dil4': nrm((DEPTH, DEC_BATCH, bufs[1], 2, HEADS_PER_GROUP, HEAD_DIM), 1.0),
        'cache_kv_dil16': nrm((DEPTH, DEC_BATCH, bufs[2], 2, HEADS_PER_GROUP, HEAD_DIM), 1.0),
        'state_ssm_re': nrm((DEPTH, DEC_BATCH, G, P), 0.5),
        'state_ssm_im': nrm((DEPTH, DEC_BATCH, G, P), 0.5),
        'state_ffn_conv': nrm((DEPTH, DEC_BATCH, CONV_W - 1, 2 * D_FF), 1.0),
        'cache_mem_kv': nrm((DEPTH, DEC_BATCH, N_MEM, 2, X_HEADS, X_HEAD_DIM), 1.0),
        'norm_mix': gain((DEPTH, D_MODEL)),
        'w_in': nrm((DEPTH, D_MODEL, IN_WIDTH), D_MODEL ** -0.5),
        'q_norm': gain((DEPTH, HEAD_DIM)),
        'k_norm': gain((DEPTH, HEAD_DIM)),
        'ssm_lam_re': -0.5 + nrm((DEPTH, G, P), 0.01),
        'ssm_lam_im': lam_im + nrm((DEPTH, G, P), 0.01),
        'ssm_log_dt': jax.random.uniform(next(ks), (DEPTH, G), f32, math.log(1e-3), math.log(1e-1)),
        'ssm_b_re': nrm((DEPTH, G, P, H), (2 * H) ** -0.5),
        'ssm_b_im': nrm((DEPTH, G, P, H), (2 * H) ** -0.5),
        'ssm_c_re': nrm((DEPTH, G, H, P), P ** -0.5),
        'ssm_c_im': nrm((DEPTH, G, H, P), P ** -0.5),
        'ssm_d': nrm((DEPTH, G, H), 1.0),
        'w_glu': nrm((DEPTH, SSM_WIDTH, SSM_WIDTH), SSM_WIDTH ** -0.5),
        'b_glu': nrm((DEPTH, SSM_WIDTH), 0.01),
        'attn_out_norm': gain((DEPTH, ATTN_WIDTH)),
        'ssm_out_norm': gain((DEPTH, SSM_WIDTH)),
        'w_out': nrm((DEPTH, D_MODEL, D_MODEL), D_MODEL ** -0.5),
        'norm_x': gain((DEPTH, D_MODEL)),
        'norm_mem': gain((DEPTH, D_MODEL)),
        'w_xq': nrm((DEPTH, D_MODEL, D_MODEL), D_MODEL ** -0.5),
        'w_xkv': nrm((DEPTH, D_MODEL, 2 * D_MODEL), D_MODEL ** -0.5),
        'xq_norm': gain((DEPTH, X_HEAD_DIM)),
        'xk_norm': gain((DEPTH, X_HEAD_DIM)),
        'w_xo': nrm((DEPTH, D_MODEL, D_MODEL), D_MODEL ** -0.5),
        'norm_ffn': gain((DEPTH, D_MODEL)),
        'w_up': nrm((DEPTH, D_MODEL, 2 * D_FF), D_MODEL ** -0.5),
        'conv_w': nrm((DEPTH, CONV_W, 2 * D_FF), CONV_W ** -0.5),
        'conv_b': nrm((DEPTH, 2 * D_FF), 0.01),
        'w_down': nrm((DEPTH, D_FF, D_MODEL), D_FF ** -0.5),
    }


def reference(x_prompt, x_sample, mem_prompt, cache_kv_dil1, cache_kv_dil4, cache_kv_dil16,
              state_ssm_re, state_ssm_im, state_ffn_conv, cache_mem_kv,
              norm_mix, w_in, q_norm, k_norm, ssm_lam_re, ssm_lam_im, ssm_log_dt,
              ssm_b_re, ssm_b_im, ssm_c_re, ssm_c_im, ssm_d, w_glu, b_glu,
              attn_out_norm, ssm_out_norm, w_out, norm_x, norm_mem, w_xq, w_xkv,
              xq_norm, xk_norm, w_xo, norm_ffn, w_up, conv_w, conv_b, w_down):
    Bp, Lp = x_prompt.shape[0], x_prompt.shape[1]
    pos_p = jnp.arange(Lp)
    pos_s = PAST_LEN + jnp.arange(x_sample.shape[1])
    yp, ys = x_prompt, x_sample
    n_g = len(ATTN_GROUPS)
    kv_p = [[] for _ in range(n_g)]
    kv_s = [[] for _ in range(n_g)]
    re_p, im_p, conv_p, mem_p = [], [], [], []
    re_s, im_s, conv_s = [], [], []
    for l in range(DEPTH):
        p = dict(norm_mix=norm_mix[l], w_in=w_in[l], q_norm=q_norm[l], k_norm=k_norm[l],
                 ssm_lam_re=ssm_lam_re[l], ssm_lam_im=ssm_lam_im[l], ssm_log_dt=ssm_log_dt[l],
                 ssm_b_re=ssm_b_re[l], ssm_b_im=ssm_b_im[l], ssm_c_re=ssm_c_re[l], ssm_c_im=ssm_c_im[l],
                 ssm_d=ssm_d[l], w_glu=w_glu[l], b_glu=b_glu[l], attn_out_norm=attn_out_norm[l],
                 ssm_out_norm=ssm_out_norm[l], w_out=w_out[l], norm_x=norm_x[l], norm_mem=norm_mem[l],
                 w_xq=w_xq[l], w_xkv=w_xkv[l], xq_norm=xq_norm[l], xk_norm=xk_norm[l], w_xo=w_xo[l],
                 norm_ffn=norm_ffn[l], w_up=w_up[l], conv_w=conv_w[l], conv_b=conv_b[l], w_down=w_down[l])
        mkv_p = memory_kv(mem_prompt, p)
        s0 = jnp.zeros((Bp, SSM_GROUPS, SSM_STATE), jnp.float32)
        cb0 = jnp.zeros((Bp, CONV_W - 1, 2 * D_FF), yp.dtype)
        yp, nkv, hr, hi, nc = decoder_layer(yp, pos_p, None, s0, s0, cb0, mkv_p, p)
        for g in range(n_g):
            kv_p[g].append(nkv[g])
        re_p.append(hr)
        im_p.append(hi)
        conv_p.append(nc)
        mem_p.append(mkv_p)
        past = (cache_kv_dil1[l], cache_kv_dil4[l], cache_kv_dil16[l])
        ys, nkv, hr, hi, nc = decoder_layer(ys, pos_s, past, state_ssm_re[l], state_ssm_im[l],
                                            state_ffn_conv[l], cache_mem_kv[l], p)
        for g in range(n_g):
            kv_s[g].append(nkv[g])
        re_s.append(hr)
        im_s.append(hi)
        conv_s.append(nc)
    kvp = [jnp.stack(a) for a in kv_p]
    kvs = [jnp.stack(a) for a in kv_s]
    return (yp, ys,
            kvp[0], kvp[1], kvp[2], jnp.stack(re_p), jnp.stack(im_p), jnp.stack(conv_p), jnp.stack(mem_p),
            kvs[0], kvs[1], kvs[2], jnp.stack(re_s), jnp.stack(im_s), jnp.stack(conv_s))
```

```python
import functools
import math

import jax
import jax.numpy as jnp
from jax import lax
from jax.experimental import pallas as pl
from jax.experimental.pallas import tpu as pltpu

F32 = jnp.float32
BF16 = jnp.bfloat16

D_MODEL = 1024
HEAD_DIM = 64
ATTN_GROUPS = ((128, 1), (512, 4), (2048, 16))
HEADS_PER_GROUP = 4
GROUP_W = HEADS_PER_GROUP * HEAD_DIM
N_GROUPS = len(ATTN_GROUPS)
ATTN_WIDTH = N_GROUPS * GROUP_W
SSM_WIDTH = D_MODEL - ATTN_WIDTH
SSM_GROUP = 16
SSM_GROUPS = SSM_WIDTH // SSM_GROUP
SSM_STATE = 64
SSM_LANES = SSM_GROUPS * SSM_STATE
IN_WIDTH = 3 * ATTN_WIDTH + SSM_WIDTH
ROT_DIM = HEAD_DIM // 4
ROPE_THETA = 500000.0
N_MEM = 256
X_HEADS = 4
X_HEAD_DIM = D_MODEL // X_HEADS
D_FF = 2816
CONV_W = 3
EPS = 1e-6
ATTN_BLK = 128
NEG = -1e30

VMEM_LIMIT = 56 * 1024 * 1024


def _cparams(sem):
    return pltpu.CompilerParams(dimension_semantics=sem, vmem_limit_bytes=VMEM_LIMIT)


def _dot(a, b):
    return jnp.dot(a, b, preferred_element_type=F32)


def _dot_t(a, b):
    return lax.dot_general(a, b, (((1,), (1,)), ((), ())), preferred_element_type=F32)


def _rms_rows(x, gain):
    y = x * lax.rsqrt(jnp.mean(x * x, axis=-1, keepdims=True) + EPS)
    return y * gain


def _seg_ones(width, seg):
    r = lax.broadcasted_iota(jnp.int32, (width, width), 0) // seg
    c = lax.broadcasted_iota(jnp.int32, (width, width), 1) // seg
    return (r == c).astype(BF16)


def _seg_rms(z, gain, ones_seg, seg):
    sq = z * z
    hi = sq.astype(BF16)
    lo = (sq - hi.astype(F32)).astype(BF16)
    ssum = _dot(hi, ones_seg) + _dot(lo, ones_seg)
    return z * lax.rsqrt(ssum * (1.0 / seg) + EPS) * gain


def _inproj_kernel(x_ref, g_ref, w_ref, qn_ref, kn_ref, c_ref, s1_ref, s2_ref,
                   q0_ref, q1_ref, q2_ref, kv0_ref, kv1_ref, kv2_ref, u_ref):
    xn = _rms_rows(x_ref[...], g_ref[...]).astype(BF16)
    ones64 = _seg_ones(GROUP_W, HEAD_DIM)
    cosw, sin1, sin2 = c_ref[...], s1_ref[...], s2_ref[...]
    qn, kn = qn_ref[...], kn_ref[...]
    half = ROT_DIM // 2

    def rope(z):
        return z * cosw + pltpu.roll(z, GROUP_W - half, 1) * sin1 + pltpu.roll(z, half, 1) * sin2

    q_refs = (q0_ref, q1_ref, q2_ref)
    kv_refs = (kv0_ref, kv1_ref, kv2_ref)
    for g in range(N_GROUPS):
        zq = _dot(xn, w_ref[:, g * GROUP_W:(g + 1) * GROUP_W])
        zk = _dot(xn, w_ref[:, ATTN_WIDTH + g * GROUP_W:ATTN_WIDTH + (g + 1) * GROUP_W])
        zv = _dot(xn, w_ref[:, 2 * ATTN_WIDTH + g * GROUP_W:2 * ATTN_WIDTH + (g + 1) * GROUP_W])
        q_refs[g][...] = rope(_seg_rms(zq, qn, ones64, HEAD_DIM)) * (HEAD_DIM ** -0.5)
        kv_refs[g][:, :GROUP_W] = rope(_seg_rms(zk, kn, ones64, HEAD_DIM))
        kv_refs[g][:, GROUP_W:] = zv
    u_ref[...] = _dot(xn, w_ref[:, 3 * ATTN_WIDTH:])


def _in_proj(x2d, gain, w_bf, qn, kn, cos_t, sin1_t, sin2_t, tm):
    n = x2d.shape[0]
    n_pos = cos_t.shape[0] // tm
    row = lambda i: (i, 0)
    fixed = lambda i: (0, 0)
    pos = lambda i: (i % n_pos, 0)
    sds = jax.ShapeDtypeStruct
    return pl.pallas_call(
        _inproj_kernel,
        grid=(n // tm,),
        in_specs=[pl.BlockSpec((tm, D_MODEL), row), pl.BlockSpec((1, D_MODEL), fixed),
                  pl.BlockSpec((D_MODEL, IN_WIDTH), fixed),
                  pl.BlockSpec((1, GROUP_W), fixed), pl.BlockSpec((1, GROUP_W), fixed),
                  pl.BlockSpec((tm, GROUP_W), pos), pl.BlockSpec((tm, GROUP_W), pos),
                  pl.BlockSpec((tm, GROUP_W), pos)],
        out_specs=[pl.BlockSpec((tm, GROUP_W), row)] * 3 + [pl.BlockSpec((tm, 2 * GROUP_W), row)] * 3
                  + [pl.BlockSpec((tm, SSM_WIDTH), row)],
        out_shape=[sds((n, GROUP_W), F32)] * 3 + [sds((n, 2 * GROUP_W), F32)] * 3 + [sds((n, SSM_WIDTH), F32)],
        compiler_params=_cparams(("arbitrary",)),
        name="in_proj",
    )(x2d, gain, w_bf, qn, kn, cos_t, sin1_t, sin2_t)


def _pattn_kernel(*refs, has_prev):
    if has_prev:
        q_ref, kc_ref, vc_ref, kp_ref, vp_ref, o_ref, lse_ref = refs
    else:
        q_ref, kc_ref, vc_ref, o_ref, lse_ref = refs
    n = pl.program_id(2)
    blk = ATTN_BLK
    q = q_ref[...]
    head = lax.broadcasted_iota(jnp.int32, (1, GROUP_W), 1) // HEAD_DIM
    qs = jnp.concatenate([jnp.where(head == h, q, 0.0) for h in range(HEADS_PER_GROUP)], axis=0).astype(BF16)
    if has_prev:
        k = jnp.concatenate([kp_ref[...], kc_ref[...]], axis=0).astype(BF16)
        v = jnp.concatenate([vp_ref[...], vc_ref[...]], axis=0).astype(BF16)
    else:
        k = kc_ref[...].astype(BF16)
        v = vc_ref[...].astype(BF16)
    nk = k.shape[0]
    s = _dot_t(qs, k)
    qi = lax.broadcasted_iota(jnp.int32, (HEADS_PER_GROUP * blk, nk), 0) % blk
    ki = lax.broadcasted_iota(jnp.int32, (HEADS_PER_GROUP * blk, nk), 1)
    if has_prev:
        mask = (ki >= qi) & (ki <= qi + blk) & ((n > 0) | (ki >= blk))
    else:
        mask = ki <= qi
    s = jnp.where(mask, s, NEG)
    m = jnp.max(s, axis=-1, keepdims=True)
    p = jnp.exp(s - m)
    l = jnp.sum(p, axis=-1, keepdims=True)
    o_all = _dot(p.astype(BF16), v) / l
    lse = m + jnp.log(l)
    o = jnp.zeros((blk, GROUP_W), F32)
    lb = jnp.zeros((blk, GROUP_W), F32)
    for h in range(HEADS_PER_GROUP):
        o = o + jnp.where(head == h, o_all[h * blk:(h + 1) * blk], 0.0)
        lb = lb + jnp.where(head == h, lse[h * blk:(h + 1) * blk], 0.0)
    o_ref[...] = o
    lse_ref[...] = lb


def _prompt_attn(q, kv, batch, seq, dil):
    sub = seq // dil
    nb = sub // ATTN_BLK
    has_prev = nb > 1
    q3 = q.reshape(batch, sub, dil * GROUP_W)
    kv3 = kv.reshape(batch, sub, dil * 2 * GROUP_W)
    blk = (None, ATTN_BLK, GROUP_W)
    in_specs = [pl.BlockSpec(blk, lambda b, r, n: (b, n, r)),
                pl.BlockSpec(blk, lambda b, r, n: (b, n, 2 * r)),
                pl.BlockSpec(blk, lambda b, r, n: (b, n, 2 * r + 1))]
    args = [q3, kv3, kv3]
    if has_prev:
        in_specs += [pl.BlockSpec(blk, lambda b, r, n: (b, jnp.maximum(n - 1, 0), 2 * r)),
                     pl.BlockSpec(blk, lambda b, r, n: (b, jnp.maximum(n - 1, 0), 2 * r + 1))]
        args += [kv3, kv3]
    o, lse = pl.pallas_call(
        functools.partial(_pattn_kernel, has_prev=has_prev),
        grid=(batch, dil, nb),
        in_specs=in_specs,
        out_specs=[pl.BlockSpec(blk, lambda b, r, n: (b, n, r))] * 2,
        out_shape=[jax.ShapeDtypeStruct((batch, sub, dil * GROUP_W), F32)] * 2,
        compiler_params=_cparams(("arbitrary",) * 3),
        name=f"prompt_attn_d{dil}",
    )(*args)
    return o.reshape(batch * seq, GROUP_W), lse.reshape(batch * seq, GROUP_W)


def _sattn_kernel(q_ref, kvn_ref, cache_ref, o_ref, lse_ref, *, dil, bb_n, t_n):
    head = lax.broadcasted_iota(jnp.int32, (1, GROUP_W), 1) // HEAD_DIM
    hrow = lax.broadcasted_iota(jnp.int32, (HEADS_PER_GROUP, 1), 0)
    sel = head == hrow
    cidx = lax.broadcasted_iota(jnp.int32, (1, ATTN_BLK), 1)
    for bb in range(bb_n):
        q = q_ref[bb]
        knew = kvn_ref[bb, :, :GROUP_W]
        vnew = kvn_ref[bb, :, GROUP_W:]
        for t in range(t_n):
            off = 0 if dil == 1 else t * 2 * GROUP_W
            kc = cache_ref[bb, :, off:off + GROUP_W].astype(BF16)
            vc = cache_ref[bb, :, off + GROUP_W:off + 2 * GROUP_W].astype(BF16)
            qm = jnp.where(sel, q[t:t + 1, :], 0.0)
            s_c = _dot_t(qm.astype(BF16), kc)
            if dil == 1:
                s_c = jnp.where(cidx >= t, s_c, NEG)
                new_rows = range(t + 1)
            else:
                new_rows = (t,)
            qmr = qm.astype(BF16).astype(F32)
            s_n = [jnp.sum(qmr * knew[u:u + 1, :].astype(BF16).astype(F32), axis=-1, keepdims=True) for u in new_rows]
            m = jnp.max(s_c, axis=-1, keepdims=True)
            for sn in s_n:
                m = jnp.maximum(m, sn)
            p_c = jnp.exp(s_c - m)
            l = jnp.sum(p_c, axis=-1, keepdims=True)
            acc = _dot(p_c.astype(BF16), vc)
            for u, sn in zip(new_rows, s_n):
                p_n = jnp.exp(sn - m)
                l = l + p_n
                acc = acc + p_n.astype(BF16).astype(F32) * vnew[u:u + 1, :].astype(BF16).astype(F32)
            o_all = acc / l
            lse = m + jnp.log(l)
            o_ref[bb, t:t + 1, :] = jnp.sum(jnp.where(sel, o_all, 0.0), axis=0, keepdims=True)
            lse_ref[bb, t:t + 1, :] = jnp.sum(jnp.where(sel, lse, 0.0), axis=0, keepdims=True)


def _sample_attn(q, kvn, cache, batch, t_n, dil, bb_n=8):
    n_buf = cache.shape[1]
    assert n_buf == ATTN_BLK * dil and (dil == 1 or t_n <= dil)
    cache3 = cache.reshape(batch, ATTN_BLK, dil * 2 * GROUP_W)
    width = min(dil, t_n) * 2 * GROUP_W
    q3 = q.reshape(batch, t_n, GROUP_W)
    kvn3 = kvn.reshape(batch, t_n, 2 * GROUP_W)
    o, lse = pl.pallas_call(
        functools.partial(_sattn_kernel, dil=dil, bb_n=bb_n, t_n=t_n),
        grid=(batch // bb_n,),
        in_specs=[pl.BlockSpec((bb_n, t_n, GROUP_W), lambda i: (i, 0, 0)),
                  pl.BlockSpec((bb_n, t_n, 2 * GROUP_W), lambda i: (i, 0, 0)),
                  pl.BlockSpec((bb_n, ATTN_BLK, width), lambda i: (i, 0, 0))],
        out_specs=[pl.BlockSpec((bb_n, t_n, GROUP_W), lambda i: (i, 0, 0))] * 2,
        out_shape=[jax.ShapeDtypeStruct((batch, t_n, GROUP_W), F32)] * 2,
        compiler_params=_cparams(("arbitrary",)),
        name=f"sample_attn_d{dil}",
    )(q3, kvn3, cache3)
    return o.reshape(batch * t_n, GROUP_W), lse.reshape(batch * t_n, GROUP_W)


def _ssm_kernel(u_ref, s0r_ref, s0i_ref, lre_ref, lim_ref, ldt_ref, bre_ref, bim_ref, cre_ref, cim_ref, d_ref,
                y_ref, fr_ref, fi_ref, hre, him, str_, sti, *, n_seq, steps):
    i = pl.program_id(0)

    @pl.when(i == 0)
    def _():
        str_[...] = s0r_ref[...]
        sti[...] = s0i_ref[...]

    lam_re, lam_im = lre_ref[...], lim_ref[...]
    dt = jnp.exp(ldt_ref[...])
    mag = jnp.exp(lam_re * dt)
    a_re = mag * jnp.cos(lam_im * dt)
    a_im = mag * jnp.sin(lam_im * dt)
    den = lam_re * lam_re + lam_im * lam_im
    nr, ni = a_re - 1.0, a_im
    c_re = (nr * lam_re + ni * lam_im) / den
    c_im = (ni * lam_re - nr * lam_im) / den

    u = u_ref[...]
    ub = u.astype(BF16)
    bu_re = _dot(ub, bre_ref[...])
    bu_im = _dot(ub, bim_ref[...])
    hre[...] = c_re * bu_re - c_im * bu_im
    him[...] = c_re * bu_im + c_im * bu_re

    if n_seq <= 8:
        def step(t, carry):
            sr, si = carry
            rows = pl.ds(pl.multiple_of(t * n_seq, n_seq), n_seq)
            nr_ = a_re * sr - a_im * si + hre[rows, :]
            ni_ = a_re * si + a_im * sr + him[rows, :]
            hre[rows, :] = nr_
            him[rows, :] = ni_
            return nr_, ni_

        sr, si = lax.fori_loop(0, steps, step, (str_[...], sti[...]))
        str_[...] = sr
        sti[...] = si
    else:
        for t in range(steps):
            rows = slice(t * n_seq, (t + 1) * n_seq)
            sr, si = str_[...], sti[...]
            nr_ = a_re * sr - a_im * si + hre[rows, :]
            ni_ = a_re * si + a_im * sr + him[rows, :]
            hre[rows, :] = nr_
            him[rows, :] = ni_
            str_[...] = nr_
            sti[...] = ni_

    y = _dot(hre[...].astype(BF16), cre_ref[...]) - _dot(him[...].astype(BF16), cim_ref[...])
    y_ref[...] = y + d_ref[...] * u
    fr_ref[...] = str_[...]
    fi_ref[...] = sti[...]


def _ssm(u_tm, s0r, s0i, lam_re, lam_im, log_dt, b_re, b_im, c_re, c_im, dvec, n_seq, n_steps, chunk):
    rows = chunk * n_seq
    fixed = lambda i: (0, 0)
    sds = jax.ShapeDtypeStruct
    return pl.pallas_call(
        functools.partial(_ssm_kernel, n_seq=n_seq, steps=chunk),
        grid=(n_steps // chunk,),
        in_specs=[pl.BlockSpec((rows, SSM_WIDTH), lambda i: (i, 0)),
                  pl.BlockSpec((n_seq, SSM_LANES), fixed), pl.BlockSpec((n_seq, SSM_LANES), fixed),
                  pl.BlockSpec((1, SSM_LANES), fixed), pl.BlockSpec((1, SSM_LANES), fixed),
                  pl.BlockSpec((1, SSM_LANES), fixed),
                  pl.BlockSpec((SSM_WIDTH, SSM_LANES), fixed), pl.BlockSpec((SSM_WIDTH, SSM_LANES), fixed),
                  pl.BlockSpec((SSM_LANES, SSM_WIDTH), fixed), pl.BlockSpec((SSM_LANES, SSM_WIDTH), fixed),
                  pl.BlockSpec((1, SSM_WIDTH), fixed)],
        out_specs=[pl.BlockSpec((rows, SSM_WIDTH), lambda i: (i, 0)),
                   pl.BlockSpec((n_seq, SSM_LANES), fixed), pl.BlockSpec((n_seq, SSM_LANES), fixed)],
        out_shape=[sds((n_steps * n_seq, SSM_WIDTH), F32), sds((n_seq, SSM_LANES), F32),
                   sds((n_seq, SSM_LANES), F32)],
        scratch_shapes=[pltpu.VMEM((rows, SSM_LANES), F32), pltpu.VMEM((rows, SSM_LANES), F32),
                        pltpu.VMEM((n_seq, SSM_LANES), F32), pltpu.VMEM((n_seq, SSM_LANES), F32)],
        compiler_params=_cparams(("arbitrary",)),
        name=f"ssm_scan_{n_seq}",
    )(u_tm, s0r, s0i, lam_re, lam_im, log_dt, b_re, b_im, c_re, c_im, dvec)


def _mixout_kernel(o0_ref, o1_ref, o2_ref, l0_ref, l1_ref, l2_ref, y_ref, x_ref,
                   ga_ref, gs_ref, wglu_ref, bglu_ref, wout_ref, h_ref):
    l0, l1, l2 = l0_ref[...], l1_ref[...], l2_ref[...]
    m = jnp.maximum(jnp.maximum(l0, l1), l2)
    e0, e1, e2 = jnp.exp(l0 - m), jnp.exp(l1 - m), jnp.exp(l2 - m)
    den = e0 + e1 + e2
    a = [o0_ref[...] * (e0 / den), o1_ref[...] * (e1 / den), o2_ref[...] * (e2 / den)]
    ssq = sum(jnp.sum(ai * ai, axis=-1, keepdims=True) for ai in a)
    inv = lax.rsqrt(ssq * (1.0 / ATTN_WIDTH) + EPS)
    acc = x_ref[...]
    for g in range(N_GROUPS):
        an = (a[g] * inv * ga_ref[:, g * GROUP_W:(g + 1) * GROUP_W]).astype(BF16)
        acc = acc + _dot(an, wout_ref[g * GROUP_W:(g + 1) * GROUP_W, :])
    y = y_ref[...]
    zs = 0.5 * y * (1.0 + jnp.tanh(math.sqrt(2.0 / math.pi) * (y + 0.044715 * (y * y * y))))
    gate = jax.nn.sigmoid(_dot(zs.astype(BF16), wglu_ref[...]) + bglu_ref[...])
    so = _rms_rows(zs * gate, gs_ref[...]).astype(BF16)
    h_ref[...] = acc + _dot(so, wout_ref[ATTN_WIDTH:, :])


def _mix_out(os_, ls_, y, x2d, g_attn, g_ssm, wglu_bf, bglu, wout_bf, tm):
    n = x2d.shape[0]
    row = lambda i: (i, 0)
    fixed = lambda i: (0, 0)
    return pl.pallas_call(
        _mixout_kernel,
        grid=(n // tm,),
        in_specs=[pl.BlockSpec((tm, GROUP_W), row)] * 7 + [pl.BlockSpec((tm, D_MODEL), row)]
                 + [pl.BlockSpec((1, ATTN_WIDTH), fixed), pl.BlockSpec((1, SSM_WIDTH), fixed),
                    pl.BlockSpec((SSM_WIDTH, SSM_WIDTH), fixed), pl.BlockSpec((1, SSM_WIDTH), fixed),
                    pl.BlockSpec((D_MODEL, D_MODEL), fixed)],
        out_specs=pl.BlockSpec((tm, D_MODEL), row),
        out_shape=jax.ShapeDtypeStruct((n, D_MODEL), F32),
        compiler_params=_cparams(("arbitrary",)),
        name="mix_out",
    )(*os_, *ls_, y, x2d, g_attn, g_ssm, wglu_bf, bglu, wout_bf)


def _memkv_kernel(x_ref, g_ref, w_ref, kn_ref, o_ref):
    xn = _rms_rows(x_ref[...], g_ref[...]).astype(BF16)
    kn = kn_ref[...]
    for h in range(X_HEADS):
        cols = slice(h * X_HEAD_DIM, (h + 1) * X_HEAD_DIM)
        o_ref[:, cols] = _rms_rows(_dot(xn, w_ref[:, cols]), kn)
    o_ref[:, D_MODEL:] = _dot(xn, w_ref[:, D_MODEL:])


def _memory_kv(mem2d, gain, w_bf, kn, tm):
    n = mem2d.shape[0]
    fixed = lambda i: (0, 0)
    return pl.pallas_call(
        _memkv_kernel,
        grid=(n // tm,),
        in_specs=[pl.BlockSpec((tm, D_MODEL), lambda i: (i, 0)), pl.BlockSpec((1, D_MODEL), fixed),
                  pl.BlockSpec((D_MODEL, 2 * D_MODEL), fixed), pl.BlockSpec((1, X_HEAD_DIM), fixed)],
        out_specs=pl.BlockSpec((tm, 2 * D_MODEL), lambda i: (i, 0)),
        out_shape=jax.ShapeDtypeStruct((n, 2 * D_MODEL), F32),
        compiler_params=_cparams(("arbitrary",)),
        name="memory_kv",
    )(mem2d, gain, w_bf, kn)


def _cross_kernel(h_ref, mkv_ref, g_ref, wq_ref, qn_ref, wo_ref, o_ref, att, *, seqs, rows_per_seq):
    h = h_ref[...]
    hn = _rms_rows(h, g_ref[...]).astype(BF16)
    qn = qn_ref[...]
    for hd in range(X_HEADS):
        cols = slice(hd * X_HEAD_DIM, (hd + 1) * X_HEAD_DIM)
        vcols = slice(D_MODEL + hd * X_HEAD_DIM, D_MODEL + (hd + 1) * X_HEAD_DIM)
        q = _rms_rows(_dot(hn, wq_ref[:, cols]), qn) * (X_HEAD_DIM ** -0.5)
        for sq in range(seqs):
            rows = slice(sq * rows_per_seq, (sq + 1) * rows_per_seq)
            k = mkv_ref[sq, :, cols].astype(BF16)
            v = mkv_ref[sq, :, vcols].astype(BF16)
            s = _dot_t(q[rows].astype(BF16), k)
            m = jnp.max(s, axis=-1, keepdims=True)
            p = jnp.exp(s - m)
            l = jnp.sum(p, axis=-1, keepdims=True)
            att[rows, cols] = _dot(p.astype(BF16), v) / l
    o_ref[...] = h + _dot(att[...].astype(BF16), wo_ref[...])


def _cross(h2d, mkv, gain, wq_bf, qn, wo_bf, rows_per_seq, tm):
    n = h2d.shape[0]
    if tm <= rows_per_seq:
        seqs, per = 1, tm
        blocks_per_seq = rows_per_seq // tm
        mkv_map = lambda i: (i // blocks_per_seq, 0, 0)
    else:
        seqs, per = tm // rows_per_seq, rows_per_seq
        mkv_map = lambda i: (i, 0, 0)
    fixed = lambda i: (0, 0)
    return pl.pallas_call(
        functools.partial(_cross_kernel, seqs=seqs, rows_per_seq=per),
        grid=(n // tm,),
        in_specs=[pl.BlockSpec((tm, D_MODEL), lambda i: (i, 0)),
                  pl.BlockSpec((seqs, N_MEM, 2 * D_MODEL), mkv_map),
                  pl.BlockSpec((1, D_MODEL), fixed), pl.BlockSpec((D_MODEL, D_MODEL), fixed),
                  pl.BlockSpec((1, X_HEAD_DIM), fixed), pl.BlockSpec((D_MODEL, D_MODEL), fixed)],
        out_specs=pl.BlockSpec((tm, D_MODEL), lambda i: (i, 0)),
        out_shape=jax.ShapeDtypeStruct((n, D_MODEL), F32),
        scratch_shapes=[pltpu.VMEM((tm, D_MODEL), F32)],
        compiler_params=_cparams(("arbitrary",)),
        name=f"cross_attn_{seqs}",
    )(h2d, mkv, gain, wq_bf, qn, wo_bf)


FFN_CHUNKS = 2
FFN_TF = D_FF // FFN_CHUNKS
CARRY_ROWS = 8


def _conv_gate(up_g, up_v, prev_g, prev_v, bufg, bufv, cwg_ref, cwv_ref, cbg_ref, cbv_ref, tm, lag):
    hist = prev_g.shape[0]
    bufg[0:hist] = prev_g
    bufg[hist:hist + tm] = up_g
    bufv[0:hist] = prev_v
    bufv[hist:hist + tm] = up_v
    cg = cbg_ref[...] + (cwg_ref[0:1, :] * bufg[hist - 2 * lag:hist - 2 * lag + tm]
                         + cwg_ref[1:2, :] * bufg[hist - lag:hist - lag + tm] + cwg_ref[2:3, :] * up_g)
    cv = cbv_ref[...] + (cwv_ref[0:1, :] * bufv[hist - 2 * lag:hist - 2 * lag + tm]
                         + cwv_ref[1:2, :] * bufv[hist - lag:hist - lag + tm] + cwv_ref[2:3, :] * up_v)
    return (cg * jax.nn.sigmoid(cg) * cv).astype(BF16)


def _ffn_prompt_kernel(h_ref, g_ref, wg_ref, wv_ref, cwg_ref, cwv_ref, cbg_ref, cbv_ref, wd_ref,
                       o_ref, tg_ref, tv_ref, xn, bufg, bufv, carry, *, tm):
    i = pl.program_id(1)
    c = pl.program_id(2)

    @pl.when(c == 0)
    def _():
        xn[...] = _rms_rows(h_ref[...], g_ref[...]).astype(BF16)

    up_g = _dot(xn[...], wg_ref[...])
    up_v = _dot(xn[...], wv_ref[...])
    @pl.when(i == 0)
    def _():
        carry[c] = jnp.zeros((2, CARRY_ROWS, FFN_TF), F32)

    act = _conv_gate(up_g, up_v, carry[c, 0], carry[c, 1], bufg, bufv, cwg_ref, cwv_ref, cbg_ref, cbv_ref, tm, 1)
    carry[c, 0] = up_g[tm - CARRY_ROWS:]
    carry[c, 1] = up_v[tm - CARRY_ROWS:]
    tg_ref[...] = up_g[tm - CARRY_ROWS:]
    tv_ref[...] = up_v[tm - CARRY_ROWS:]
    contrib = _dot(act, wd_ref[...])

    @pl.when(c == 0)
    def _():
        o_ref[...] = h_ref[...] + contrib

    @pl.when(c > 0)
    def _():
        o_ref[...] += contrib


def _ffn_prompt(h2d, gain, wup_bf, conv_w, conv_b, wdown_bf, batch, seq, tm):
    n_i = seq // tm
    nc, tf = FFN_CHUNKS, FFN_TF
    row = lambda b, i, c: (b * n_i + i, 0)
    gcol = lambda b, i, c: (0, c)
    vcol = lambda b, i, c: (0, nc + c)
    sds = jax.ShapeDtypeStruct
    return pl.pallas_call(
        functools.partial(_ffn_prompt_kernel, tm=tm),
        grid=(batch, n_i, nc),
        in_specs=[pl.BlockSpec((tm, D_MODEL), row), pl.BlockSpec((1, D_MODEL), lambda b, i, c: (0, 0)),
                  pl.BlockSpec((D_MODEL, tf), gcol), pl.BlockSpec((D_MODEL, tf), vcol),
                  pl.BlockSpec((CONV_W, tf), gcol), pl.BlockSpec((CONV_W, tf), vcol),
                  pl.BlockSpec((1, tf), gcol), pl.BlockSpec((1, tf), vcol),
                  pl.BlockSpec((tf, D_MODEL), lambda b, i, c: (c, 0))],
        out_specs=[pl.BlockSpec((tm, D_MODEL), row),
                   pl.BlockSpec((None, None, CARRY_ROWS, tf), lambda b, i, c: (b, i, 0, c)),
                   pl.BlockSpec((None, None, CARRY_ROWS, tf), lambda b, i, c: (b, i, 0, c))],
        out_shape=[sds((batch * seq, D_MODEL), F32), sds((batch, n_i, CARRY_ROWS, D_FF), F32),
                   sds((batch, n_i, CARRY_ROWS, D_FF), F32)],
        scratch_shapes=[pltpu.VMEM((tm, D_MODEL), BF16), pltpu.VMEM((tm + CARRY_ROWS, tf), F32),
                        pltpu.VMEM((tm + CARRY_ROWS, tf), F32), pltpu.VMEM((nc, 2, CARRY_ROWS, tf), F32)],
        compiler_params=_cparams(("arbitrary",) * 3),
        name="ffn_prompt",
    )(h2d, gain, wup_bf, wup_bf, conv_w, conv_w, conv_b, conv_b, wdown_bf)


def _ffn_sample_kernel(h_ref, g_ref, sg_ref, sv_ref, wg_ref, wv_ref, cwg_ref, cwv_ref, cbg_ref, cbv_ref, wd_ref,
                       o_ref, ng_ref, nv_ref, xn, bufg, bufv, *, n_seq, rows):
    c = pl.program_id(0)

    @pl.when(c == 0)
    def _():
        xn[...] = _rms_rows(h_ref[...], g_ref[...]).astype(BF16)

    up_g = _dot(xn[...], wg_ref[...])
    up_v = _dot(xn[...], wv_ref[...])
    act = _conv_gate(up_g, up_v, sg_ref[...], sv_ref[...], bufg, bufv, cwg_ref, cwv_ref, cbg_ref, cbv_ref,
                     rows, n_seq)
    hist = (CONV_W - 1) * n_seq
    ng_ref[...] = bufg[rows:rows + hist]
    nv_ref[...] = bufv[rows:rows + hist]
    contrib = _dot(act, wd_ref[...])

    @pl.when(c == 0)
    def _():
        o_ref[...] = h_ref[...] + contrib

    @pl.when(c > 0)
    def _():
        o_ref[...] += contrib


def _ffn_sample(h_tm, gain, state_tm, wup_bf, conv_w, conv_b, wdown_bf, n_seq, n_steps):
    rows = n_seq * n_steps
    hist = (CONV_W - 1) * n_seq
    nc, tf = FFN_CHUNKS, FFN_TF
    gcol = lambda c: (0, c)
    vcol = lambda c: (0, nc + c)
    fixed = lambda c: (0, 0)
    sds = jax.ShapeDtypeStruct
    return pl.pallas_call(
        functools.partial(_ffn_sample_kernel, n_seq=n_seq, rows=rows),
        grid=(nc,),
        in_specs=[pl.BlockSpec((rows, D_MODEL), fixed), pl.BlockSpec((1, D_MODEL), fixed),
                  pl.BlockSpec((hist, tf), gcol), pl.BlockSpec((hist, tf), vcol),
                  pl.BlockSpec((D_MODEL, tf), gcol), pl.BlockSpec((D_MODEL, tf), vcol),
                  pl.BlockSpec((CONV_W, tf), gcol), pl.BlockSpec((CONV_W, tf), vcol),
                  pl.BlockSpec((1, tf), gcol), pl.BlockSpec((1, tf), vcol),
                  pl.BlockSpec((tf, D_MODEL), lambda c: (c, 0))],
        out_specs=[pl.BlockSpec((rows, D_MODEL), fixed), pl.BlockSpec((hist, tf), gcol),
                   pl.BlockSpec((hist, tf), gcol)],
        out_shape=[sds((rows, D_MODEL), F32), sds((hist, D_FF), F32), sds((hist, D_FF), F32)],
        scratch_shapes=[pltpu.VMEM((rows, D_MODEL), BF16), pltpu.VMEM((rows + hist, tf), F32),
                        pltpu.VMEM((rows + hist, tf), F32)],
        compiler_params=_cparams(("arbitrary",)),
        name="ffn_sample",
    )(h_tm, gain, state_tm, state_tm, wup_bf, wup_bf, conv_w, conv_w, conv_b, conv_b, wdown_bf)


def _rope_tables(pos):
    half = ROT_DIM // 2
    inv = jnp.power(jnp.float32(ROPE_THETA), -jnp.arange(half, dtype=F32) * 2.0 / ROT_DIM)
    ang = pos.astype(F32)[:, None] * inv[None, :]
    cos, sin = jnp.cos(ang), jnp.sin(ang)
    ones = jnp.ones((pos.shape[0], HEAD_DIM - ROT_DIM), F32)
    zeros = jnp.zeros((pos.shape[0], HEAD_DIM - ROT_DIM), F32)
    zh = jnp.zeros_like(sin)
    c_head = jnp.concatenate([cos, cos, ones], axis=1)
    s1_head = jnp.concatenate([-sin, zh, zeros], axis=1)
    s2_head = jnp.concatenate([zh, sin, zeros], axis=1)
    tile = lambda a: jnp.tile(a, (1, HEADS_PER_GROUP))
    return tile(c_head), tile(s1_head), tile(s2_head)


def _block_diag(w):
    g, a, b = w.shape
    eye = jnp.eye(g, dtype=w.dtype)
    return (eye[:, None, :, None] * w[:, :, None, :]).reshape(g * a, g * b)


def _layer_params(l, norm_mix, w_in, q_norm, k_norm, ssm_lam_re, ssm_lam_im, ssm_log_dt, ssm_b_re, ssm_b_im,
                  ssm_c_re, ssm_c_im, ssm_d, w_glu, b_glu, attn_out_norm, ssm_out_norm, w_out, norm_x, norm_mem,
                  w_xq, w_xkv, xq_norm, xk_norm, w_xo, norm_ffn, w_up, conv_w, conv_b, w_down):
    row = lambda a: a.reshape(1, -1).astype(F32)
    return dict(
        norm_mix=row(norm_mix[l]), w_in=w_in[l].astype(BF16),
        q_norm=row(jnp.tile(q_norm[l], HEADS_PER_GROUP)), k_norm=row(jnp.tile(k_norm[l], HEADS_PER_GROUP)),
        lam_re=row(ssm_lam_re[l]), lam_im=row(ssm_lam_im[l]),
        log_dt=row(jnp.repeat(ssm_log_dt[l], SSM_STATE)),
        b_re=_block_diag(jnp.swapaxes(ssm_b_re[l], 1, 2)).astype(BF16),
        b_im=_block_diag(jnp.swapaxes(ssm_b_im[l], 1, 2)).astype(BF16),
        c_re=_block_diag(jnp.swapaxes(ssm_c_re[l], 1, 2)).astype(BF16),
        c_im=_block_diag(jnp.swapaxes(ssm_c_im[l], 1, 2)).astype(BF16),
        ssm_d=row(ssm_d[l]), w_glu=w_glu[l].astype(BF16), b_glu=row(b_glu[l]),
        attn_out_norm=row(attn_out_norm[l]), ssm_out_norm=row(ssm_out_norm[l]), w_out=w_out[l].astype(BF16),
        norm_x=row(norm_x[l]), norm_mem=row(norm_mem[l]), w_xq=w_xq[l].astype(BF16),
        w_xkv=w_xkv[l].astype(BF16), xq_norm=row(xq_norm[l]), xk_norm=row(xk_norm[l]),
        w_xo=w_xo[l].astype(BF16), norm_ffn=row(norm_ffn[l]), w_up=w_up[l].astype(BF16),
        conv_w=conv_w[l].astype(F32), conv_b=row(conv_b[l]), w_down=w_down[l].astype(BF16))


def _to_time_major(a, n_seq, n_steps):
    return a.reshape(n_seq, n_steps, -1).swapaxes(0, 1).reshape(n_steps * n_seq, -1)


def _to_seq_major(a, n_seq, n_steps):
    return a.reshape(n_steps, n_seq, -1).swapaxes(0, 1).reshape(n_seq * n_steps, -1)


def _mixer_front(x2d, p, tables, tm):
    return _in_proj(x2d, p['norm_mix'], p['w_in'], p['q_norm'], p['k_norm'], *tables, tm)


def _run_ssm(u, s0r, s0i, p, n_seq, n_steps, chunk):
    u_tm = _to_time_major(u, n_seq, n_steps)
    y_tm, fr, fi = _ssm(u_tm, s0r, s0i, p['lam_re'], p['lam_im'], p['log_dt'], p['b_re'], p['b_im'],
                        p['c_re'], p['c_im'], p['ssm_d'], n_seq, n_steps, chunk)
    return _to_seq_major(y_tm, n_seq, n_steps), fr, fi


def kernel(x_prompt, x_sample, mem_prompt, cache_kv_dil1, cache_kv_dil4, cache_kv_dil16, state_ssm_re, state_ssm_im, state_ffn_conv, cache_mem_kv, norm_mix, w_in, q_norm, k_norm, ssm_lam_re, ssm_lam_im, ssm_log_dt, ssm_b_re, ssm_b_im, ssm_c_re, ssm_c_im, ssm_d, w_glu, b_glu, attn_out_norm, ssm_out_norm, w_out, norm_x, norm_mem, w_xq, w_xkv, xq_norm, xk_norm, w_xo, norm_ffn, w_up, conv_w, conv_b, w_down):
    bp, lp, _ = x_prompt.shape
    bs, ls, _ = x_sample.shape
    depth = w_in.shape[0]
    past_len = cache_kv_dil16.shape[2]
    caches = (cache_kv_dil1, cache_kv_dil4, cache_kv_dil16)
    tables_p = _rope_tables(jnp.arange(lp))
    tables_s = tuple(jnp.tile(t, (bs, 1)) for t in _rope_tables(past_len + jnp.arange(ls)))

    yp = x_prompt.reshape(bp * lp, D_MODEL)
    ys = x_sample.reshape(bs * ls, D_MODEL)
    kv_p = [[] for _ in range(N_GROUPS)]
    kv_s = [[] for _ in range(N_GROUPS)]
    re_p, im_p, conv_p, mem_p, re_s, im_s, conv_s = [], [], [], [], [], [], []
    for l in range(depth):
        p = _layer_params(l, norm_mix, w_in, q_norm, k_norm, ssm_lam_re, ssm_lam_im, ssm_log_dt, ssm_b_re,
                          ssm_b_im, ssm_c_re, ssm_c_im, ssm_d, w_glu, b_glu, attn_out_norm, ssm_out_norm, w_out,
                          norm_x, norm_mem, w_xq, w_xkv, xq_norm, xk_norm, w_xo, norm_ffn, w_up, conv_w, conv_b,
                          w_down)
        mkv = _memory_kv(mem_prompt.reshape(bp * N_MEM, D_MODEL), p['norm_mem'], p['w_xkv'], p['xk_norm'], 256)
        mkv = mkv.reshape(bp, N_MEM, 2 * D_MODEL)
        mem_p.append(mkv.reshape(bp, N_MEM, 2, X_HEADS, X_HEAD_DIM))
        q0, q1, q2, kv0, kv1, kv2, u = _mixer_front(yp, p, tables_p, 512)
        os_, ls_ = [], []
        for g, (qg, kvg) in enumerate(((q0, kv0), (q1, kv1), (q2, kv2))):
            win, dil = ATTN_GROUPS[g]
            o, lse = _prompt_attn(qg, kvg, bp, lp, dil)
            os_.append(o)
            ls_.append(lse)
            keep = min(win, lp)
            kv_p[g].append(kvg.reshape(bp, lp, 2, HEADS_PER_GROUP, HEAD_DIM)[:, lp - keep:])
        zero_state = jnp.zeros((bp, SSM_LANES), F32)
        y, fr, fi = _run_ssm(u, zero_state, zero_state, p, bp, lp, 128)
        re_p.append(fr.reshape(bp, SSM_GROUPS, SSM_STATE))
        im_p.append(fi.reshape(bp, SSM_GROUPS, SSM_STATE))
        h = _mix_out(os_, ls_, y, yp, p['attn_out_norm'], p['ssm_out_norm'], p['w_glu'], p['b_glu'], p['w_out'], 512)
        h = _cross(h, mkv, p['norm_x'], p['w_xq'], p['xq_norm'], p['w_xo'], lp, 512)
        yp, tg, tv = _ffn_prompt(h, p['norm_ffn'], p['w_up'], p['conv_w'], p['conv_b'], p['w_down'], bp, lp, 512)
        conv_p.append(jnp.concatenate([tg[:, -1], tv[:, -1]], axis=-1)[:, CARRY_ROWS - (CONV_W - 1):])
        q0, q1, q2, kv0, kv1, kv2, u = _mixer_front(ys, p, tables_s, bs * ls)
        os_, ls_ = [], []
        for g, (qg, kvg) in enumerate(((q0, kv0), (q1, kv1), (q2, kv2))):
            win, dil = ATTN_GROUPS[g]
            o, lse = _sample_attn(qg, kvg, caches[g][l], bs, ls, dil)
            os_.append(o)
            ls_.append(lse)
            kv_s[g].append(kvg.reshape(bs, ls, 2, HEADS_PER_GROUP, HEAD_DIM))
        y, fr, fi = _run_ssm(u, state_ssm_re[l].reshape(bs, SSM_LANES), state_ssm_im[l].reshape(bs, SSM_LANES),
                             p, bs, ls, ls)
        re_s.append(fr.reshape(bs, SSM_GROUPS, SSM_STATE))
        im_s.append(fi.reshape(bs, SSM_GROUPS, SSM_STATE))
        h = _mix_out(os_, ls_, y, ys, p['attn_out_norm'], p['ssm_out_norm'], p['w_glu'], p['b_glu'], p['w_out'],
                     bs * ls)
        h = _cross(h, cache_mem_kv[l].reshape(bs, N_MEM, 2 * D_MODEL), p['norm_x'], p['w_xq'], p['xq_norm'],
                   p['w_xo'], ls, 4 * ls)
        state_tm = state_ffn_conv[l].swapaxes(0, 1).reshape((CONV_W - 1) * bs, 2 * D_FF)
        ys_tm, ng, nv = _ffn_sample(_to_time_major(h, bs, ls), p['norm_ffn'], state_tm, p['w_up'], p['conv_w'],
                                    p['conv_b'], p['w_down'], bs, ls)
        ys = _to_seq_major(ys_tm, bs, ls)
        conv_s.append(jnp.concatenate([ng, nv], axis=-1).reshape(CONV_W - 1, bs, 2 * D_FF).swapaxes(0, 1))

    st = jnp.stack
    return (yp.reshape(bp, lp, D_MODEL), ys.reshape(bs, ls, D_MODEL),
            st(kv_p[0]), st(kv_p[1]), st(kv_p[2]), st(re_p), st(im_p), st(conv_p), st(mem_p),
            st(kv_s[0]), st(kv_s[1]), st(kv_s[2]), st(re_s), st(im_s), st(conv_s))
```

```python
import functools
import math

import jax
import jax.numpy as jnp
from jax import lax
from jax.experimental import pallas as pl
from jax.experimental.pallas import tpu as pltpu

F32 = jnp.float32
BF16 = jnp.bfloat16

D_MODEL = 1024
HEAD_DIM = 64
ATTN_GROUPS = ((128, 1), (512, 4), (2048, 16))
HEADS_PER_GROUP = 4
GROUP_W = HEADS_PER_GROUP * HEAD_DIM
N_GROUPS = len(ATTN_GROUPS)
ATTN_WIDTH = N_GROUPS * GROUP_W
SSM_WIDTH = D_MODEL - ATTN_WIDTH
SSM_GROUP = 16
SSM_GROUPS = SSM_WIDTH // SSM_GROUP
SSM_STATE = 64
SSM_LANES = SSM_GROUPS * SSM_STATE
IN_WIDTH = 3 * ATTN_WIDTH + SSM_WIDTH
ROT_DIM = HEAD_DIM // 4
ROPE_THETA = 500000.0
N_MEM = 256
X_HEADS = 4
X_HEAD_DIM = D_MODEL // X_HEADS
D_FF = 2816
CONV_W = 3
EPS = 1e-6
ATTN_BLK = 128
NEG = -1e30

VMEM_LIMIT = 56 * 1024 * 1024


def _cparams(sem):
    return pltpu.CompilerParams(dimension_semantics=sem, vmem_limit_bytes=VMEM_LIMIT)


def _dot(a, b):
    return jnp.dot(a, b, preferred_element_type=F32)


def _dot_t(a, b):
    return lax.dot_general(a, b, (((1,), (1,)), ((), ())), preferred_element_type=F32)


def _rms_rows(x, gain):
    y = x * lax.rsqrt(jnp.mean(x * x, axis=-1, keepdims=True) + EPS)
    return y * gain


LANES = 128
LANE_SPLIT = GROUP_W // LANES


def _halves_spec(n_halves, rows, index_map):
    return pl.BlockSpec((n_halves, rows, LANES), index_map)


def _load_halves(ref, first, rows):
    return jnp.concatenate([ref[first + c, rows, :] for c in range(LANE_SPLIT)], axis=1)


def _store_halves(ref, first, value, rows=slice(None)):
    for c in range(LANE_SPLIT):
        ref[first + c, rows, :] = value[:, c * LANES:(c + 1) * LANES]


def _seg_ones(width, seg):
    r = lax.broadcasted_iota(jnp.int32, (width, width), 0) // seg
    c = lax.broadcasted_iota(jnp.int32, (width, width), 1) // seg
    return (r == c).astype(BF16)


def _seg_rms(z, gain, ones_seg, seg):
    sq = z * z
    hi = sq.astype(BF16)
    lo = (sq - hi.astype(F32)).astype(BF16)
    ssum = _dot(hi, ones_seg) + _dot(lo, ones_seg)
    return z * lax.rsqrt(ssum * (1.0 / seg) + EPS) * gain


def _inproj_kernel(x_ref, g_ref, w_ref, qn_ref, kn_ref, c_ref, s1_ref, s2_ref,
                   q0_ref, q1_ref, q2_ref, kv0_ref, kv1_ref, kv2_ref, u_ref):
    xn = _rms_rows(x_ref[...], g_ref[...]).astype(BF16)
    ones64 = _seg_ones(GROUP_W, HEAD_DIM)
    cosw, sin1, sin2 = c_ref[...], s1_ref[...], s2_ref[...]
    qn, kn = qn_ref[...], kn_ref[...]
    half = ROT_DIM // 2

    def rope(z):
        return z * cosw + pltpu.roll(z, GROUP_W - half, 1) * sin1 + pltpu.roll(z, half, 1) * sin2

    q_refs = (q0_ref, q1_ref, q2_ref)
    kv_refs = (kv0_ref, kv1_ref, kv2_ref)
    for g in range(N_GROUPS):
        zq = _dot(xn, w_ref[:, g * GROUP_W:(g + 1) * GROUP_W])
        zk = _dot(xn, w_ref[:, ATTN_WIDTH + g * GROUP_W:ATTN_WIDTH + (g + 1) * GROUP_W])
        zv = _dot(xn, w_ref[:, 2 * ATTN_WIDTH + g * GROUP_W:2 * ATTN_WIDTH + (g + 1) * GROUP_W])
        _store_halves(q_refs[g], 0, rope(_seg_rms(zq, qn, ones64, HEAD_DIM)) * (HEAD_DIM ** -0.5))
        _store_halves(kv_refs[g], 0, rope(_seg_rms(zk, kn, ones64, HEAD_DIM)))
        _store_halves(kv_refs[g], LANE_SPLIT, zv)
    u_ref[...] = _dot(xn, w_ref[:, 3 * ATTN_WIDTH:])


def _in_proj(x2d, gain, w_bf, qn, kn, cos_t, sin1_t, sin2_t, tm):
    n = x2d.shape[0]
    n_pos = cos_t.shape[0] // tm
    row = lambda i: (i, 0)
    fixed = lambda i: (0, 0)
    pos = lambda i: (i % n_pos, 0)
    sds = jax.ShapeDtypeStruct
    return pl.pallas_call(
        _inproj_kernel,
        grid=(n // tm,),
        in_specs=[pl.BlockSpec((tm, D_MODEL), row), pl.BlockSpec((1, D_MODEL), fixed),
                  pl.BlockSpec((D_MODEL, IN_WIDTH), fixed),
                  pl.BlockSpec((1, GROUP_W), fixed), pl.BlockSpec((1, GROUP_W), fixed),
                  pl.BlockSpec((tm, GROUP_W), pos), pl.BlockSpec((tm, GROUP_W), pos),
                  pl.BlockSpec((tm, GROUP_W), pos)],
        out_specs=[_halves_spec(LANE_SPLIT, tm, lambda i: (0, i, 0))] * 3
                  + [_halves_spec(2 * LANE_SPLIT, tm, lambda i: (0, i, 0))] * 3 + [pl.BlockSpec((tm, SSM_WIDTH), row)],
        out_shape=[sds((LANE_SPLIT, n, LANES), F32)] * 3 + [sds((2 * LANE_SPLIT, n, LANES), F32)] * 3
                  + [sds((n, SSM_WIDTH), F32)],
        compiler_params=_cparams(("arbitrary",)),
        name="in_proj",
    )(x2d, gain, w_bf, qn, kn, cos_t, sin1_t, sin2_t)


def _pattn_kernel(*refs, dil, n_sub, has_prev):
    if has_prev:
        q_ref, kvc_ref, kvp_ref, o_ref, lse_ref = refs
    else:
        q_ref, kvc_ref, o_ref, lse_ref = refs
    n = pl.program_id(1)
    blk = ATTN_BLK
    span = blk * dil
    head = lax.broadcasted_iota(jnp.int32, (1, GROUP_W), 1) // HEAD_DIM
    nk = 2 * blk if has_prev else blk
    qi = lax.broadcasted_iota(jnp.int32, (HEADS_PER_GROUP * blk, nk), 0) % blk
    ki = lax.broadcasted_iota(jnp.int32, (HEADS_PER_GROUP * blk, nk), 1)
    band = ((ki >= qi) & (ki <= qi + blk)) if has_prev else (ki <= qi)

    def sub(j, r):
        rows = pl.ds(j * span + r, blk, stride=dil) if dil > 1 else pl.ds(j * span, blk)
        q = _load_halves(q_ref, 0, rows)
        qs = jnp.concatenate([jnp.where(head == h, q, 0.0) for h in range(HEADS_PER_GROUP)], axis=0).astype(BF16)
        kc, vc = _load_halves(kvc_ref, 0, rows), _load_halves(kvc_ref, LANE_SPLIT, rows)
        if has_prev:
            if j > 0:
                prow = pl.ds((j - 1) * span + r, blk, stride=dil) if dil > 1 else pl.ds((j - 1) * span, blk)
                kp, vp = _load_halves(kvc_ref, 0, prow), _load_halves(kvc_ref, LANE_SPLIT, prow)
                mask = band
            else:
                prow = pl.ds(r, blk, stride=dil) if dil > 1 else pl.ds(0, blk)
                kp, vp = _load_halves(kvp_ref, 0, prow), _load_halves(kvp_ref, LANE_SPLIT, prow)
                mask = band & ((n > 0) | (ki >= blk))
            k = jnp.concatenate([kp, kc], axis=0).astype(BF16)
            v = jnp.concatenate([vp, vc], axis=0).astype(BF16)
        else:
            k, v, mask = kc.astype(BF16), vc.astype(BF16), band
        s = jnp.where(mask, _dot_t(qs, k), NEG)
        m = jnp.max(s, axis=-1, keepdims=True)
        p = jnp.exp(s - m)
        l = jnp.sum(p, axis=-1, keepdims=True)
        o_all = _dot(p.astype(BF16), v) / l
        lse = m + jnp.log(l)
        o = jnp.zeros((blk, GROUP_W), F32)
        lb = jnp.zeros((blk, GROUP_W), F32)
        for h in range(HEADS_PER_GROUP):
            o = o + jnp.where(head == h, o_all[h * blk:(h + 1) * blk], 0.0)
            lb = lb + jnp.where(head == h, lse[h * blk:(h + 1) * blk], 0.0)
        _store_halves(o_ref, 0, o, rows)
        _store_halves(lse_ref, 0, lb, rows)

    for j in range(n_sub):
        if dil <= 4:
            for r in range(dil):
                sub(j, r)
        else:
            def body(r, carry, j=j):
                sub(j, r)
                return carry

            lax.fori_loop(0, dil, body, 0)


def _prompt_attn(q, kv, batch, seq, dil):
    span = ATTN_BLK * dil
    has_prev = seq > span
    n_sub = max(1, 512 // span) if has_prev else 1
    rows = n_sub * span
    steps = seq // rows
    cur = lambda b, n: (0, b * steps + n, 0)
    in_specs = [_halves_spec(LANE_SPLIT, rows, cur), _halves_spec(2 * LANE_SPLIT, rows, cur)]
    args = [q, kv]
    if has_prev:
        in_specs.append(_halves_spec(2 * LANE_SPLIT, span,
                                     lambda b, n: (0, b * (seq // span) + jnp.maximum(n * n_sub - 1, 0), 0)))
        args.append(kv)
    return pl.pallas_call(
        functools.partial(_pattn_kernel, dil=dil, n_sub=n_sub, has_prev=has_prev),
        grid=(batch, steps),
        in_specs=in_specs,
        out_specs=[_halves_spec(LANE_SPLIT, rows, cur)] * 2,
        out_shape=[jax.ShapeDtypeStruct((LANE_SPLIT, batch * seq, LANES), F32)] * 2,
        compiler_params=_cparams(("arbitrary",) * 2),
        name=f"prompt_attn_d{dil}",
    )(*args)


def _sattn_kernel(q_ref, kvn_ref, ct_ref, o_ref, lse_ref, qm, osel, lsel, *, dil, bb_n, t_n):
    n_buf = ct_ref.shape[-1]
    hp = HEADS_PER_GROUP
    head = lax.broadcasted_iota(jnp.int32, (1, GROUP_W), 1) // HEAD_DIM
    sel = head == lax.broadcasted_iota(jnp.int32, (hp, 1), 0)
    row = lax.broadcasted_iota(jnp.int32, (t_n * hp, 1), 0)
    trow = row // hp
    sel_rows = head == row % hp
    pos = lax.broadcasted_iota(jnp.int32, (1, n_buf), 1)
    cmask = (pos >= trow) if dil == 1 else ((pos & (dil - 1)) == trow)
    for bb in range(bb_n):
        for t in range(t_n):
            r = bb * t_n + t
            qm[t * hp:(t + 1) * hp, :] = jnp.where(sel, _load_halves(q_ref, 0, slice(r, r + 1)), 0.0)
        qmb = qm[...].astype(BF16)
        qmr = qmb.astype(F32)
        s_c = jnp.where(cmask, _dot(qmb, ct_ref[bb, 0].astype(BF16)), NEG)
        m = jnp.max(s_c, axis=-1, keepdims=True)
        s_n = []
        for u in range(t_n):
            r = bb * t_n + u
            kn = _load_halves(kvn_ref, 0, slice(r, r + 1)).astype(BF16).astype(F32)
            valid = (trow >= u) if dil == 1 else (trow == u)
            sn = jnp.where(valid, jnp.sum(qmr * kn, axis=-1, keepdims=True), NEG)
            s_n.append(sn)
            m = jnp.maximum(m, sn)
        p_c = jnp.exp(s_c - m)
        l = jnp.sum(p_c, axis=-1, keepdims=True)
        acc = _dot_t(p_c.astype(BF16), ct_ref[bb, 1].astype(BF16))
        for u in range(t_n):
            p_n = jnp.exp(s_n[u] - m)
            l = l + p_n
            r = bb * t_n + u
            vn = _load_halves(kvn_ref, LANE_SPLIT, slice(r, r + 1)).astype(BF16).astype(F32)
            acc = acc + p_n.astype(BF16).astype(F32) * vn
        osel[...] = jnp.where(sel_rows, acc / l, 0.0)
        lsel[...] = jnp.where(sel_rows, m + jnp.log(l), 0.0)
        for t in range(t_n):
            r = slice(bb * t_n + t, bb * t_n + t + 1)
            _store_halves(o_ref, 0, jnp.sum(osel[t * hp:(t + 1) * hp, :], axis=0, keepdims=True), r)
            _store_halves(lse_ref, 0, jnp.sum(lsel[t * hp:(t + 1) * hp, :], axis=0, keepdims=True), r)


def _sample_attn(q, kvn, cache, batch, t_n, dil, bb_n):
    n_buf = cache.shape[1]
    assert n_buf == ATTN_BLK * dil and (dil == 1 or t_n <= dil)
    ct = jnp.transpose(cache, (0, 2, 3, 4, 1)).reshape(batch, 2, GROUP_W, n_buf)
    rows_n = t_n * HEADS_PER_GROUP
    blk = lambda i: (0, i, 0)
    return pl.pallas_call(
        functools.partial(_sattn_kernel, dil=dil, bb_n=bb_n, t_n=t_n),
        grid=(batch // bb_n,),
        in_specs=[_halves_spec(LANE_SPLIT, bb_n * t_n, blk), _halves_spec(2 * LANE_SPLIT, bb_n * t_n, blk),
                  pl.BlockSpec((bb_n, 2, GROUP_W, n_buf), lambda i: (i, 0, 0, 0))],
        out_specs=[_halves_spec(LANE_SPLIT, bb_n * t_n, blk)] * 2,
        out_shape=[jax.ShapeDtypeStruct((LANE_SPLIT, batch * t_n, LANES), F32)] * 2,
        scratch_shapes=[pltpu.VMEM((rows_n, GROUP_W), F32)] * 3,
        compiler_params=_cparams(("arbitrary",)),
        name=f"sample_attn_d{dil}",
    )(q, kvn, ct)


def _ssm_kernel(u_ref, s0r_ref, s0i_ref, lre_ref, lim_ref, ldt_ref, bre_ref, bim_ref, cre_ref, cim_ref, d_ref,
                y_ref, fr_ref, fi_ref, hre, him, str_, sti, *, n_seq, steps):
    i = pl.program_id(0)

    @pl.when(i == 0)
    def _():
        str_[...] = s0r_ref[...]
        sti[...] = s0i_ref[...]

    lam_re, lam_im = lre_ref[...], lim_ref[...]
    dt = jnp.exp(ldt_ref[...])
    mag = jnp.exp(lam_re * dt)
    a_re = mag * jnp.cos(lam_im * dt)
    a_im = mag * jnp.sin(lam_im * dt)
    den = lam_re * lam_re + lam_im * lam_im
    nr, ni = a_re - 1.0, a_im
    c_re = (nr * lam_re + ni * lam_im) / den
    c_im = (ni * lam_re - nr * lam_im) / den

    u = u_ref[...]
    ub = u.astype(BF16)
    bu_re = _dot(ub, bre_ref[...])
    bu_im = _dot(ub, bim_ref[...])
    hre[...] = c_re * bu_re - c_im * bu_im
    him[...] = c_re * bu_im + c_im * bu_re

    if n_seq <= 8:
        def step(t, carry):
            sr, si = carry
            rows = pl.ds(pl.multiple_of(t * n_seq, n_seq), n_seq)
            nr_ = a_re * sr - a_im * si + hre[rows, :]
            ni_ = a_re * si + a_im * sr + him[rows, :]
            hre[rows, :] = nr_
            him[rows, :] = ni_
            return nr_, ni_

        sr, si = lax.fori_loop(0, steps, step, (str_[...], sti[...]))
        str_[...] = sr
        sti[...] = si
    else:
        for t in range(steps):
            rows = slice(t * n_seq, (t + 1) * n_seq)
            sr, si = str_[...], sti[...]
            nr_ = a_re * sr - a_im * si + hre[rows, :]
            ni_ = a_re * si + a_im * sr + him[rows, :]
            hre[rows, :] = nr_
            him[rows, :] = ni_
            str_[...] = nr_
            sti[...] = ni_

    y = _dot(hre[...].astype(BF16), cre_ref[...]) - _dot(him[...].astype(BF16), cim_ref[...])
    y_ref[...] = y + d_ref[...] * u
    fr_ref[...] = str_[...]
    fi_ref[...] = sti[...]


def _ssm(u_tm, s0r, s0i, lam_re, lam_im, log_dt, b_re, b_im, c_re, c_im, dvec, n_seq, n_steps, chunk):
    rows = chunk * n_seq
    fixed = lambda i: (0, 0)
    sds = jax.ShapeDtypeStruct
    return pl.pallas_call(
        functools.partial(_ssm_kernel, n_seq=n_seq, steps=chunk),
        grid=(n_steps // chunk,),
        in_specs=[pl.BlockSpec((rows, SSM_WIDTH), lambda i: (i, 0)),
                  pl.BlockSpec((n_seq, SSM_LANES), fixed), pl.BlockSpec((n_seq, SSM_LANES), fixed),
                  pl.BlockSpec((1, SSM_LANES), fixed), pl.BlockSpec((1, SSM_LANES), fixed),
                  pl.BlockSpec((1, SSM_LANES), fixed),
                  pl.BlockSpec((SSM_WIDTH, SSM_LANES), fixed), pl.BlockSpec((SSM_WIDTH, SSM_LANES), fixed),
                  pl.BlockSpec((SSM_LANES, SSM_WIDTH), fixed), pl.BlockSpec((SSM_LANES, SSM_WIDTH), fixed),
                  pl.BlockSpec((1, SSM_WIDTH), fixed)],
        out_specs=[pl.BlockSpec((rows, SSM_WIDTH), lambda i: (i, 0)),
                   pl.BlockSpec((n_seq, SSM_LANES), fixed), pl.BlockSpec((n_seq, SSM_LANES), fixed)],
        out_shape=[sds((n_steps * n_seq, SSM_WIDTH), F32), sds((n_seq, SSM_LANES), F32),
                   sds((n_seq, SSM_LANES), F32)],
        scratch_shapes=[pltpu.VMEM((rows, SSM_LANES), F32), pltpu.VMEM((rows, SSM_LANES), F32),
                        pltpu.VMEM((n_seq, SSM_LANES), F32), pltpu.VMEM((n_seq, SSM_LANES), F32)],
        compiler_params=_cparams(("arbitrary",)),
        name=f"ssm_scan_{n_seq}",
    )(u_tm, s0r, s0i, lam_re, lam_im, log_dt, b_re, b_im, c_re, c_im, dvec)


def _mixout_kernel(o0_ref, o1_ref, o2_ref, l0_ref, l1_ref, l2_ref, y_ref, x_ref,
                   ga_ref, gs_ref, wglu_ref, bglu_ref, wout_ref, h_ref):
    full = slice(None)
    l0, l1, l2 = (_load_halves(r, 0, full) for r in (l0_ref, l1_ref, l2_ref))
    m = jnp.maximum(jnp.maximum(l0, l1), l2)
    e0, e1, e2 = jnp.exp(l0 - m), jnp.exp(l1 - m), jnp.exp(l2 - m)
    den = e0 + e1 + e2
    a = [_load_halves(r, 0, full) * (e / den) for r, e in ((o0_ref, e0), (o1_ref, e1), (o2_ref, e2))]
    ssq = sum(jnp.sum(ai * ai, axis=-1, keepdims=True) for ai in a)
    inv = lax.rsqrt(ssq * (1.0 / ATTN_WIDTH) + EPS)
    acc = x_ref[...]
    for g in range(N_GROUPS):
        an = (a[g] * inv * ga_ref[:, g * GROUP_W:(g + 1) * GROUP_W]).astype(BF16)
        acc = acc + _dot(an, wout_ref[g * GROUP_W:(g + 1) * GROUP_W, :])
    y = y_ref[...]
    zs = 0.5 * y * (1.0 + jnp.tanh(math.sqrt(2.0 / math.pi) * (y + 0.044715 * (y * y * y))))
    gate = jax.nn.sigmoid(_dot(zs.astype(BF16), wglu_ref[...]) + bglu_ref[...])
    so = _rms_rows(zs * gate, gs_ref[...]).astype(BF16)
    h_ref[...] = acc + _dot(so, wout_ref[ATTN_WIDTH:, :])


def _mix_out(os_, ls_, y, x2d, g_attn, g_ssm, wglu_bf, bglu, wout_bf, tm):
    n = x2d.shape[0]
    row = lambda i: (i, 0)
    fixed = lambda i: (0, 0)
    return pl.pallas_call(
        _mixout_kernel,
        grid=(n // tm,),
        in_specs=[_halves_spec(LANE_SPLIT, tm, lambda i: (0, i, 0))] * 6
                 + [pl.BlockSpec((tm, SSM_WIDTH), row), pl.BlockSpec((tm, D_MODEL), row)]
                 + [pl.BlockSpec((1, ATTN_WIDTH), fixed), pl.BlockSpec((1, SSM_WIDTH), fixed),
                    pl.BlockSpec((SSM_WIDTH, SSM_WIDTH), fixed), pl.BlockSpec((1, SSM_WIDTH), fixed),
                    pl.BlockSpec((D_MODEL, D_MODEL), fixed)],
        out_specs=pl.BlockSpec((tm, D_MODEL), row),
        out_shape=jax.ShapeDtypeStruct((n, D_MODEL), F32),
        compiler_params=_cparams(("arbitrary",)),
        name="mix_out",
    )(*os_, *ls_, y, x2d, g_attn, g_ssm, wglu_bf, bglu, wout_bf)


def _memkv_kernel(x_ref, g_ref, w_ref, kn_ref, o_ref):
    xn = _rms_rows(x_ref[...], g_ref[...]).astype(BF16)
    kn = kn_ref[...]
    for h in range(X_HEADS):
        cols = slice(h * X_HEAD_DIM, (h + 1) * X_HEAD_DIM)
        o_ref[:, cols] = _rms_rows(_dot(xn, w_ref[:, cols]), kn)
    o_ref[:, D_MODEL:] = _dot(xn, w_ref[:, D_MODEL:])


def _memory_kv(mem2d, gain, w_bf, kn, tm):
    n = mem2d.shape[0]
    fixed = lambda i: (0, 0)
    return pl.pallas_call(
        _memkv_kernel,
        grid=(n // tm,),
        in_specs=[pl.BlockSpec((tm, D_MODEL), lambda i: (i, 0)), pl.BlockSpec((1, D_MODEL), fixed),
                  pl.BlockSpec((D_MODEL, 2 * D_MODEL), fixed), pl.BlockSpec((1, X_HEAD_DIM), fixed)],
        out_specs=pl.BlockSpec((tm, 2 * D_MODEL), lambda i: (i, 0)),
        out_shape=jax.ShapeDtypeStruct((n, 2 * D_MODEL), F32),
        compiler_params=_cparams(("arbitrary",)),
        name="memory_kv",
    )(mem2d, gain, w_bf, kn)


def _cross_kernel(h_ref, mkv_ref, g_ref, wq_ref, qn_ref, wo_ref, o_ref, att, *, seqs, rows_per_seq):
    h = h_ref[...]
    hn = _rms_rows(h, g_ref[...]).astype(BF16)
    qn = qn_ref[...]
    for hd in range(X_HEADS):
        cols = slice(hd * X_HEAD_DIM, (hd + 1) * X_HEAD_DIM)
        vcols = slice(D_MODEL + hd * X_HEAD_DIM, D_MODEL + (hd + 1) * X_HEAD_DIM)
        q = _rms_rows(_dot(hn, wq_ref[:, cols]), qn) * (X_HEAD_DIM ** -0.5)
        for sq in range(seqs):
            rows = slice(sq * rows_per_seq, (sq + 1) * rows_per_seq)
            k = mkv_ref[sq, :, cols].astype(BF16)
            v = mkv_ref[sq, :, vcols].astype(BF16)
            s = _dot_t(q[rows].astype(BF16), k)
            m = jnp.max(s, axis=-1, keepdims=True)
            p = jnp.exp(s - m)
            l = jnp.sum(p, axis=-1, keepdims=True)
            att[rows, cols] = _dot(p.astype(BF16), v) / l
    o_ref[...] = h + _dot(att[...].astype(BF16), wo_ref[...])


def _cross(h2d, mkv, gain, wq_bf, qn, wo_bf, rows_per_seq, tm):
    n = h2d.shape[0]
    if tm <= rows_per_seq:
        seqs, per = 1, tm
        blocks_per_seq = rows_per_seq // tm
        mkv_map = lambda i: (i // blocks_per_seq, 0, 0)
    else:
        seqs, per = tm // rows_per_seq, rows_per_seq
        mkv_map = lambda i: (i, 0, 0)
    fixed = lambda i: (0, 0)
    return pl.pallas_call(
        functools.partial(_cross_kernel, seqs=seqs, rows_per_seq=per),
        grid=(n // tm,),
        in_specs=[pl.BlockSpec((tm, D_MODEL), lambda i: (i, 0)),
                  pl.BlockSpec((seqs, N_MEM, 2 * D_MODEL), mkv_map),
                  pl.BlockSpec((1, D_MODEL), fixed), pl.BlockSpec((D_MODEL, D_MODEL), fixed),
                  pl.BlockSpec((1, X_HEAD_DIM), fixed), pl.BlockSpec((D_MODEL, D_MODEL), fixed)],
        out_specs=pl.BlockSpec((tm, D_MODEL), lambda i: (i, 0)),
        out_shape=jax.ShapeDtypeStruct((n, D_MODEL), F32),
        scratch_shapes=[pltpu.VMEM((tm, D_MODEL), F32)],
        compiler_params=_cparams(("arbitrary",)),
        name=f"cross_attn_{seqs}",
    )(h2d, mkv, gain, wq_bf, qn, wo_bf)


MEM_HALVES = X_HEAD_DIM // 128
MEM_ROW_STRIDE = 2 * MEM_HALVES * X_HEADS


def _cross_sample_kernel(h_ref, mem_ref, g_ref, wq_ref, qn_ref, wo_ref, o_ref, qsc, qm, osel, att, *, bb_n, t_n):
    h = h_ref[...]
    hn = _rms_rows(h, g_ref[...]).astype(BF16)
    qn = qn_ref[...]
    for hd in range(X_HEADS):
        cols = slice(hd * X_HEAD_DIM, (hd + 1) * X_HEAD_DIM)
        qsc[:, cols] = _rms_rows(_dot(hn, wq_ref[:, cols]), qn) * (X_HEAD_DIM ** -0.5)
    xh = X_HEADS
    head = lax.broadcasted_iota(jnp.int32, (1, D_MODEL), 1) // X_HEAD_DIM
    sel = head == lax.broadcasted_iota(jnp.int32, (xh, 1), 0)
    sel_rows = head == lax.broadcasted_iota(jnp.int32, (t_n * xh, 1), 0) % xh

    def gather(bb, kv):
        parts = [mem_ref[bb, pl.ds((kv * MEM_HALVES + half) * xh + hd, N_MEM, stride=MEM_ROW_STRIDE), :]
                 for hd in range(xh) for half in range(MEM_HALVES)]
        return jnp.concatenate(parts, axis=1).astype(BF16)

    for bb in range(bb_n):
        for t in range(t_n):
            r = bb * t_n + t
            qm[t * xh:(t + 1) * xh, :] = jnp.where(sel, qsc[r:r + 1, :], 0.0)
        s = _dot_t(qm[...].astype(BF16), gather(bb, 0))
        m = jnp.max(s, axis=-1, keepdims=True)
        p = jnp.exp(s - m)
        l = jnp.sum(p, axis=-1, keepdims=True)
        osel[...] = jnp.where(sel_rows, _dot(p.astype(BF16), gather(bb, 1)) / l, 0.0)
        for t in range(t_n):
            r = bb * t_n + t
            att[r:r + 1, :] = jnp.sum(osel[t * xh:(t + 1) * xh, :], axis=0, keepdims=True)
    o_ref[...] = h + _dot(att[...].astype(BF16), wo_ref[...])


def _cross_sample(h2d, mem_kv, gain, wq_bf, qn, wo_bf, batch, t_n, bb_n):
    mem = mem_kv.reshape(batch, N_MEM, 2, X_HEADS, MEM_HALVES, 128)
    mem = jnp.transpose(mem, (0, 1, 2, 4, 3, 5)).reshape(batch, N_MEM * MEM_ROW_STRIDE, 128)
    rows = bb_n * t_n
    fixed = lambda i: (0, 0)
    return pl.pallas_call(
        functools.partial(_cross_sample_kernel, bb_n=bb_n, t_n=t_n),
        grid=(batch // bb_n,),
        in_specs=[pl.BlockSpec((rows, D_MODEL), lambda i: (i, 0)),
                  pl.BlockSpec((bb_n, N_MEM * MEM_ROW_STRIDE, 128), lambda i: (i, 0, 0)),
                  pl.BlockSpec((1, D_MODEL), fixed), pl.BlockSpec((D_MODEL, D_MODEL), fixed),
                  pl.BlockSpec((1, X_HEAD_DIM), fixed), pl.BlockSpec((D_MODEL, D_MODEL), fixed)],
        out_specs=pl.BlockSpec((rows, D_MODEL), lambda i: (i, 0)),
        out_shape=jax.ShapeDtypeStruct((batch * t_n, D_MODEL), F32),
        scratch_shapes=[pltpu.VMEM((rows, D_MODEL), F32), pltpu.VMEM((t_n * X_HEADS, D_MODEL), F32),
                        pltpu.VMEM((t_n * X_HEADS, D_MODEL), F32), pltpu.VMEM((rows, D_MODEL), F32)],
        compiler_params=_cparams(("arbitrary",)),
        name="cross_attn_sample",
    )(h2d, mem, gain, wq_bf, qn, wo_bf)


FFN_CHUNKS = 2
FFN_TF = D_FF // FFN_CHUNKS
CARRY_ROWS = 8


def _conv_gate(up_g, up_v, prev_g, prev_v, bufg, bufv, cwg_ref, cwv_ref, cbg_ref, cbv_ref, tm, lag):
    hist = prev_g.shape[0]
    bufg[0:hist] = prev_g
    bufg[hist:hist + tm] = up_g
    bufv[0:hist] = prev_v
    bufv[hist:hist + tm] = up_v
    cg = cbg_ref[...] + (cwg_ref[0:1, :] * bufg[hist - 2 * lag:hist - 2 * lag + tm]
                         + cwg_ref[1:2, :] * bufg[hist - lag:hist - lag + tm] + cwg_ref[2:3, :] * up_g)
    cv = cbv_ref[...] + (cwv_ref[0:1, :] * bufv[hist - 2 * lag:hist - 2 * lag + tm]
                         + cwv_ref[1:2, :] * bufv[hist - lag:hist - lag + tm] + cwv_ref[2:3, :] * up_v)
    return (cg * jax.nn.sigmoid(cg) * cv).astype(BF16)


def _ffn_prompt_kernel(h_ref, g_ref, wg_ref, wv_ref, cwg_ref, cwv_ref, cbg_ref, cbv_ref, wd_ref,
                       o_ref, tg_ref, tv_ref, xn, bufg, bufv, carry, *, tm):
    i = pl.program_id(1)
    c = pl.program_id(2)

    @pl.when(c == 0)
    def _():
        xn[...] = _rms_rows(h_ref[...], g_ref[...]).astype(BF16)

    up_g = _dot(xn[...], wg_ref[...])
    up_v = _dot(xn[...], wv_ref[...])
    @pl.when(i == 0)
    def _():
        carry[c] = jnp.zeros((2, CARRY_ROWS, FFN_TF), F32)

    act = _conv_gate(up_g, up_v, carry[c, 0], carry[c, 1], bufg, bufv, cwg_ref, cwv_ref, cbg_ref, cbv_ref, tm, 1)
    carry[c, 0] = up_g[tm - CARRY_ROWS:]
    carry[c, 1] = up_v[tm - CARRY_ROWS:]
    tg_ref[...] = up_g[tm - CARRY_ROWS:]
    tv_ref[...] = up_v[tm - CARRY_ROWS:]
    contrib = _dot(act, wd_ref[...])

    @pl.when(c == 0)
    def _():
        o_ref[...] = h_ref[...] + contrib

    @pl.when(c > 0)
    def _():
        o_ref[...] += contrib


def _ffn_prompt(h2d, gain, wup_bf, conv_w, conv_b, wdown_bf, batch, seq, tm):
    n_i = seq // tm
    nc, tf = FFN_CHUNKS, FFN_TF
    row = lambda b, i, c: (b * n_i + i, 0)
    gcol = lambda b, i, c: (0, c)
    vcol = lambda b, i, c: (0, nc + c)
    sds = jax.ShapeDtypeStruct
    return pl.pallas_call(
        functools.partial(_ffn_prompt_kernel, tm=tm),
        grid=(batch, n_i, nc),
        in_specs=[pl.BlockSpec((tm, D_MODEL), row), pl.BlockSpec((1, D_MODEL), lambda b, i, c: (0, 0)),
                  pl.BlockSpec((D_MODEL, tf), gcol), pl.BlockSpec((D_MODEL, tf), vcol),
                  pl.BlockSpec((CONV_W, tf), gcol), pl.BlockSpec((CONV_W, tf), vcol),
                  pl.BlockSpec((1, tf), gcol), pl.BlockSpec((1, tf), vcol),
                  pl.BlockSpec((tf, D_MODEL), lambda b, i, c: (c, 0))],
        out_specs=[pl.BlockSpec((tm, D_MODEL), row),
                   pl.BlockSpec((None, None, CARRY_ROWS, tf), lambda b, i, c: (b, i, 0, c)),
                   pl.BlockSpec((None, None, CARRY_ROWS, tf), lambda b, i, c: (b, i, 0, c))],
        out_shape=[sds((batch * seq, D_MODEL), F32), sds((batch, n_i, CARRY_ROWS, D_FF), F32),
                   sds((batch, n_i, CARRY_ROWS, D_FF), F32)],
        scratch_shapes=[pltpu.VMEM((tm, D_MODEL), BF16), pltpu.VMEM((tm + CARRY_ROWS, tf), F32),
                        pltpu.VMEM((tm + CARRY_ROWS, tf), F32), pltpu.VMEM((nc, 2, CARRY_ROWS, tf), F32)],
        compiler_params=_cparams(("arbitrary",) * 3),
        name="ffn_prompt",
    )(h2d, gain, wup_bf, wup_bf, conv_w, conv_w, conv_b, conv_b, wdown_bf)


def _ffn_sample_kernel(h_ref, g_ref, sg_ref, sv_ref, wg_ref, wv_ref, cwg_ref, cwv_ref, cbg_ref, cbv_ref, wd_ref,
                       o_ref, ng_ref, nv_ref, xn, bufg, bufv, *, n_seq, rows):
    c = pl.program_id(0)

    @pl.when(c == 0)
    def _():
        xn[...] = _rms_rows(h_ref[...], g_ref[...]).astype(BF16)

    up_g = _dot(xn[...], wg_ref[...])
    up_v = _dot(xn[...], wv_ref[...])
    act = _conv_gate(up_g, up_v, sg_ref[...], sv_ref[...], bufg, bufv, cwg_ref, cwv_ref, cbg_ref, cbv_ref,
                     rows, n_seq)
    hist = (CONV_W - 1) * n_seq
    ng_ref[...] = bufg[rows:rows + hist]
    nv_ref[...] = bufv[rows:rows + hist]
    contrib = _dot(act, wd_ref[...])

    @pl.when(c == 0)
    def _():
        o_ref[...] = h_ref[...] + contrib

    @pl.when(c > 0)
    def _():
        o_ref[...] += contrib


def _ffn_sample(h_tm, gain, state_tm, wup_bf, conv_w, conv_b, wdown_bf, n_seq, n_steps):
    rows = n_seq * n_steps
    hist = (CONV_W - 1) * n_seq
    nc, tf = FFN_CHUNKS, FFN_TF
    gcol = lambda c: (0, c)
    vcol = lambda c: (0, nc + c)
    fixed = lambda c: (0, 0)
    sds = jax.ShapeDtypeStruct
    return pl.pallas_call(
        functools.partial(_ffn_sample_kernel, n_seq=n_seq, rows=rows),
        grid=(nc,),
        in_specs=[pl.BlockSpec((rows, D_MODEL), fixed), pl.BlockSpec((1, D_MODEL), fixed),
                  pl.BlockSpec((hist, tf), gcol), pl.BlockSpec((hist, tf), vcol),
                  pl.BlockSpec((D_MODEL, tf), gcol), pl.BlockSpec((D_MODEL, tf), vcol),
                  pl.BlockSpec((CONV_W, tf), gcol), pl.BlockSpec((CONV_W, tf), vcol),
                  pl.BlockSpec((1, tf), gcol), pl.BlockSpec((1, tf), vcol),
                  pl.BlockSpec((tf, D_MODEL), lambda c: (c, 0))],
        out_specs=[pl.BlockSpec((rows, D_MODEL), fixed), pl.BlockSpec((hist, tf), gcol),
                   pl.BlockSpec((hist, tf), gcol)],
        out_shape=[sds((rows, D_MODEL), F32), sds((hist, D_FF), F32), sds((hist, D_FF), F32)],
        scratch_shapes=[pltpu.VMEM((rows, D_MODEL), BF16), pltpu.VMEM((rows + hist, tf), F32),
                        pltpu.VMEM((rows + hist, tf), F32)],
        compiler_params=_cparams(("arbitrary",)),
        name="ffn_sample",
    )(h_tm, gain, state_tm, state_tm, wup_bf, wup_bf, conv_w, conv_w, conv_b, conv_b, wdown_bf)


def _rope_tables(pos):
    half = ROT_DIM // 2
    inv = jnp.power(jnp.float32(ROPE_THETA), -jnp.arange(half, dtype=F32) * 2.0 / ROT_DIM)
    ang = pos.astype(F32)[:, None] * inv[None, :]
    cos, sin = jnp.cos(ang), jnp.sin(ang)
    ones = jnp.ones((pos.shape[0], HEAD_DIM - ROT_DIM), F32)
    zeros = jnp.zeros((pos.shape[0], HEAD_DIM - ROT_DIM), F32)
    zh = jnp.zeros_like(sin)
    c_head = jnp.concatenate([cos, cos, ones], axis=1)
    s1_head = jnp.concatenate([-sin, zh, zeros], axis=1)
    s2_head = jnp.concatenate([zh, sin, zeros], axis=1)
    tile = lambda a: jnp.tile(a, (1, HEADS_PER_GROUP))
    return tile(c_head), tile(s1_head), tile(s2_head)


def _block_diag(w):
    g, a, b = w.shape
    eye = jnp.eye(g, dtype=w.dtype)
    return (eye[:, None, :, None] * w[:, :, None, :]).reshape(g * a, g * b)


def _layer_params(l, norm_mix, w_in, q_norm, k_norm, ssm_lam_re, ssm_lam_im, ssm_log_dt, ssm_b_re, ssm_b_im,
                  ssm_c_re, ssm_c_im, ssm_d, w_glu, b_glu, attn_out_norm, ssm_out_norm, w_out, norm_x, norm_mem,
                  w_xq, w_xkv, xq_norm, xk_norm, w_xo, norm_ffn, w_up, conv_w, conv_b, w_down):
    row = lambda a: a.reshape(1, -1).astype(F32)
    return dict(
        norm_mix=row(norm_mix[l]), w_in=w_in[l].astype(BF16),
        q_norm=row(jnp.tile(q_norm[l], HEADS_PER_GROUP)), k_norm=row(jnp.tile(k_norm[l], HEADS_PER_GROUP)),
        lam_re=row(ssm_lam_re[l]), lam_im=row(ssm_lam_im[l]),
        log_dt=row(jnp.repeat(ssm_log_dt[l], SSM_STATE)),
        b_re=_block_diag(jnp.swapaxes(ssm_b_re[l], 1, 2)).astype(BF16),
        b_im=_block_diag(jnp.swapaxes(ssm_b_im[l], 1, 2)).astype(BF16),
        c_re=_block_diag(jnp.swapaxes(ssm_c_re[l], 1, 2)).astype(BF16),
        c_im=_block_diag(jnp.swapaxes(ssm_c_im[l], 1, 2)).astype(BF16),
        ssm_d=row(ssm_d[l]), w_glu=w_glu[l].astype(BF16), b_glu=row(b_glu[l]),
        attn_out_norm=row(attn_out_norm[l]), ssm_out_norm=row(ssm_out_norm[l]), w_out=w_out[l].astype(BF16),
        norm_x=row(norm_x[l]), norm_mem=row(norm_mem[l]), w_xq=w_xq[l].astype(BF16),
        w_xkv=w_xkv[l].astype(BF16), xq_norm=row(xq_norm[l]), xk_norm=row(xk_norm[l]),
        w_xo=w_xo[l].astype(BF16), norm_ffn=row(norm_ffn[l]), w_up=w_up[l].astype(BF16),
        conv_w=conv_w[l].astype(F32), conv_b=row(conv_b[l]), w_down=w_down[l].astype(BF16))


def _kv_rows(kv_halves, n_seq, n_steps):
    heads_per_half = LANES // HEAD_DIM
    a = kv_halves.reshape(2, LANE_SPLIT, n_seq, n_steps, heads_per_half, HEAD_DIM)
    return jnp.transpose(a, (2, 3, 0, 1, 4, 5)).reshape(n_seq, n_steps, 2, HEADS_PER_GROUP, HEAD_DIM)


def _to_time_major(a, n_seq, n_steps):
    return a.reshape(n_seq, n_steps, -1).swapaxes(0, 1).reshape(n_steps * n_seq, -1)


def _to_seq_major(a, n_seq, n_steps):
    return a.reshape(n_steps, n_seq, -1).swapaxes(0, 1).reshape(n_seq * n_steps, -1)


def _mixer_front(x2d, p, tables, tm):
    return _in_proj(x2d, p['norm_mix'], p['w_in'], p['q_norm'], p['k_norm'], *tables, tm)


def _run_ssm(u, s0r, s0i, p, n_seq, n_steps, chunk):
    u_tm = _to_time_major(u, n_seq, n_steps)
    y_tm, fr, fi = _ssm(u_tm, s0r, s0i, p['lam_re'], p['lam_im'], p['log_dt'], p['b_re'], p['b_im'],
                        p['c_re'], p['c_im'], p['ssm_d'], n_seq, n_steps, chunk)
    return _to_seq_major(y_tm, n_seq, n_steps), fr, fi


def kernel(x_prompt, x_sample, mem_prompt, cache_kv_dil1, cache_kv_dil4, cache_kv_dil16, state_ssm_re, state_ssm_im, state_ffn_conv, cache_mem_kv, norm_mix, w_in, q_norm, k_norm, ssm_lam_re, ssm_lam_im, ssm_log_dt, ssm_b_re, ssm_b_im, ssm_c_re, ssm_c_im, ssm_d, w_glu, b_glu, attn_out_norm, ssm_out_norm, w_out, norm_x, norm_mem, w_xq, w_xkv, xq_norm, xk_norm, w_xo, norm_ffn, w_up, conv_w, conv_b, w_down):
    bp, lp, _ = x_prompt.shape
    bs, ls, _ = x_sample.shape
    depth = w_in.shape[0]
    past_len = cache_kv_dil16.shape[2]
    caches = (cache_kv_dil1, cache_kv_dil4, cache_kv_dil16)
    tables_p = _rope_tables(jnp.arange(lp))
    tables_s = tuple(jnp.tile(t, (bs, 1)) for t in _rope_tables(past_len + jnp.arange(ls)))

    yp = x_prompt.reshape(bp * lp, D_MODEL)
    ys = x_sample.reshape(bs * ls, D_MODEL)
    kv_p = [[] for _ in range(N_GROUPS)]
    kv_s = [[] for _ in range(N_GROUPS)]
    re_p, im_p, conv_p, mem_p, re_s, im_s, conv_s = [], [], [], [], [], [], []
    for l in range(depth):
        p = _layer_params(l, norm_mix, w_in, q_norm, k_norm, ssm_lam_re, ssm_lam_im, ssm_log_dt, ssm_b_re,
                          ssm_b_im, ssm_c_re, ssm_c_im, ssm_d, w_glu, b_glu, attn_out_norm, ssm_out_norm, w_out,
                          norm_x, norm_mem, w_xq, w_xkv, xq_norm, xk_norm, w_xo, norm_ffn, w_up, conv_w, conv_b,
                          w_down)
        mkv = _memory_kv(mem_prompt.reshape(bp * N_MEM, D_MODEL), p['norm_mem'], p['w_xkv'], p['xk_norm'], 256)
        mkv = mkv.reshape(bp, N_MEM, 2 * D_MODEL)
        mem_p.append(mkv.reshape(bp, N_MEM, 2, X_HEADS, X_HEAD_DIM))
        q0, q1, q2, kv0, kv1, kv2, u = _mixer_front(yp, p, tables_p, 512)
        os_, ls_ = [], []
        for g, (qg, kvg) in enumerate(((q0, kv0), (q1, kv1), (q2, kv2))):
            win, dil = ATTN_GROUPS[g]
            o, lse = _prompt_attn(qg, kvg, bp, lp, dil)
            os_.append(o)
            ls_.append(lse)
            keep = min(win, lp)
            kv_p[g].append(_kv_rows(kvg, bp, lp)[:, lp - keep:])
        zero_state = jnp.zeros((bp, SSM_LANES), F32)
        y, fr, fi = _run_ssm(u, zero_state, zero_state, p, bp, lp, 128)
        re_p.append(fr.reshape(bp, SSM_GROUPS, SSM_STATE))
        im_p.append(fi.reshape(bp, SSM_GROUPS, SSM_STATE))
        h = _mix_out(os_, ls_, y, yp, p['attn_out_norm'], p['ssm_out_norm'], p['w_glu'], p['b_glu'], p['w_out'], 512)
        h = _cross(h, mkv, p['norm_x'], p['w_xq'], p['xq_norm'], p['w_xo'], lp, 512)
        yp, tg, tv = _ffn_prompt(h, p['norm_ffn'], p['w_up'], p['conv_w'], p['conv_b'], p['w_down'], bp, lp, 512)
        conv_p.append(jnp.concatenate([tg[:, -1], tv[:, -1]], axis=-1)[:, CARRY_ROWS - (CONV_W - 1):])
        q0, q1, q2, kv0, kv1, kv2, u = _mixer_front(ys, p, tables_s, bs * ls)
        os_, ls_ = [], []
        for g, (qg, kvg) in enumerate(((q0, kv0), (q1, kv1), (q2, kv2))):
            win, dil = ATTN_GROUPS[g]
            o, lse = _sample_attn(qg, kvg, caches[g][l], bs, ls, dil, max(1, 32 // dil))
            os_.append(o)
            ls_.append(lse)
            kv_s[g].append(_kv_rows(kvg, bs, ls))
        y, fr, fi = _run_ssm(u, state_ssm_re[l].reshape(bs, SSM_LANES), state_ssm_im[l].reshape(bs, SSM_LANES),
                             p, bs, ls, ls)
        re_s.append(fr.reshape(bs, SSM_GROUPS, SSM_STATE))
        im_s.append(fi.reshape(bs, SSM_GROUPS, SSM_STATE))
        h = _mix_out(os_, ls_, y, ys, p['attn_out_norm'], p['ssm_out_norm'], p['w_glu'], p['b_glu'], p['w_out'],
                     bs * ls)
        h = _cross_sample(h, cache_mem_kv[l], p['norm_x'], p['w_xq'], p['xq_norm'], p['w_xo'], bs, ls, 4)
        state_tm = state_ffn_conv[l].swapaxes(0, 1).reshape((CONV_W - 1) * bs, 2 * D_FF)
        ys_tm, ng, nv = _ffn_sample(_to_time_major(h, bs, ls), p['norm_ffn'], state_tm, p['w_up'], p['conv_w'],
                                    p['conv_b'], p['w_down'], bs, ls)
        ys = _to_seq_major(ys_tm, bs, ls)
        conv_s.append(jnp.concatenate([ng, nv], axis=-1).reshape(CONV_W - 1, bs, 2 * D_FF).swapaxes(0, 1))

    st = jnp.stack
    return (yp.reshape(bp, lp, D_MODEL), ys.reshape(bs, ls, D_MODEL),
            st(kv_p[0]), st(kv_p[1]), st(kv_p[2]), st(re_p), st(im_p), st(conv_p), st(mem_p),
            st(kv_s[0]), st(kv_s[1]), st(kv_s[2]), st(re_s), st(im_s), st(conv_s))
```

```python
import functools
import math

import jax
import jax.numpy as jnp
from jax import lax
from jax.experimental import pallas as pl
from jax.experimental.pallas import tpu as pltpu

F32 = jnp.float32
BF16 = jnp.bfloat16

D_MODEL = 1024
HEAD_DIM = 64
ATTN_GROUPS = ((128, 1), (512, 4), (2048, 16))
HEADS_PER_GROUP = 4
GROUP_W = HEADS_PER_GROUP * HEAD_DIM
N_GROUPS = len(ATTN_GROUPS)
ATTN_WIDTH = N_GROUPS * GROUP_W
SSM_WIDTH = D_MODEL - ATTN_WIDTH
SSM_GROUP = 16
SSM_GROUPS = SSM_WIDTH // SSM_GROUP
SSM_STATE = 64
SSM_LANES = SSM_GROUPS * SSM_STATE
IN_WIDTH = 3 * ATTN_WIDTH + SSM_WIDTH
ROT_DIM = HEAD_DIM // 4
ROPE_THETA = 500000.0
N_MEM = 256
X_HEADS = 4
X_HEAD_DIM = D_MODEL // X_HEADS
D_FF = 2816
CONV_W = 3
EPS = 1e-6
ATTN_BLK = 128
NEG = -1e30

VMEM_LIMIT = 56 * 1024 * 1024


def _cparams(sem):
    return pltpu.CompilerParams(dimension_semantics=sem, vmem_limit_bytes=VMEM_LIMIT)


def _dot(a, b):
    return jnp.dot(a, b, preferred_element_type=F32)


def _dot_t(a, b):
    return lax.dot_general(a, b, (((1,), (1,)), ((), ())), preferred_element_type=F32)


def _rms_rows(x, gain):
    y = x * lax.rsqrt(jnp.mean(x * x, axis=-1, keepdims=True) + EPS)
    return y * gain


LANES = 128
LANE_SPLIT = GROUP_W // LANES


def _halves_spec(n_halves, rows, index_map):
    return pl.BlockSpec((n_halves, rows, LANES), index_map)


def _load_halves(ref, first, rows):
    return jnp.concatenate([ref[first + c, rows, :] for c in range(LANE_SPLIT)], axis=1)


def _store_halves(ref, first, value, rows=slice(None)):
    for c in range(LANE_SPLIT):
        ref[first + c, rows, :] = value[:, c * LANES:(c + 1) * LANES]


def _head_ones():
    r = lax.broadcasted_iota(jnp.int32, (GROUP_W, GROUP_W), 0) // HEAD_DIM
    c = lax.broadcasted_iota(jnp.int32, (GROUP_W, GROUP_W), 1) // HEAD_DIM
    return (r == c).astype(BF16)


def _head_rms(z, gain, ones_head):
    sq = z * z
    hi = sq.astype(BF16)
    lo = (sq - hi.astype(F32)).astype(BF16)
    ssum = _dot(hi, ones_head) + _dot(lo, ones_head)
    return z * lax.rsqrt(ssum * (1.0 / HEAD_DIM) + EPS) * gain


def _inproj_kernel(x_ref, g_ref, w_ref, qn_ref, kn_ref, c_ref, s1_ref, s2_ref,
                   q0_ref, q1_ref, q2_ref, kv0_ref, kv1_ref, kv2_ref, u_ref):
    xn = _rms_rows(x_ref[...], g_ref[...]).astype(BF16)
    ones_head = _head_ones()
    cosw, sin1, sin2 = c_ref[...], s1_ref[...], s2_ref[...]
    qn, kn = qn_ref[...], kn_ref[...]
    half = ROT_DIM // 2

    def rope(z):
        return z * cosw + pltpu.roll(z, GROUP_W - half, 1) * sin1 + pltpu.roll(z, half, 1) * sin2

    q_refs = (q0_ref, q1_ref, q2_ref)
    kv_refs = (kv0_ref, kv1_ref, kv2_ref)
    for g in range(N_GROUPS):
        zq = _dot(xn, w_ref[:, g * GROUP_W:(g + 1) * GROUP_W])
        zk = _dot(xn, w_ref[:, ATTN_WIDTH + g * GROUP_W:ATTN_WIDTH + (g + 1) * GROUP_W])
        zv = _dot(xn, w_ref[:, 2 * ATTN_WIDTH + g * GROUP_W:2 * ATTN_WIDTH + (g + 1) * GROUP_W])
        _store_halves(q_refs[g], 0, rope(_head_rms(zq, qn, ones_head)) * (HEAD_DIM ** -0.5))
        _store_halves(kv_refs[g], 0, rope(_head_rms(zk, kn, ones_head)))
        _store_halves(kv_refs[g], LANE_SPLIT, zv)
    u_ref[...] = _dot(xn, w_ref[:, 3 * ATTN_WIDTH:])


def _in_proj(x2d, gain, w_bf, qn, kn, cos_t, sin1_t, sin2_t, tm):
    n = x2d.shape[0]
    n_pos = cos_t.shape[0] // tm
    row = lambda i: (i, 0)
    fixed = lambda i: (0, 0)
    pos = lambda i: (i % n_pos, 0)
    sds = jax.ShapeDtypeStruct
    return pl.pallas_call(
        _inproj_kernel,
        grid=(n // tm,),
        in_specs=[pl.BlockSpec((tm, D_MODEL), row), pl.BlockSpec((1, D_MODEL), fixed),
                  pl.BlockSpec((D_MODEL, IN_WIDTH), fixed),
                  pl.BlockSpec((1, GROUP_W), fixed), pl.BlockSpec((1, GROUP_W), fixed),
                  pl.BlockSpec((tm, GROUP_W), pos), pl.BlockSpec((tm, GROUP_W), pos),
                  pl.BlockSpec((tm, GROUP_W), pos)],
        out_specs=[_halves_spec(LANE_SPLIT, tm, lambda i: (0, i, 0))] * 3
                  + [_halves_spec(2 * LANE_SPLIT, tm, lambda i: (0, i, 0))] * 3 + [pl.BlockSpec((tm, SSM_WIDTH), row)],
        out_shape=[sds((LANE_SPLIT, n, LANES), F32)] * 3 + [sds((2 * LANE_SPLIT, n, LANES), F32)] * 3
                  + [sds((n, SSM_WIDTH), F32)],
        compiler_params=_cparams(("arbitrary",)),
        name="in_proj",
    )(x2d, gain, w_bf, qn, kn, cos_t, sin1_t, sin2_t)


def _pattn_kernel(*refs, dil, n_sub, has_prev):
    if has_prev:
        q_ref, kvc_ref, kvp_ref, o_ref, lse_ref = refs
    else:
        q_ref, kvc_ref, o_ref, lse_ref = refs
    n = pl.program_id(1)
    blk = ATTN_BLK
    span = blk * dil
    head = lax.broadcasted_iota(jnp.int32, (1, GROUP_W), 1) // HEAD_DIM
    nk = 2 * blk if has_prev else blk
    qi = lax.broadcasted_iota(jnp.int32, (HEADS_PER_GROUP * blk, nk), 0) % blk
    ki = lax.broadcasted_iota(jnp.int32, (HEADS_PER_GROUP * blk, nk), 1)
    band = ((ki >= qi) & (ki <= qi + blk)) if has_prev else (ki <= qi)
    band_first = (band & ((n > 0) | (ki >= blk))) if has_prev else band

    def sub(j, r):
        rows = pl.ds(j * span + r, blk, stride=dil) if dil > 1 else pl.ds(j * span, blk)
        q = _load_halves(q_ref, 0, rows)
        qs = jnp.concatenate([jnp.where(head == h, q, 0.0) for h in range(HEADS_PER_GROUP)], axis=0).astype(BF16)
        kc, vc = _load_halves(kvc_ref, 0, rows), _load_halves(kvc_ref, LANE_SPLIT, rows)
        if has_prev:
            if j > 0:
                prow = pl.ds((j - 1) * span + r, blk, stride=dil) if dil > 1 else pl.ds((j - 1) * span, blk)
                kp, vp = _load_halves(kvc_ref, 0, prow), _load_halves(kvc_ref, LANE_SPLIT, prow)
                mask = band
            else:
                prow = pl.ds(r, blk, stride=dil) if dil > 1 else pl.ds(0, blk)
                kp, vp = _load_halves(kvp_ref, 0, prow), _load_halves(kvp_ref, LANE_SPLIT, prow)
                mask = band_first
            k = jnp.concatenate([kp, kc], axis=0).astype(BF16)
            v = jnp.concatenate([vp, vc], axis=0).astype(BF16)
        else:
            k, v, mask = kc.astype(BF16), vc.astype(BF16), band
        s = jnp.where(mask, _dot_t(qs, k), NEG)
        m = jnp.max(s, axis=-1, keepdims=True)
        p = jnp.exp(s - m)
        l = jnp.sum(p, axis=-1, keepdims=True)
        pv = _dot(p.astype(BF16), v)

        def pick(x):
            out = x[(HEADS_PER_GROUP - 1) * blk:]
            for h in range(HEADS_PER_GROUP - 2, -1, -1):
                out = jnp.where(head == h, x[h * blk:(h + 1) * blk], out)
            return out

        l_c = pick(l)
        _store_halves(o_ref, 0, pick(pv) / l_c, rows)
        _store_halves(lse_ref, 0, pick(m) + jnp.log(l_c), rows)

    for j in range(n_sub):
        if dil <= 4:
            for r in range(dil):
                sub(j, r)
        else:
            unroll = 8

            def body(it, carry, j=j):
                for u in range(unroll):
                    sub(j, it * unroll + u)
                return carry

            lax.fori_loop(0, dil // unroll, body, 0)


def _prompt_attn(q, kv, batch, seq, dil):
    span = ATTN_BLK * dil
    has_prev = seq > span
    n_sub = max(1, 512 // span) if has_prev else 1
    rows = n_sub * span
    steps = seq // rows
    cur = lambda b, n: (0, b * steps + n, 0)
    in_specs = [_halves_spec(LANE_SPLIT, rows, cur), _halves_spec(2 * LANE_SPLIT, rows, cur)]
    args = [q, kv]
    if has_prev:
        in_specs.append(_halves_spec(2 * LANE_SPLIT, span,
                                     lambda b, n: (0, b * (seq // span) + jnp.maximum(n * n_sub - 1, 0), 0)))
        args.append(kv)
    return pl.pallas_call(
        functools.partial(_pattn_kernel, dil=dil, n_sub=n_sub, has_prev=has_prev),
        grid=(batch, steps),
        in_specs=in_specs,
        out_specs=[_halves_spec(LANE_SPLIT, rows, cur)] * 2,
        out_shape=[jax.ShapeDtypeStruct((LANE_SPLIT, batch * seq, LANES), F32)] * 2,
        compiler_params=_cparams(("arbitrary",) * 2),
        name=f"prompt_attn_d{dil}",
    )(*args)


def _sattn_kernel(q_ref, kvn_ref, ct_ref, o_ref, lse_ref, qm, osel, lsel, *, dil, bb_n, t_n):
    n_buf = ct_ref.shape[-1]
    hp = HEADS_PER_GROUP
    head = lax.broadcasted_iota(jnp.int32, (1, GROUP_W), 1) // HEAD_DIM
    sel = head == lax.broadcasted_iota(jnp.int32, (hp, 1), 0)
    row = lax.broadcasted_iota(jnp.int32, (t_n * hp, 1), 0)
    trow = row // hp
    sel_rows = head == row % hp
    pos = lax.broadcasted_iota(jnp.int32, (1, n_buf), 1)
    cmask = (pos >= trow) if dil == 1 else ((pos & (dil - 1)) == trow)
    for bb in range(bb_n):
        for t in range(t_n):
            r = bb * t_n + t
            qm[t * hp:(t + 1) * hp, :] = jnp.where(sel, _load_halves(q_ref, 0, slice(r, r + 1)), 0.0)
        qmb = qm[...].astype(BF16)
        qmr = qmb.astype(F32)
        s_c = jnp.where(cmask, _dot(qmb, ct_ref[bb, 0].astype(BF16)), NEG)
        m = jnp.max(s_c, axis=-1, keepdims=True)
        s_n = []
        for u in range(t_n):
            r = bb * t_n + u
            kn = _load_halves(kvn_ref, 0, slice(r, r + 1)).astype(BF16).astype(F32)
            valid = (trow >= u) if dil == 1 else (trow == u)
            sn = jnp.where(valid, jnp.sum(qmr * kn, axis=-1, keepdims=True), NEG)
            s_n.append(sn)
            m = jnp.maximum(m, sn)
        p_c = jnp.exp(s_c - m)
        l = jnp.sum(p_c, axis=-1, keepdims=True)
        acc = _dot_t(p_c.astype(BF16), ct_ref[bb, 1].astype(BF16))
        for u in range(t_n):
            p_n = jnp.exp(s_n[u] - m)
            l = l + p_n
            r = bb * t_n + u
            vn = _load_halves(kvn_ref, LANE_SPLIT, slice(r, r + 1)).astype(BF16).astype(F32)
            acc = acc + p_n.astype(BF16).astype(F32) * vn
        osel[...] = jnp.where(sel_rows, acc / l, 0.0)
        lsel[...] = jnp.where(sel_rows, m + jnp.log(l), 0.0)
        for t in range(t_n):
            r = slice(bb * t_n + t, bb * t_n + t + 1)
            _store_halves(o_ref, 0, jnp.sum(osel[t * hp:(t + 1) * hp, :], axis=0, keepdims=True), r)
            _store_halves(lse_ref, 0, jnp.sum(lsel[t * hp:(t + 1) * hp, :], axis=0, keepdims=True), r)


def _sample_attn(q, kvn, cache, batch, t_n, dil, bb_n):
    n_buf = cache.shape[1]
    assert n_buf == ATTN_BLK * dil and (dil == 1 or t_n <= dil)
    ct = jnp.transpose(cache, (0, 2, 3, 4, 1)).reshape(batch, 2, GROUP_W, n_buf)
    rows_n = t_n * HEADS_PER_GROUP
    blk = lambda i: (0, i, 0)
    return pl.pallas_call(
        functools.partial(_sattn_kernel, dil=dil, bb_n=bb_n, t_n=t_n),
        grid=(batch // bb_n,),
        in_specs=[_halves_spec(LANE_SPLIT, bb_n * t_n, blk), _halves_spec(2 * LANE_SPLIT, bb_n * t_n, blk),
                  pl.BlockSpec((bb_n, 2, GROUP_W, n_buf), lambda i: (i, 0, 0, 0))],
        out_specs=[_halves_spec(LANE_SPLIT, bb_n * t_n, blk)] * 2,
        out_shape=[jax.ShapeDtypeStruct((LANE_SPLIT, batch * t_n, LANES), F32)] * 2,
        scratch_shapes=[pltpu.VMEM((rows_n, GROUP_W), F32)] * 3,
        compiler_params=_cparams(("arbitrary",)),
        name=f"sample_attn_d{dil}",
    )(q, kvn, ct)


def _ssm_kernel(u_ref, s0r_ref, s0i_ref, lre_ref, lim_ref, ldt_ref, bre_ref, bim_ref, cre_ref, cim_ref, d_ref,
                y_ref, fr_ref, fi_ref, hre, him, str_, sti, *, n_seq, steps):
    i = pl.program_id(0)

    @pl.when(i == 0)
    def _():
        str_[...] = s0r_ref[...]
        sti[...] = s0i_ref[...]

    lam_re, lam_im = lre_ref[...], lim_ref[...]
    dt = jnp.exp(ldt_ref[...])
    mag = jnp.exp(lam_re * dt)
    a_re = mag * jnp.cos(lam_im * dt)
    a_im = mag * jnp.sin(lam_im * dt)
    den = lam_re * lam_re + lam_im * lam_im
    nr, ni = a_re - 1.0, a_im
    c_re = (nr * lam_re + ni * lam_im) / den
    c_im = (ni * lam_re - nr * lam_im) / den

    u = u_ref[...]
    ub = u.astype(BF16)
    bu_re = _dot(ub, bre_ref[...])
    bu_im = _dot(ub, bim_ref[...])
    hre[...] = c_re * bu_re - c_im * bu_im
    him[...] = c_re * bu_im + c_im * bu_re

    if n_seq <= 8:
        def step(t, carry):
            sr, si = carry
            rows = pl.ds(pl.multiple_of(t * n_seq, n_seq), n_seq)
            nr_ = a_re * sr - a_im * si + hre[rows, :]
            ni_ = a_re * si + a_im * sr + him[rows, :]
            hre[rows, :] = nr_
            him[rows, :] = ni_
            return nr_, ni_

        sr, si = lax.fori_loop(0, steps, step, (str_[...], sti[...]))
        str_[...] = sr
        sti[...] = si
    else:
        for t in range(steps):
            rows = slice(t * n_seq, (t + 1) * n_seq)
            sr, si = str_[...], sti[...]
            nr_ = a_re * sr - a_im * si + hre[rows, :]
            ni_ = a_re * si + a_im * sr + him[rows, :]
            hre[rows, :] = nr_
            him[rows, :] = ni_
            str_[...] = nr_
            sti[...] = ni_

    y = _dot(hre[...].astype(BF16), cre_ref[...]) - _dot(him[...].astype(BF16), cim_ref[...])
    y_ref[...] = y + d_ref[...] * u
    fr_ref[...] = str_[...]
    fi_ref[...] = sti[...]


def _ssm(u_tm, s0r, s0i, lam_re, lam_im, log_dt, b_re, b_im, c_re, c_im, dvec, n_seq, n_steps, chunk):
    rows = chunk * n_seq
    fixed = lambda i: (0, 0)
    sds = jax.ShapeDtypeStruct
    return pl.pallas_call(
        functools.partial(_ssm_kernel, n_seq=n_seq, steps=chunk),
        grid=(n_steps // chunk,),
        in_specs=[pl.BlockSpec((rows, SSM_WIDTH), lambda i: (i, 0)),
                  pl.BlockSpec((n_seq, SSM_LANES), fixed), pl.BlockSpec((n_seq, SSM_LANES), fixed),
                  pl.BlockSpec((1, SSM_LANES), fixed), pl.BlockSpec((1, SSM_LANES), fixed),
                  pl.BlockSpec((1, SSM_LANES), fixed),
                  pl.BlockSpec((SSM_WIDTH, SSM_LANES), fixed), pl.BlockSpec((SSM_WIDTH, SSM_LANES), fixed),
                  pl.BlockSpec((SSM_LANES, SSM_WIDTH), fixed), pl.BlockSpec((SSM_LANES, SSM_WIDTH), fixed),
                  pl.BlockSpec((1, SSM_WIDTH), fixed)],
        out_specs=[pl.BlockSpec((rows, SSM_WIDTH), lambda i: (i, 0)),
                   pl.BlockSpec((n_seq, SSM_LANES), fixed), pl.BlockSpec((n_seq, SSM_LANES), fixed)],
        out_shape=[sds((n_steps * n_seq, SSM_WIDTH), F32), sds((n_seq, SSM_LANES), F32),
                   sds((n_seq, SSM_LANES), F32)],
        scratch_shapes=[pltpu.VMEM((rows, SSM_LANES), F32), pltpu.VMEM((rows, SSM_LANES), F32),
                        pltpu.VMEM((n_seq, SSM_LANES), F32), pltpu.VMEM((n_seq, SSM_LANES), F32)],
        compiler_params=_cparams(("arbitrary",)),
        name=f"ssm_scan_{n_seq}",
    )(u_tm, s0r, s0i, lam_re, lam_im, log_dt, b_re, b_im, c_re, c_im, dvec)


def _mixout_value(o_refs, l_refs, y_ref, x_ref, ga_ref, gs_ref, wglu_ref, bglu_ref, wout_ref):
    full = slice(None)
    l0, l1, l2 = (_load_halves(r, 0, full) for r in l_refs)
    m = jnp.maximum(jnp.maximum(l0, l1), l2)
    e0, e1, e2 = jnp.exp(l0 - m), jnp.exp(l1 - m), jnp.exp(l2 - m)
    den = e0 + e1 + e2
    a = [_load_halves(r, 0, full) * (e / den) for r, e in zip(o_refs, (e0, e1, e2))]
    ssq = sum(jnp.sum(ai * ai, axis=-1, keepdims=True) for ai in a)
    inv = lax.rsqrt(ssq * (1.0 / ATTN_WIDTH) + EPS)
    acc = x_ref[...]
    for g in range(N_GROUPS):
        an = (a[g] * inv * ga_ref[:, g * GROUP_W:(g + 1) * GROUP_W]).astype(BF16)
        acc = acc + _dot(an, wout_ref[g * GROUP_W:(g + 1) * GROUP_W, :])
    y = y_ref[...]
    zs = 0.5 * y * (1.0 + jnp.tanh(math.sqrt(2.0 / math.pi) * (y + 0.044715 * (y * y * y))))
    gate = jax.nn.sigmoid(_dot(zs.astype(BF16), wglu_ref[...]) + bglu_ref[...])
    so = _rms_rows(zs * gate, gs_ref[...]).astype(BF16)
    return acc + _dot(so, wout_ref[ATTN_WIDTH:, :])


def _mixout_kernel(o0_ref, o1_ref, o2_ref, l0_ref, l1_ref, l2_ref, y_ref, x_ref,
                   ga_ref, gs_ref, wglu_ref, bglu_ref, wout_ref, h_ref):
    h_ref[...] = _mixout_value((o0_ref, o1_ref, o2_ref), (l0_ref, l1_ref, l2_ref), y_ref, x_ref,
                               ga_ref, gs_ref, wglu_ref, bglu_ref, wout_ref)


def _mix_out(os_, ls_, y, x2d, g_attn, g_ssm, wglu_bf, bglu, wout_bf, tm):
    n = x2d.shape[0]
    row = lambda i: (i, 0)
    fixed = lambda i: (0, 0)
    return pl.pallas_call(
        _mixout_kernel,
        grid=(n // tm,),
        in_specs=[_halves_spec(LANE_SPLIT, tm, lambda i: (0, i, 0))] * 6
                 + [pl.BlockSpec((tm, SSM_WIDTH), row), pl.BlockSpec((tm, D_MODEL), row)]
                 + [pl.BlockSpec((1, ATTN_WIDTH), fixed), pl.BlockSpec((1, SSM_WIDTH), fixed),
                    pl.BlockSpec((SSM_WIDTH, SSM_WIDTH), fixed), pl.BlockSpec((1, SSM_WIDTH), fixed),
                    pl.BlockSpec((D_MODEL, D_MODEL), fixed)],
        out_specs=pl.BlockSpec((tm, D_MODEL), row),
        out_shape=jax.ShapeDtypeStruct((n, D_MODEL), F32),
        compiler_params=_cparams(("arbitrary",)),
        name="mix_out",
    )(*os_, *ls_, y, x2d, g_attn, g_ssm, wglu_bf, bglu, wout_bf)


def _memkv_kernel(x_ref, g_ref, w_ref, kn_ref, o_ref):
    xn = _rms_rows(x_ref[...], g_ref[...]).astype(BF16)
    kn = kn_ref[...]
    for h in range(X_HEADS):
        cols = slice(h * X_HEAD_DIM, (h + 1) * X_HEAD_DIM)
        o_ref[:, cols] = _rms_rows(_dot(xn, w_ref[:, cols]), kn)
    o_ref[:, D_MODEL:] = _dot(xn, w_ref[:, D_MODEL:])


def _memory_kv(mem2d, gain, w_bf, kn, tm):
    n = mem2d.shape[0]
    fixed = lambda i: (0, 0)
    return pl.pallas_call(
        _memkv_kernel,
        grid=(n // tm,),
        in_specs=[pl.BlockSpec((tm, D_MODEL), lambda i: (i, 0)), pl.BlockSpec((1, D_MODEL), fixed),
                  pl.BlockSpec((D_MODEL, 2 * D_MODEL), fixed), pl.BlockSpec((1, X_HEAD_DIM), fixed)],
        out_specs=pl.BlockSpec((tm, 2 * D_MODEL), lambda i: (i, 0)),
        out_shape=jax.ShapeDtypeStruct((n, 2 * D_MODEL), F32),
        compiler_params=_cparams(("arbitrary",)),
        name="memory_kv",
    )(mem2d, gain, w_bf, kn)


def _mix_cross_kernel(o0_ref, o1_ref, o2_ref, l0_ref, l1_ref, l2_ref, y_ref, x_ref,
                      ga_ref, gs_ref, wglu_ref, bglu_ref, wout_ref,
                      mkv_ref, gx_ref, wq_ref, qn_ref, wo_ref, o_ref, att):
    h = _mixout_value((o0_ref, o1_ref, o2_ref), (l0_ref, l1_ref, l2_ref), y_ref, x_ref,
                      ga_ref, gs_ref, wglu_ref, bglu_ref, wout_ref)
    hn = _rms_rows(h, gx_ref[...]).astype(BF16)
    qn = qn_ref[...]
    for hd in range(X_HEADS):
        cols = slice(hd * X_HEAD_DIM, (hd + 1) * X_HEAD_DIM)
        vcols = slice(D_MODEL + hd * X_HEAD_DIM, D_MODEL + (hd + 1) * X_HEAD_DIM)
        q = _rms_rows(_dot(hn, wq_ref[:, cols]), qn) * (X_HEAD_DIM ** -0.5)
        s = _dot_t(q.astype(BF16), mkv_ref[:, cols].astype(BF16))
        m = jnp.max(s, axis=-1, keepdims=True)
        p = jnp.exp(s - m)
        l = jnp.sum(p, axis=-1, keepdims=True)
        att[:, cols] = (_dot(p.astype(BF16), mkv_ref[:, vcols].astype(BF16)) / l).astype(BF16)
    o_ref[...] = h + _dot(att[...], wo_ref[...])


def _mix_cross(os_, ls_, y, x2d, mkv, p, rows_per_seq, tm):
    n = x2d.shape[0]
    blocks_per_seq = rows_per_seq // tm
    row = lambda i: (i, 0)
    fixed = lambda i: (0, 0)
    return pl.pallas_call(
        _mix_cross_kernel,
        grid=(n // tm,),
        in_specs=[_halves_spec(LANE_SPLIT, tm, lambda i: (0, i, 0))] * 6
                 + [pl.BlockSpec((tm, SSM_WIDTH), row), pl.BlockSpec((tm, D_MODEL), row)]
                 + [pl.BlockSpec((1, ATTN_WIDTH), fixed), pl.BlockSpec((1, SSM_WIDTH), fixed),
                    pl.BlockSpec((SSM_WIDTH, SSM_WIDTH), fixed), pl.BlockSpec((1, SSM_WIDTH), fixed),
                    pl.BlockSpec((D_MODEL, D_MODEL), fixed)]
                 + [pl.BlockSpec((None, N_MEM, 2 * D_MODEL), lambda i: (i // blocks_per_seq, 0, 0)),
                    pl.BlockSpec((1, D_MODEL), fixed), pl.BlockSpec((D_MODEL, D_MODEL), fixed),
                    pl.BlockSpec((1, X_HEAD_DIM), fixed), pl.BlockSpec((D_MODEL, D_MODEL), fixed)],
        out_specs=pl.BlockSpec((tm, D_MODEL), row),
        out_shape=jax.ShapeDtypeStruct((n, D_MODEL), F32),
        scratch_shapes=[pltpu.VMEM((tm, D_MODEL), BF16)],
        compiler_params=_cparams(("arbitrary",)),
        name="mix_cross",
    )(*os_, *ls_, y, x2d, p['attn_out_norm'], p['ssm_out_norm'], p['w_glu'], p['b_glu'], p['w_out'],
      mkv, p['norm_x'], p['w_xq'], p['xq_norm'], p['w_xo'])


MEM_HALVES = X_HEAD_DIM // 128
MEM_ROW_STRIDE = 2 * MEM_HALVES * X_HEADS


def _cross_sample_kernel(h_ref, mem_ref, g_ref, wq_ref, qn_ref, wo_ref, o_ref, qsc, qm, osel, att, *, bb_n, t_n):
    h = h_ref[...]
    hn = _rms_rows(h, g_ref[...]).astype(BF16)
    qn = qn_ref[...]
    for hd in range(X_HEADS):
        cols = slice(hd * X_HEAD_DIM, (hd + 1) * X_HEAD_DIM)
        qsc[:, cols] = _rms_rows(_dot(hn, wq_ref[:, cols]), qn) * (X_HEAD_DIM ** -0.5)
    xh = X_HEADS
    head = lax.broadcasted_iota(jnp.int32, (1, D_MODEL), 1) // X_HEAD_DIM
    sel = head == lax.broadcasted_iota(jnp.int32, (xh, 1), 0)
    sel_rows = head == lax.broadcasted_iota(jnp.int32, (t_n * xh, 1), 0) % xh

    def gather(bb, kv):
        parts = [mem_ref[bb, pl.ds((kv * MEM_HALVES + half) * xh + hd, N_MEM, stride=MEM_ROW_STRIDE), :]
                 for hd in range(xh) for half in range(MEM_HALVES)]
        return jnp.concatenate(parts, axis=1).astype(BF16)

    for bb in range(bb_n):
        for t in range(t_n):
            r = bb * t_n + t
            qm[t * xh:(t + 1) * xh, :] = jnp.where(sel, qsc[r:r + 1, :], 0.0)
        s = _dot_t(qm[...].astype(BF16), gather(bb, 0))
        m = jnp.max(s, axis=-1, keepdims=True)
        p = jnp.exp(s - m)
        l = jnp.sum(p, axis=-1, keepdims=True)
        osel[...] = jnp.where(sel_rows, _dot(p.astype(BF16), gather(bb, 1)) / l, 0.0)
        for t in range(t_n):
            r = bb * t_n + t
            att[r:r + 1, :] = jnp.sum(osel[t * xh:(t + 1) * xh, :], axis=0, keepdims=True)
    o_ref[...] = h + _dot(att[...].astype(BF16), wo_ref[...])


def _cross_sample(h2d, mem_kv, gain, wq_bf, qn, wo_bf, batch, t_n, bb_n):
    mem = mem_kv.reshape(batch, N_MEM, 2, X_HEADS, MEM_HALVES, 128)
    mem = jnp.transpose(mem, (0, 1, 2, 4, 3, 5)).reshape(batch, N_MEM * MEM_ROW_STRIDE, 128)
    rows = bb_n * t_n
    fixed = lambda i: (0, 0)
    return pl.pallas_call(
        functools.partial(_cross_sample_kernel, bb_n=bb_n, t_n=t_n),
        grid=(batch // bb_n,),
        in_specs=[pl.BlockSpec((rows, D_MODEL), lambda i: (i, 0)),
                  pl.BlockSpec((bb_n, N_MEM * MEM_ROW_STRIDE, 128), lambda i: (i, 0, 0)),
                  pl.BlockSpec((1, D_MODEL), fixed), pl.BlockSpec((D_MODEL, D_MODEL), fixed),
                  pl.BlockSpec((1, X_HEAD_DIM), fixed), pl.BlockSpec((D_MODEL, D_MODEL), fixed)],
        out_specs=pl.BlockSpec((rows, D_MODEL), lambda i: (i, 0)),
        out_shape=jax.ShapeDtypeStruct((batch * t_n, D_MODEL), F32),
        scratch_shapes=[pltpu.VMEM((rows, D_MODEL), F32), pltpu.VMEM((t_n * X_HEADS, D_MODEL), F32),
                        pltpu.VMEM((t_n * X_HEADS, D_MODEL), F32), pltpu.VMEM((rows, D_MODEL), F32)],
        compiler_params=_cparams(("arbitrary",)),
        name="cross_attn_sample",
    )(h2d, mem, gain, wq_bf, qn, wo_bf)


FFN_CHUNKS = 2
FFN_TF = D_FF // FFN_CHUNKS
CARRY_ROWS = 8


def _conv_gate(up_g, up_v, prev_g, prev_v, bufg, bufv, cwg_ref, cwv_ref, cbg_ref, cbv_ref, tm, lag):
    hist = prev_g.shape[0]
    bufg[0:hist] = prev_g
    bufg[hist:hist + tm] = up_g
    bufv[0:hist] = prev_v
    bufv[hist:hist + tm] = up_v
    cg = cbg_ref[...] + (cwg_ref[0:1, :] * bufg[hist - 2 * lag:hist - 2 * lag + tm]
                         + cwg_ref[1:2, :] * bufg[hist - lag:hist - lag + tm] + cwg_ref[2:3, :] * up_g)
    cv = cbv_ref[...] + (cwv_ref[0:1, :] * bufv[hist - 2 * lag:hist - 2 * lag + tm]
                         + cwv_ref[1:2, :] * bufv[hist - lag:hist - lag + tm] + cwv_ref[2:3, :] * up_v)
    return (cg * jax.nn.sigmoid(cg) * cv).astype(BF16)


FFN_COLS = 256
FFN_NCHUNK = D_FF // FFN_COLS


def _ffn_prompt_kernel(h_ref, g_ref, wup_ref, cw_ref, cb_ref, wd_ref, o_ref, tail_ref, act, carry, *, tm):
    i = pl.program_id(1)
    h = h_ref[...]
    xn = _rms_rows(h, g_ref[...]).astype(BF16)

    @pl.when(i == 0)
    def _():
        carry[...] = jnp.zeros(carry.shape, F32)

    for n in range(FFN_NCHUNK):
        conv = []
        for part in range(2):
            cols = slice(part * D_FF + n * FFN_COLS, part * D_FF + (n + 1) * FFN_COLS)
            up = _dot(xn, wup_ref[:, cols])
            ext = jnp.concatenate([carry[part, n], up], axis=0)
            conv.append(cb_ref[:, cols] + (cw_ref[0:1, cols] * ext[CARRY_ROWS - 2:CARRY_ROWS - 2 + tm]
                                           + cw_ref[1:2, cols] * ext[CARRY_ROWS - 1:CARRY_ROWS - 1 + tm]
                                           + cw_ref[2:3, cols] * up))
            carry[part, n] = up[tm - CARRY_ROWS:]
            tail_ref[:, cols] = up[tm - CARRY_ROWS:]
        cg, cv = conv
        act[:, n * FFN_COLS:(n + 1) * FFN_COLS] = (cg * jax.nn.sigmoid(cg) * cv).astype(BF16)
    o_ref[...] = h + _dot(act[...], wd_ref[...])


def _ffn_prompt(h2d, gain, wup_bf, conv_w, conv_b, wdown_bf, batch, seq, tm):
    n_i = seq // tm
    row = lambda b, i: (b * n_i + i, 0)
    fixed = lambda b, i: (0, 0)
    once = pl.Buffered(1)
    sds = jax.ShapeDtypeStruct
    return pl.pallas_call(
        functools.partial(_ffn_prompt_kernel, tm=tm),
        grid=(batch, n_i),
        in_specs=[pl.BlockSpec((tm, D_MODEL), row), pl.BlockSpec((1, D_MODEL), fixed),
                  pl.BlockSpec((D_MODEL, 2 * D_FF), fixed, pipeline_mode=once),
                  pl.BlockSpec((CONV_W, 2 * D_FF), fixed), pl.BlockSpec((1, 2 * D_FF), fixed),
                  pl.BlockSpec((D_FF, D_MODEL), fixed, pipeline_mode=once)],
        out_specs=[pl.BlockSpec((tm, D_MODEL), row),
                   pl.BlockSpec((None, None, CARRY_ROWS, 2 * D_FF), lambda b, i: (b, i, 0, 0))],
        out_shape=[sds((batch * seq, D_MODEL), F32), sds((batch, n_i, CARRY_ROWS, 2 * D_FF), F32)],
        scratch_shapes=[pltpu.VMEM((tm, D_FF), BF16), pltpu.VMEM((2, FFN_NCHUNK, CARRY_ROWS, FFN_COLS), F32)],
        compiler_params=_cparams(("arbitrary",) * 2),
        name="ffn_prompt",
    )(h2d, gain, wup_bf, conv_w, conv_b, wdown_bf)


def _ffn_sample_kernel(h_ref, g_ref, sg_ref, sv_ref, wg_ref, wv_ref, cwg_ref, cwv_ref, cbg_ref, cbv_ref, wd_ref,
                       o_ref, ng_ref, nv_ref, xn, bufg, bufv, *, n_seq, rows):
    c = pl.program_id(0)

    @pl.when(c == 0)
    def _():
        xn[...] = _rms_rows(h_ref[...], g_ref[...]).astype(BF16)

    up_g = _dot(xn[...], wg_ref[...])
    up_v = _dot(xn[...], wv_ref[...])
    act = _conv_gate(up_g, up_v, sg_ref[...], sv_ref[...], bufg, bufv, cwg_ref, cwv_ref, cbg_ref, cbv_ref,
                     rows, n_seq)
    hist = (CONV_W - 1) * n_seq
    ng_ref[...] = bufg[rows:rows + hist]
    nv_ref[...] = bufv[rows:rows + hist]
    contrib = _dot(act, wd_ref[...])

    @pl.when(c == 0)
    def _():
        o_ref[...] = h_ref[...] + contrib

    @pl.when(c > 0)
    def _():
        o_ref[...] += contrib


def _ffn_sample(h_tm, gain, state_tm, wup_bf, conv_w, conv_b, wdown_bf, n_seq, n_steps):
    rows = n_seq * n_steps
    hist = (CONV_W - 1) * n_seq
    nc, tf = FFN_CHUNKS, FFN_TF
    gcol = lambda c: (0, c)
    vcol = lambda c: (0, nc + c)
    fixed = lambda c: (0, 0)
    sds = jax.ShapeDtypeStruct
    return pl.pallas_call(
        functools.partial(_ffn_sample_kernel, n_seq=n_seq, rows=rows),
        grid=(nc,),
        in_specs=[pl.BlockSpec((rows, D_MODEL), fixed), pl.BlockSpec((1, D_MODEL), fixed),
                  pl.BlockSpec((hist, tf), gcol), pl.BlockSpec((hist, tf), vcol),
                  pl.BlockSpec((D_MODEL, tf), gcol), pl.BlockSpec((D_MODEL, tf), vcol),
                  pl.BlockSpec((CONV_W, tf), gcol), pl.BlockSpec((CONV_W, tf), vcol),
                  pl.BlockSpec((1, tf), gcol), pl.BlockSpec((1, tf), vcol),
                  pl.BlockSpec((tf, D_MODEL), lambda c: (c, 0))],
        out_specs=[pl.BlockSpec((rows, D_MODEL), fixed), pl.BlockSpec((hist, tf), gcol),
                   pl.BlockSpec((hist, tf), gcol)],
        out_shape=[sds((rows, D_MODEL), F32), sds((hist, D_FF), F32), sds((hist, D_FF), F32)],
        scratch_shapes=[pltpu.VMEM((rows, D_MODEL), BF16), pltpu.VMEM((rows + hist, tf), F32),
                        pltpu.VMEM((rows + hist, tf), F32)],
        compiler_params=_cparams(("arbitrary",)),
        name="ffn_sample",
    )(h_tm, gain, state_tm, state_tm, wup_bf, wup_bf, conv_w, conv_w, conv_b, conv_b, wdown_bf)


def _rope_tables(pos):
    half = ROT_DIM // 2
    inv = jnp.power(jnp.float32(ROPE_THETA), -jnp.arange(half, dtype=F32) * 2.0 / ROT_DIM)
    ang = pos.astype(F32)[:, None] * inv[None, :]
    cos, sin = jnp.cos(ang), jnp.sin(ang)
    ones = jnp.ones((pos.shape[0], HEAD_DIM - ROT_DIM), F32)
    zeros = jnp.zeros((pos.shape[0], HEAD_DIM - ROT_DIM), F32)
    zh = jnp.zeros_like(sin)
    c_head = jnp.concatenate([cos, cos, ones], axis=1)
    s1_head = jnp.concatenate([-sin, zh, zeros], axis=1)
    s2_head = jnp.concatenate([zh, sin, zeros], axis=1)
    tile = lambda a: jnp.tile(a, (1, HEADS_PER_GROUP))
    return tile(c_head), tile(s1_head), tile(s2_head)


def _block_diag(w):
    g, a, b = w.shape
    eye = jnp.eye(g, dtype=w.dtype)
    return (eye[:, None, :, None] * w[:, :, None, :]).reshape(g * a, g * b)


def _layer_params(l, norm_mix, w_in, q_norm, k_norm, ssm_lam_re, ssm_lam_im, ssm_log_dt, ssm_b_re, ssm_b_im,
                  ssm_c_re, ssm_c_im, ssm_d, w_glu, b_glu, attn_out_norm, ssm_out_norm, w_out, norm_x, norm_mem,
                  w_xq, w_xkv, xq_norm, xk_norm, w_xo, norm_ffn, w_up, conv_w, conv_b, w_down):
    row = lambda a: a.reshape(1, -1).astype(F32)
    return dict(
        norm_mix=row(norm_mix[l]), w_in=w_in[l].astype(BF16),
        q_norm=row(jnp.tile(q_norm[l], HEADS_PER_GROUP)), k_norm=row(jnp.tile(k_norm[l], HEADS_PER_GROUP)),
        lam_re=row(ssm_lam_re[l]), lam_im=row(ssm_lam_im[l]),
        log_dt=row(jnp.repeat(ssm_log_dt[l], SSM_STATE)),
        b_re=_block_diag(jnp.swapaxes(ssm_b_re[l], 1, 2)).astype(BF16),
        b_im=_block_diag(jnp.swapaxes(ssm_b_im[l], 1, 2)).astype(BF16),
        c_re=_block_diag(jnp.swapaxes(ssm_c_re[l], 1, 2)).astype(BF16),
        c_im=_block_diag(jnp.swapaxes(ssm_c_im[l], 1, 2)).astype(BF16),
        ssm_d=row(ssm_d[l]), w_glu=w_glu[l].astype(BF16), b_glu=row(b_glu[l]),
        attn_out_norm=row(attn_out_norm[l]), ssm_out_norm=row(ssm_out_norm[l]), w_out=w_out[l].astype(BF16),
        norm_x=row(norm_x[l]), norm_mem=row(norm_mem[l]), w_xq=w_xq[l].astype(BF16),
        w_xkv=w_xkv[l].astype(BF16), xq_norm=row(xq_norm[l]), xk_norm=row(xk_norm[l]),
        w_xo=w_xo[l].astype(BF16), norm_ffn=row(norm_ffn[l]), w_up=w_up[l].astype(BF16),
        conv_w=conv_w[l].astype(F32), conv_b=row(conv_b[l]), w_down=w_down[l].astype(BF16))


def _kv_rows(kv_halves, n_seq, n_steps):
    heads_per_half = LANES // HEAD_DIM
    a = kv_halves.reshape(2, LANE_SPLIT, n_seq, n_steps, heads_per_half, HEAD_DIM)
    return jnp.transpose(a, (2, 3, 0, 1, 4, 5)).reshape(n_seq, n_steps, 2, HEADS_PER_GROUP, HEAD_DIM)


def _to_time_major(a, n_seq, n_steps):
    return a.reshape(n_seq, n_steps, -1).swapaxes(0, 1).reshape(n_steps * n_seq, -1)


def _to_seq_major(a, n_seq, n_steps):
    return a.reshape(n_steps, n_seq, -1).swapaxes(0, 1).reshape(n_seq * n_steps, -1)


def _mixer_front(x2d, p, tables, tm):
    return _in_proj(x2d, p['norm_mix'], p['w_in'], p['q_norm'], p['k_norm'], *tables, tm)


def _run_ssm(u, s0r, s0i, p, n_seq, n_steps, chunk):
    u_tm = _to_time_major(u, n_seq, n_steps)
    y_tm, fr, fi = _ssm(u_tm, s0r, s0i, p['lam_re'], p['lam_im'], p['log_dt'], p['b_re'], p['b_im'],
                        p['c_re'], p['c_im'], p['ssm_d'], n_seq, n_steps, chunk)
    return _to_seq_major(y_tm, n_seq, n_steps), fr, fi


def kernel(x_prompt, x_sample, mem_prompt, cache_kv_dil1, cache_kv_dil4, cache_kv_dil16, state_ssm_re, state_ssm_im, state_ffn_conv, cache_mem_kv, norm_mix, w_in, q_norm, k_norm, ssm_lam_re, ssm_lam_im, ssm_log_dt, ssm_b_re, ssm_b_im, ssm_c_re, ssm_c_im, ssm_d, w_glu, b_glu, attn_out_norm, ssm_out_norm, w_out, norm_x, norm_mem, w_xq, w_xkv, xq_norm, xk_norm, w_xo, norm_ffn, w_up, conv_w, conv_b, w_down):
    bp, lp, _ = x_prompt.shape
    bs, ls, _ = x_sample.shape
    depth = w_in.shape[0]
    past_len = cache_kv_dil16.shape[2]
    caches = (cache_kv_dil1, cache_kv_dil4, cache_kv_dil16)
    tables_p = _rope_tables(jnp.arange(lp))
    tables_s = tuple(jnp.tile(t, (bs, 1)) for t in _rope_tables(past_len + jnp.arange(ls)))

    yp = x_prompt.reshape(bp * lp, D_MODEL)
    ys = x_sample.reshape(bs * ls, D_MODEL)
    kv_p = [[] for _ in range(N_GROUPS)]
    kv_s = [[] for _ in range(N_GROUPS)]
    re_p, im_p, conv_p, mem_p, re_s, im_s, conv_s = [], [], [], [], [], [], []
    for l in range(depth):
        p = _layer_params(l, norm_mix, w_in, q_norm, k_norm, ssm_lam_re, ssm_lam_im, ssm_log_dt, ssm_b_re,
                          ssm_b_im, ssm_c_re, ssm_c_im, ssm_d, w_glu, b_glu, attn_out_norm, ssm_out_norm, w_out,
                          norm_x, norm_mem, w_xq, w_xkv, xq_norm, xk_norm, w_xo, norm_ffn, w_up, conv_w, conv_b,
                          w_down)
        mkv = _memory_kv(mem_prompt.reshape(bp * N_MEM, D_MODEL), p['norm_mem'], p['w_xkv'], p['xk_norm'], 256)
        mkv = mkv.reshape(bp, N_MEM, 2 * D_MODEL)
        mem_p.append(mkv.reshape(bp, N_MEM, 2, X_HEADS, X_HEAD_DIM))
        q0, q1, q2, kv0, kv1, kv2, u = _mixer_front(yp, p, tables_p, 512)
        os_, ls_ = [], []
        for g, (qg, kvg) in enumerate(((q0, kv0), (q1, kv1), (q2, kv2))):
            win, dil = ATTN_GROUPS[g]
            o, lse = _prompt_attn(qg, kvg, bp, lp, dil)
            os_.append(o)
            ls_.append(lse)
            keep = min(win, lp)
            kv_p[g].append(_kv_rows(kvg, bp, lp)[:, lp - keep:])
        zero_state = jnp.zeros((bp, SSM_LANES), F32)
        y, fr, fi = _run_ssm(u, zero_state, zero_state, p, bp, lp, 128)
        re_p.append(fr.reshape(bp, SSM_GROUPS, SSM_STATE))
        im_p.append(fi.reshape(bp, SSM_GROUPS, SSM_STATE))
        h = _mix_cross(os_, ls_, y, yp, mkv, p, lp, 512)
        yp, tail = _ffn_prompt(h, p['norm_ffn'], p['w_up'], p['conv_w'], p['conv_b'], p['w_down'], bp, lp, 512)
        conv_p.append(tail[:, -1, CARRY_ROWS - (CONV_W - 1):])
        q0, q1, q2, kv0, kv1, kv2, u = _mixer_front(ys, p, tables_s, bs * ls)
        os_, ls_ = [], []
        for g, (qg, kvg) in enumerate(((q0, kv0), (q1, kv1), (q2, kv2))):
            win, dil = ATTN_GROUPS[g]
            o, lse = _sample_attn(qg, kvg, caches[g][l], bs, ls, dil, max(1, 32 // dil))
            os_.append(o)
            ls_.append(lse)
            kv_s[g].append(_kv_rows(kvg, bs, ls))
        y, fr, fi = _run_ssm(u, state_ssm_re[l].reshape(bs, SSM_LANES), state_ssm_im[l].reshape(bs, SSM_LANES),
                             p, bs, ls, ls)
        re_s.append(fr.reshape(bs, SSM_GROUPS, SSM_STATE))
        im_s.append(fi.reshape(bs, SSM_GROUPS, SSM_STATE))
        h = _mix_out(os_, ls_, y, ys, p['attn_out_norm'], p['ssm_out_norm'], p['w_glu'], p['b_glu'], p['w_out'],
                     bs * ls)
        h = _cross_sample(h, cache_mem_kv[l], p['norm_x'], p['w_xq'], p['xq_norm'], p['w_xo'], bs, ls, 4)
        state_tm = state_ffn_conv[l].swapaxes(0, 1).reshape((CONV_W - 1) * bs, 2 * D_FF)
        ys_tm, ng, nv = _ffn_sample(_to_time_major(h, bs, ls), p['norm_ffn'], state_tm, p['w_up'], p['conv_w'],
                                    p['conv_b'], p['w_down'], bs, ls)
        ys = _to_seq_major(ys_tm, bs, ls)
        conv_s.append(jnp.concatenate([ng, nv], axis=-1).reshape(CONV_W - 1, bs, 2 * D_FF).swapaxes(0, 1))

    st = jnp.stack
    return (yp.reshape(bp, lp, D_MODEL), ys.reshape(bs, ls, D_MODEL),
            st(kv_p[0]), st(kv_p[1]), st(kv_p[2]), st(re_p), st(im_p), st(conv_p), st(mem_p),
            st(kv_s[0]), st(kv_s[1]), st(kv_s[2]), st(re_s), st(im_s), st(conv_s))
```

```python
import functools
import math

import jax
import jax.numpy as jnp
from jax import lax
from jax.experimental import pallas as pl
from jax.experimental.pallas import tpu as pltpu

F32 = jnp.float32
BF16 = jnp.bfloat16

D_MODEL = 1024
HEAD_DIM = 64
ATTN_GROUPS = ((128, 1), (512, 4), (2048, 16))
HEADS_PER_GROUP = 4
GROUP_W = HEADS_PER_GROUP * HEAD_DIM
N_GROUPS = len(ATTN_GROUPS)
ATTN_WIDTH = N_GROUPS * GROUP_W
SSM_WIDTH = D_MODEL - ATTN_WIDTH
SSM_GROUP = 16
SSM_GROUPS = SSM_WIDTH // SSM_GROUP
SSM_STATE = 64
SSM_LANES = SSM_GROUPS * SSM_STATE
IN_WIDTH = 3 * ATTN_WIDTH + SSM_WIDTH
ROT_DIM = HEAD_DIM // 4
ROPE_THETA = 500000.0
N_MEM = 256
X_HEADS = 4
X_HEAD_DIM = D_MODEL // X_HEADS
D_FF = 2816
CONV_W = 3
EPS = 1e-6
ATTN_BLK = 128
NEG = -1e30

VMEM_LIMIT = 56 * 1024 * 1024


def _cparams(sem):
    return pltpu.CompilerParams(dimension_semantics=sem, vmem_limit_bytes=VMEM_LIMIT)


def _dot(a, b):
    return jnp.dot(a, b, preferred_element_type=F32)


def _dot_t(a, b):
    return lax.dot_general(a, b, (((1,), (1,)), ((), ())), preferred_element_type=F32)


def _rms_rows(x, gain):
    y = x * lax.rsqrt(jnp.mean(x * x, axis=-1, keepdims=True) + EPS)
    return y * gain


LANES = 128
LANE_SPLIT = GROUP_W // LANES


def _halves_spec(n_halves, rows, index_map):
    return pl.BlockSpec((n_halves, rows, LANES), index_map)


def _load_halves(ref, first, rows):
    return jnp.concatenate([ref[first + c, rows, :] for c in range(LANE_SPLIT)], axis=1)


def _store_halves(ref, first, value, rows=slice(None)):
    for c in range(LANE_SPLIT):
        ref[first + c, rows, :] = value[:, c * LANES:(c + 1) * LANES]


def _head_ones():
    r = lax.broadcasted_iota(jnp.int32, (GROUP_W, GROUP_W), 0) // HEAD_DIM
    c = lax.broadcasted_iota(jnp.int32, (GROUP_W, GROUP_W), 1) // HEAD_DIM
    return (r == c).astype(BF16)


def _head_inv_rms(z, ones_head):
    sq = z * z
    hi = sq.astype(BF16)
    lo = (sq - hi.astype(F32)).astype(BF16)
    ssum = _dot(hi, ones_head) + _dot(lo, ones_head)
    return lax.rsqrt(ssum * (1.0 / HEAD_DIM) + EPS)


def _inproj_kernel(x_ref, g_ref, w_ref, qn_ref, kn_ref, c_ref, s1_ref, s2_ref,
                   q0_ref, q1_ref, q2_ref, kv0_ref, kv1_ref, kv2_ref, u_ref):
    xn = _rms_rows(x_ref[...], g_ref[...]).astype(BF16)
    ones_head = _head_ones()
    cosw, sin1, sin2 = c_ref[...], s1_ref[...], s2_ref[...]
    qn, kn = qn_ref[...], kn_ref[...]
    half = ROT_DIM // 2

    def rope(z):
        return z * cosw + pltpu.roll(z, GROUP_W - half, 1) * sin1 + pltpu.roll(z, half, 1) * sin2

    q_refs = (q0_ref, q1_ref, q2_ref)
    kv_refs = (kv0_ref, kv1_ref, kv2_ref)
    proj = lambda start: _dot(xn, w_ref[:, start:start + GROUP_W])
    for g in range(N_GROUPS):
        zq, zk, zv = (proj(part * ATTN_WIDTH + g * GROUP_W) for part in range(3))
        _store_halves(q_refs[g], 0, rope(zq * _head_inv_rms(zq, ones_head) * qn) * (HEAD_DIM ** -0.5))
        _store_halves(kv_refs[g], 0, rope(zk * _head_inv_rms(zk, ones_head) * kn))
        _store_halves(kv_refs[g], LANE_SPLIT, zv)
    u_ref[...] = proj(3 * ATTN_WIDTH)


def _in_proj(x2d, gain, w_bf, qn, kn, cos_t, sin1_t, sin2_t, tm):
    n = x2d.shape[0]
    n_pos = cos_t.shape[0] // tm
    row = lambda i: (i, 0)
    fixed = lambda i: (0, 0)
    pos = lambda i: (i % n_pos, 0)
    sds = jax.ShapeDtypeStruct
    return pl.pallas_call(
        _inproj_kernel,
        grid=(n // tm,),
        in_specs=[pl.BlockSpec((tm, D_MODEL), row), pl.BlockSpec((1, D_MODEL), fixed),
                  pl.BlockSpec((D_MODEL, IN_WIDTH), fixed),
                  pl.BlockSpec((1, GROUP_W), fixed), pl.BlockSpec((1, GROUP_W), fixed),
                  pl.BlockSpec((tm, GROUP_W), pos), pl.BlockSpec((tm, GROUP_W), pos),
                  pl.BlockSpec((tm, GROUP_W), pos)],
        out_specs=[_halves_spec(LANE_SPLIT, tm, lambda i: (0, i, 0))] * 3
                  + [_halves_spec(2 * LANE_SPLIT, tm, lambda i: (0, i, 0))] * 3 + [pl.BlockSpec((tm, SSM_WIDTH), row)],
        out_shape=[sds((LANE_SPLIT, n, LANES), F32)] * 3 + [sds((2 * LANE_SPLIT, n, LANES), F32)] * 3
                  + [sds((n, SSM_WIDTH), F32)],
        compiler_params=_cparams(("arbitrary",)),
        name="in_proj",
    )(x2d, gain, w_bf, qn, kn, cos_t, sin1_t, sin2_t)


def _pattn_kernel(*refs, dil, n_sub, has_prev):
    if has_prev:
        q_ref, kvc_ref, kvp_ref, o_ref, lse_ref = refs
    else:
        q_ref, kvc_ref, o_ref, lse_ref = refs
    n = pl.program_id(1)
    blk = ATTN_BLK
    span = blk * dil
    head = lax.broadcasted_iota(jnp.int32, (1, GROUP_W), 1) // HEAD_DIM
    nk = 2 * blk if has_prev else blk
    qi = lax.broadcasted_iota(jnp.int32, (HEADS_PER_GROUP * blk, nk), 0) % blk
    ki = lax.broadcasted_iota(jnp.int32, (HEADS_PER_GROUP * blk, nk), 1)
    band = ((ki >= qi) & (ki <= qi + blk)) if has_prev else (ki <= qi)
    band_first = (band & ((n > 0) | (ki >= blk))) if has_prev else band

    def sub(j, r):
        rows = pl.ds(j * span + r, blk, stride=dil) if dil > 1 else pl.ds(j * span, blk)
        q = _load_halves(q_ref, 0, rows)
        kc, vc = _load_halves(kvc_ref, 0, rows), _load_halves(kvc_ref, LANE_SPLIT, rows)
        if has_prev:
            if j > 0:
                prow = pl.ds((j - 1) * span + r, blk, stride=dil) if dil > 1 else pl.ds((j - 1) * span, blk)
                kp, vp = _load_halves(kvc_ref, 0, prow), _load_halves(kvc_ref, LANE_SPLIT, prow)
                mask = band
            else:
                prow = pl.ds(r, blk, stride=dil) if dil > 1 else pl.ds(0, blk)
                kp, vp = _load_halves(kvp_ref, 0, prow), _load_halves(kvp_ref, LANE_SPLIT, prow)
                mask = band_first
            k = jnp.concatenate([kp, kc], axis=0).astype(BF16)
            v = jnp.concatenate([vp, vc], axis=0).astype(BF16)
        else:
            k, v, mask = kc.astype(BF16), vc.astype(BF16), band
        qs = jnp.concatenate([jnp.where(head == h, q, 0.0) for h in range(HEADS_PER_GROUP)], axis=0).astype(BF16)
        return rows, qs, k, v, mask

    def pick(x):
        out = x[(HEADS_PER_GROUP - 1) * blk:]
        for h in range(HEADS_PER_GROUP - 2, -1, -1):
            out = jnp.where(head == h, x[h * blk:(h + 1) * blk], out)
        return out

    def run(subs):
        loaded = [sub(j, r) for j, r in subs]
        scores = [jnp.where(mask, _dot_t(qs, k), NEG) for _, qs, k, _, mask in loaded]
        stats = []
        for s in scores:
            m = jnp.max(s, axis=-1, keepdims=True)
            p = jnp.exp(s - m)
            stats.append((m, jnp.sum(p, axis=-1, keepdims=True), p.astype(BF16)))
        outs = [_dot(p, v) for (_, _, p), (_, _, _, v, _) in zip(stats, loaded)]
        for (rows, _, _, _, _), (m, l, _), pv in zip(loaded, stats, outs):
            l_c = pick(l)
            _store_halves(o_ref, 0, pick(pv) / l_c, rows)
            _store_halves(lse_ref, 0, pick(m) + jnp.log(l_c), rows)

    if dil * n_sub <= 4:
        run([(j, r) for j in range(n_sub) for r in range(dil)])
    else:
        per_trip = 4
        for j in range(n_sub):
            def body(it, carry, j=j):
                run([(j, it * per_trip + u) for u in range(per_trip)])
                return carry

            lax.fori_loop(0, dil // per_trip, body, 0)


def _prompt_attn(q, kv, batch, seq, dil):
    span = ATTN_BLK * dil
    has_prev = seq > span
    n_sub = max(1, 512 // span) if has_prev else 1
    rows = n_sub * span
    steps = seq // rows
    cur = lambda b, n: (0, b * steps + n, 0)
    in_specs = [_halves_spec(LANE_SPLIT, rows, cur), _halves_spec(2 * LANE_SPLIT, rows, cur)]
    args = [q, kv]
    if has_prev:
        in_specs.append(_halves_spec(2 * LANE_SPLIT, span,
                                     lambda b, n: (0, b * (seq // span) + jnp.maximum(n * n_sub - 1, 0), 0)))
        args.append(kv)
    return pl.pallas_call(
        functools.partial(_pattn_kernel, dil=dil, n_sub=n_sub, has_prev=has_prev),
        grid=(batch, steps),
        in_specs=in_specs,
        out_specs=[_halves_spec(LANE_SPLIT, rows, cur)] * 2,
        out_shape=[jax.ShapeDtypeStruct((LANE_SPLIT, batch * seq, LANES), F32)] * 2,
        compiler_params=_cparams(("arbitrary",) * 2),
        name=f"prompt_attn_d{dil}",
    )(*args)


def _sattn_kernel(q_ref, kvn_ref, ct_ref, o_ref, lse_ref, qm, osel, lsel, *, dil, bb_n, t_n):
    n_buf = ct_ref.shape[-1]
    hp = HEADS_PER_GROUP
    head = lax.broadcasted_iota(jnp.int32, (1, GROUP_W), 1) // HEAD_DIM
    sel = head == lax.broadcasted_iota(jnp.int32, (hp, 1), 0)
    row = lax.broadcasted_iota(jnp.int32, (t_n * hp, 1), 0)
    trow = row // hp
    sel_rows = head == row % hp
    pos = lax.broadcasted_iota(jnp.int32, (1, n_buf), 1)
    cmask = (pos >= trow) if dil == 1 else ((pos & (dil - 1)) == trow)
    qmbs, scores = [], []
    for bb in range(bb_n):
        for t in range(t_n):
            r = bb * t_n + t
            qm[bb, t * hp:(t + 1) * hp, :] = jnp.where(sel, _load_halves(q_ref, 0, slice(r, r + 1)), 0.0)
        qmbs.append(qm[bb].astype(BF16))
        scores.append(jnp.where(cmask, _dot(qmbs[bb], ct_ref[bb, 0].astype(BF16)), NEG))
    stats = []
    for bb in range(bb_n):
        qmr = qmbs[bb].astype(F32)
        m = jnp.max(scores[bb], axis=-1, keepdims=True)
        s_n = []
        for u in range(t_n):
            r = bb * t_n + u
            kn = _load_halves(kvn_ref, 0, slice(r, r + 1)).astype(BF16).astype(F32)
            valid = (trow >= u) if dil == 1 else (trow == u)
            sn = jnp.where(valid, jnp.sum(qmr * kn, axis=-1, keepdims=True), NEG)
            s_n.append(sn)
            m = jnp.maximum(m, sn)
        p_c = jnp.exp(scores[bb] - m)
        stats.append((m, s_n, jnp.sum(p_c, axis=-1, keepdims=True), p_c.astype(BF16)))
    accs = [_dot_t(stats[bb][3], ct_ref[bb, 1].astype(BF16)) for bb in range(bb_n)]
    for bb in range(bb_n):
        m, s_n, l, _ = stats[bb]
        acc = accs[bb]
        for u in range(t_n):
            p_n = jnp.exp(s_n[u] - m)
            l = l + p_n
            r = bb * t_n + u
            vn = _load_halves(kvn_ref, LANE_SPLIT, slice(r, r + 1)).astype(BF16).astype(F32)
            acc = acc + p_n.astype(BF16).astype(F32) * vn
        osel[bb] = jnp.where(sel_rows, acc / l, 0.0)
        lsel[bb] = jnp.where(sel_rows, m + jnp.log(l), 0.0)
        for t in range(t_n):
            r = slice(bb * t_n + t, bb * t_n + t + 1)
            _store_halves(o_ref, 0, jnp.sum(osel[bb, t * hp:(t + 1) * hp, :], axis=0, keepdims=True), r)
            _store_halves(lse_ref, 0, jnp.sum(lsel[bb, t * hp:(t + 1) * hp, :], axis=0, keepdims=True), r)


def _sample_attn(q, kvn, cache, batch, t_n, dil, bb_n):
    n_buf = cache.shape[1]
    assert n_buf == ATTN_BLK * dil and (dil == 1 or t_n <= dil)
    ct = jnp.transpose(cache, (0, 2, 3, 4, 1)).reshape(batch, 2, GROUP_W, n_buf)
    rows_n = t_n * HEADS_PER_GROUP
    blk = lambda i: (0, i, 0)
    return pl.pallas_call(
        functools.partial(_sattn_kernel, dil=dil, bb_n=bb_n, t_n=t_n),
        grid=(batch // bb_n,),
        in_specs=[_halves_spec(LANE_SPLIT, bb_n * t_n, blk), _halves_spec(2 * LANE_SPLIT, bb_n * t_n, blk),
                  pl.BlockSpec((bb_n, 2, GROUP_W, n_buf), lambda i: (i, 0, 0, 0))],
        out_specs=[_halves_spec(LANE_SPLIT, bb_n * t_n, blk)] * 2,
        out_shape=[jax.ShapeDtypeStruct((LANE_SPLIT, batch * t_n, LANES), F32)] * 2,
        scratch_shapes=[pltpu.VMEM((bb_n, rows_n, GROUP_W), F32)] * 3,
        compiler_params=_cparams(("arbitrary",)),
        name=f"sample_attn_d{dil}",
    )(q, kvn, ct)


def _ssm_kernel(u_ref, s0r_ref, s0i_ref, lre_ref, lim_ref, ldt_ref, bre_ref, bim_ref, cre_ref, cim_ref, d_ref,
                y_ref, fr_ref, fi_ref, hre, him, str_, sti, *, n_seq, steps):
    i = pl.program_id(0)

    @pl.when(i == 0)
    def _():
        str_[...] = s0r_ref[...]
        sti[...] = s0i_ref[...]

    lam_re, lam_im = lre_ref[...], lim_ref[...]
    dt = jnp.exp(ldt_ref[...])
    mag = jnp.exp(lam_re * dt)
    a_re = mag * jnp.cos(lam_im * dt)
    a_im = mag * jnp.sin(lam_im * dt)
    den = lam_re * lam_re + lam_im * lam_im
    nr, ni = a_re - 1.0, a_im
    c_re = (nr * lam_re + ni * lam_im) / den
    c_im = (ni * lam_re - nr * lam_im) / den

    u = u_ref[...]
    ub = u.astype(BF16)
    bu_re = _dot(ub, bre_ref[...])
    bu_im = _dot(ub, bim_ref[...])
    hre[...] = c_re * bu_re - c_im * bu_im
    him[...] = c_re * bu_im + c_im * bu_re

    if n_seq <= 8:
        def step(t, carry):
            sr, si = carry
            rows = pl.ds(pl.multiple_of(t * n_seq, n_seq), n_seq)
            nr_ = a_re * sr - a_im * si + hre[rows, :]
            ni_ = a_re * si + a_im * sr + him[rows, :]
            hre[rows, :] = nr_
            him[rows, :] = ni_
            return nr_, ni_

        sr, si = lax.fori_loop(0, steps, step, (str_[...], sti[...]))
        str_[...] = sr
        sti[...] = si
    else:
        for t in range(steps):
            rows = slice(t * n_seq, (t + 1) * n_seq)
            sr, si = str_[...], sti[...]
            nr_ = a_re * sr - a_im * si + hre[rows, :]
            ni_ = a_re * si + a_im * sr + him[rows, :]
            hre[rows, :] = nr_
            him[rows, :] = ni_
            str_[...] = nr_
            sti[...] = ni_

    y = _dot(hre[...].astype(BF16), cre_ref[...]) - _dot(him[...].astype(BF16), cim_ref[...])
    y_ref[...] = y + d_ref[...] * u
    fr_ref[...] = str_[...]
    fi_ref[...] = sti[...]


def _ssm(u_tm, s0r, s0i, lam_re, lam_im, log_dt, b_re, b_im, c_re, c_im, dvec, n_seq, n_steps, chunk):
    rows = chunk * n_seq
    fixed = lambda i: (0, 0)
    sds = jax.ShapeDtypeStruct
    return pl.pallas_call(
        functools.partial(_ssm_kernel, n_seq=n_seq, steps=chunk),
        grid=(n_steps // chunk,),
        in_specs=[pl.BlockSpec((rows, SSM_WIDTH), lambda i: (i, 0)),
                  pl.BlockSpec((n_seq, SSM_LANES), fixed), pl.BlockSpec((n_seq, SSM_LANES), fixed),
                  pl.BlockSpec((1, SSM_LANES), fixed), pl.BlockSpec((1, SSM_LANES), fixed),
                  pl.BlockSpec((1, SSM_LANES), fixed),
                  pl.BlockSpec((SSM_WIDTH, SSM_LANES), fixed), pl.BlockSpec((SSM_WIDTH, SSM_LANES), fixed),
                  pl.BlockSpec((SSM_LANES, SSM_WIDTH), fixed), pl.BlockSpec((SSM_LANES, SSM_WIDTH), fixed),
                  pl.BlockSpec((1, SSM_WIDTH), fixed)],
        out_specs=[pl.BlockSpec((rows, SSM_WIDTH), lambda i: (i, 0)),
                   pl.BlockSpec((n_seq, SSM_LANES), fixed), pl.BlockSpec((n_seq, SSM_LANES), fixed)],
        out_shape=[sds((n_steps * n_seq, SSM_WIDTH), F32), sds((n_seq, SSM_LANES), F32),
                   sds((n_seq, SSM_LANES), F32)],
        scratch_shapes=[pltpu.VMEM((rows, SSM_LANES), F32), pltpu.VMEM((rows, SSM_LANES), F32),
                        pltpu.VMEM((n_seq, SSM_LANES), F32), pltpu.VMEM((n_seq, SSM_LANES), F32)],
        compiler_params=_cparams(("arbitrary",)),
        name=f"ssm_scan_{n_seq}",
    )(u_tm, s0r, s0i, lam_re, lam_im, log_dt, b_re, b_im, c_re, c_im, dvec)


def _mixout_value(o_refs, l_refs, y_ref, x_ref, ga_ref, gs_ref, wglu_ref, bglu_ref, wout_ref):
    full = slice(None)
    y = y_ref[...]
    zs = 0.5 * y * (1.0 + jnp.tanh(math.sqrt(2.0 / math.pi) * (y + 0.044715 * (y * y * y))))
    glu = _dot(zs.astype(BF16), wglu_ref[...])
    l0, l1, l2 = (_load_halves(r, 0, full) for r in l_refs)
    m = jnp.maximum(jnp.maximum(l0, l1), l2)
    e0, e1, e2 = jnp.exp(l0 - m), jnp.exp(l1 - m), jnp.exp(l2 - m)
    den = e0 + e1 + e2
    a = [_load_halves(r, 0, full) * (e / den) for r, e in zip(o_refs, (e0, e1, e2))]
    ssq = sum(jnp.sum(ai * ai, axis=-1, keepdims=True) for ai in a)
    inv = lax.rsqrt(ssq * (1.0 / ATTN_WIDTH) + EPS)
    acc = x_ref[...]
    for g in range(N_GROUPS):
        an = (a[g] * inv * ga_ref[:, g * GROUP_W:(g + 1) * GROUP_W]).astype(BF16)
        acc = acc + _dot(an, wout_ref[g * GROUP_W:(g + 1) * GROUP_W, :])
    gate = jax.nn.sigmoid(glu + bglu_ref[...])
    so = _rms_rows(zs * gate, gs_ref[...]).astype(BF16)
    return acc + _dot(so, wout_ref[ATTN_WIDTH:, :])


def _mixout_kernel(o0_ref, o1_ref, o2_ref, l0_ref, l1_ref, l2_ref, y_ref, x_ref,
                   ga_ref, gs_ref, wglu_ref, bglu_ref, wout_ref, h_ref):
    h_ref[...] = _mixout_value((o0_ref, o1_ref, o2_ref), (l0_ref, l1_ref, l2_ref), y_ref, x_ref,
                               ga_ref, gs_ref, wglu_ref, bglu_ref, wout_ref)


def _mix_out(os_, ls_, y, x2d, g_attn, g_ssm, wglu_bf, bglu, wout_bf, tm):
    n = x2d.shape[0]
    row = lambda i: (i, 0)
    fixed = lambda i: (0, 0)
    return pl.pallas_call(
        _mixout_kernel,
        grid=(n // tm,),
        in_specs=[_halves_spec(LANE_SPLIT, tm, lambda i: (0, i, 0))] * 6
                 + [pl.BlockSpec((tm, SSM_WIDTH), row), pl.BlockSpec((tm, D_MODEL), row)]
                 + [pl.BlockSpec((1, ATTN_WIDTH), fixed), pl.BlockSpec((1, SSM_WIDTH), fixed),
                    pl.BlockSpec((SSM_WIDTH, SSM_WIDTH), fixed), pl.BlockSpec((1, SSM_WIDTH), fixed),
                    pl.BlockSpec((D_MODEL, D_MODEL), fixed)],
        out_specs=pl.BlockSpec((tm, D_MODEL), row),
        out_shape=jax.ShapeDtypeStruct((n, D_MODEL), F32),
        compiler_params=_cparams(("arbitrary",)),
        name="mix_out",
    )(*os_, *ls_, y, x2d, g_attn, g_ssm, wglu_bf, bglu, wout_bf)


def _memkv_kernel(x_ref, g_ref, w_ref, kn_ref, o_ref):
    xn = _rms_rows(x_ref[...], g_ref[...]).astype(BF16)
    kn = kn_ref[...]
    for h in range(X_HEADS):
        cols = slice(h * X_HEAD_DIM, (h + 1) * X_HEAD_DIM)
        o_ref[:, cols] = _rms_rows(_dot(xn, w_ref[:, cols]), kn)
    o_ref[:, D_MODEL:] = _dot(xn, w_ref[:, D_MODEL:])


def _memory_kv(mem2d, gain, w_bf, kn, tm):
    n = mem2d.shape[0]
    fixed = lambda i: (0, 0)
    return pl.pallas_call(
        _memkv_kernel,
        grid=(n // tm,),
        in_specs=[pl.BlockSpec((tm, D_MODEL), lambda i: (i, 0)), pl.BlockSpec((1, D_MODEL), fixed),
                  pl.BlockSpec((D_MODEL, 2 * D_MODEL), fixed), pl.BlockSpec((1, X_HEAD_DIM), fixed)],
        out_specs=pl.BlockSpec((tm, 2 * D_MODEL), lambda i: (i, 0)),
        out_shape=jax.ShapeDtypeStruct((n, 2 * D_MODEL), F32),
        compiler_params=_cparams(("arbitrary",)),
        name="memory_kv",
    )(mem2d, gain, w_bf, kn)


def _mix_cross_kernel(o0_ref, o1_ref, o2_ref, l0_ref, l1_ref, l2_ref, y_ref, x_ref,
                      ga_ref, gs_ref, wglu_ref, bglu_ref, wout_ref,
                      mkv_ref, gx_ref, wq_ref, qn_ref, wo_ref, o_ref, att):
    h = _mixout_value((o0_ref, o1_ref, o2_ref), (l0_ref, l1_ref, l2_ref), y_ref, x_ref,
                      ga_ref, gs_ref, wglu_ref, bglu_ref, wout_ref)
    hn = _rms_rows(h, gx_ref[...]).astype(BF16)
    qn = qn_ref[...]
    heads = [slice(hd * X_HEAD_DIM, (hd + 1) * X_HEAD_DIM) for hd in range(X_HEADS)]
    qz = [_dot(hn, wq_ref[:, cols]) for cols in heads]
    qs = [(_rms_rows(z, qn) * (X_HEAD_DIM ** -0.5)).astype(BF16) for z in qz]
    scores = [_dot_t(q, mkv_ref[:, cols].astype(BF16)) for q, cols in zip(qs, heads)]
    probs = []
    for s in scores:
        p = jnp.exp(s - jnp.max(s, axis=-1, keepdims=True))
        probs.append((p.astype(BF16), jnp.sum(p, axis=-1, keepdims=True)))
    outs = [_dot(p, mkv_ref[:, D_MODEL + cols.start:D_MODEL + cols.stop].astype(BF16)) for (p, _), cols in zip(probs, heads)]
    for pv, (_, l), cols in zip(outs, probs, heads):
        att[:, cols] = (pv / l).astype(BF16)
    o_ref[...] = h + _dot(att[...], wo_ref[...])


def _mix_cross(os_, ls_, y, x2d, mkv, p, rows_per_seq, tm):
    n = x2d.shape[0]
    blocks_per_seq = rows_per_seq // tm
    row = lambda i: (i, 0)
    fixed = lambda i: (0, 0)
    return pl.pallas_call(
        _mix_cross_kernel,
        grid=(n // tm,),
        in_specs=[_halves_spec(LANE_SPLIT, tm, lambda i: (0, i, 0))] * 6
                 + [pl.BlockSpec((tm, SSM_WIDTH), row), pl.BlockSpec((tm, D_MODEL), row)]
                 + [pl.BlockSpec((1, ATTN_WIDTH), fixed), pl.BlockSpec((1, SSM_WIDTH), fixed),
                    pl.BlockSpec((SSM_WIDTH, SSM_WIDTH), fixed), pl.BlockSpec((1, SSM_WIDTH), fixed),
                    pl.BlockSpec((D_MODEL, D_MODEL), fixed)]
                 + [pl.BlockSpec((None, N_MEM, 2 * D_MODEL), lambda i: (i // blocks_per_seq, 0, 0)),
                    pl.BlockSpec((1, D_MODEL), fixed), pl.BlockSpec((D_MODEL, D_MODEL), fixed),
                    pl.BlockSpec((1, X_HEAD_DIM), fixed), pl.BlockSpec((D_MODEL, D_MODEL), fixed)],
        out_specs=pl.BlockSpec((tm, D_MODEL), row),
        out_shape=jax.ShapeDtypeStruct((n, D_MODEL), F32),
        scratch_shapes=[pltpu.VMEM((tm, D_MODEL), BF16)],
        compiler_params=_cparams(("arbitrary",)),
        name="mix_cross",
    )(*os_, *ls_, y, x2d, p['attn_out_norm'], p['ssm_out_norm'], p['w_glu'], p['b_glu'], p['w_out'],
      mkv, p['norm_x'], p['w_xq'], p['xq_norm'], p['w_xo'])


MEM_HALVES = X_HEAD_DIM // 128
MEM_ROW_STRIDE = 2 * MEM_HALVES * X_HEADS


def _cross_sample_kernel(h_ref, mem_ref, g_ref, wq_ref, qn_ref, wo_ref, o_ref, qsc, qm, osel, att, *, bb_n, t_n):
    h = h_ref[...]
    hn = _rms_rows(h, g_ref[...]).astype(BF16)
    qn = qn_ref[...]
    for hd in range(X_HEADS):
        cols = slice(hd * X_HEAD_DIM, (hd + 1) * X_HEAD_DIM)
        qsc[:, cols] = _rms_rows(_dot(hn, wq_ref[:, cols]), qn) * (X_HEAD_DIM ** -0.5)
    xh = X_HEADS
    head = lax.broadcasted_iota(jnp.int32, (1, D_MODEL), 1) // X_HEAD_DIM
    sel = head == lax.broadcasted_iota(jnp.int32, (xh, 1), 0)
    sel_rows = head == lax.broadcasted_iota(jnp.int32, (t_n * xh, 1), 0) % xh

    def gather(bb, kv):
        parts = [mem_ref[bb, pl.ds((kv * MEM_HALVES + half) * xh + hd, N_MEM, stride=MEM_ROW_STRIDE), :]
                 for hd in range(xh) for half in range(MEM_HALVES)]
        return jnp.concatenate(parts, axis=1).astype(BF16)

    scores = []
    for bb in range(bb_n):
        for t in range(t_n):
            r = bb * t_n + t
            qm[bb, t * xh:(t + 1) * xh, :] = jnp.where(sel, qsc[r:r + 1, :], 0.0)
        scores.append(_dot_t(qm[bb].astype(BF16), gather(bb, 0)))
    probs = []
    for s in scores:
        p = jnp.exp(s - jnp.max(s, axis=-1, keepdims=True))
        probs.append((p.astype(BF16), jnp.sum(p, axis=-1, keepdims=True)))
    outs = [_dot(probs[bb][0], gather(bb, 1)) for bb in range(bb_n)]
    for bb in range(bb_n):
        osel[bb] = jnp.where(sel_rows, outs[bb] / probs[bb][1], 0.0)
        for t in range(t_n):
            r = bb * t_n + t
            att[r:r + 1, :] = jnp.sum(osel[bb, t * xh:(t + 1) * xh, :], axis=0, keepdims=True)
    o_ref[...] = h + _dot(att[...].astype(BF16), wo_ref[...])


def _cross_sample(h2d, mem_kv, gain, wq_bf, qn, wo_bf, batch, t_n, bb_n):
    mem = mem_kv.reshape(batch, N_MEM, 2, X_HEADS, MEM_HALVES, 128)
    mem = jnp.transpose(mem, (0, 1, 2, 4, 3, 5)).reshape(batch, N_MEM * MEM_ROW_STRIDE, 128)
    rows = bb_n * t_n
    fixed = lambda i: (0, 0)
    return pl.pallas_call(
        functools.partial(_cross_sample_kernel, bb_n=bb_n, t_n=t_n),
        grid=(batch // bb_n,),
        in_specs=[pl.BlockSpec((rows, D_MODEL), lambda i: (i, 0)),
                  pl.BlockSpec((bb_n, N_MEM * MEM_ROW_STRIDE, 128), lambda i: (i, 0, 0)),
                  pl.BlockSpec((1, D_MODEL), fixed), pl.BlockSpec((D_MODEL, D_MODEL), fixed),
                  pl.BlockSpec((1, X_HEAD_DIM), fixed), pl.BlockSpec((D_MODEL, D_MODEL), fixed)],
        out_specs=pl.BlockSpec((rows, D_MODEL), lambda i: (i, 0)),
        out_shape=jax.ShapeDtypeStruct((batch * t_n, D_MODEL), F32),
        scratch_shapes=[pltpu.VMEM((rows, D_MODEL), F32), pltpu.VMEM((bb_n, t_n * X_HEADS, D_MODEL), F32),
                        pltpu.VMEM((bb_n, t_n * X_HEADS, D_MODEL), F32), pltpu.VMEM((rows, D_MODEL), F32)],
        compiler_params=_cparams(("arbitrary",)),
        name="cross_attn_sample",
    )(h2d, mem, gain, wq_bf, qn, wo_bf)


FFN_CHUNKS = 2
FFN_TF = D_FF // FFN_CHUNKS
CARRY_ROWS = 8


def _conv_gate(up_g, up_v, prev_g, prev_v, bufg, bufv, cwg_ref, cwv_ref, cbg_ref, cbv_ref, tm, lag):
    hist = prev_g.shape[0]
    bufg[0:hist] = prev_g
    bufg[hist:hist + tm] = up_g
    bufv[0:hist] = prev_v
    bufv[hist:hist + tm] = up_v
    cg = cbg_ref[...] + (cwg_ref[0:1, :] * bufg[hist - 2 * lag:hist - 2 * lag + tm]
                         + cwg_ref[1:2, :] * bufg[hist - lag:hist - lag + tm] + cwg_ref[2:3, :] * up_g)
    cv = cbv_ref[...] + (cwv_ref[0:1, :] * bufv[hist - 2 * lag:hist - 2 * lag + tm]
                         + cwv_ref[1:2, :] * bufv[hist - lag:hist - lag + tm] + cwv_ref[2:3, :] * up_v)
    return (cg * jax.nn.sigmoid(cg) * cv).astype(BF16)


FFN_COLS = 256
FFN_NCHUNK = D_FF // FFN_COLS


def _ffn_prompt_kernel(h_ref, g_ref, wup_ref, cw_ref, cb_ref, wd_ref, o_ref, tail_ref, act, carry, *, tm):
    i = pl.program_id(1)
    h = h_ref[...]
    xn = _rms_rows(h, g_ref[...]).astype(BF16)

    @pl.when(i == 0)
    def _():
        carry[...] = jnp.zeros(carry.shape, F32)

    for n in range(FFN_NCHUNK):
        conv = []
        for part in range(2):
            cols = slice(part * D_FF + n * FFN_COLS, part * D_FF + (n + 1) * FFN_COLS)
            up = _dot(xn, wup_ref[:, cols])
            ext = jnp.concatenate([carry[part, n], up], axis=0)
            conv.append(cb_ref[:, cols] + (cw_ref[0:1, cols] * ext[CARRY_ROWS - 2:CARRY_ROWS - 2 + tm]
                                           + cw_ref[1:2, cols] * ext[CARRY_ROWS - 1:CARRY_ROWS - 1 + tm]
                                           + cw_ref[2:3, cols] * up))
            carry[part, n] = up[tm - CARRY_ROWS:]
            tail_ref[:, cols] = up[tm - CARRY_ROWS:]
        cg, cv = conv
        act[:, n * FFN_COLS:(n + 1) * FFN_COLS] = (cg * jax.nn.sigmoid(cg) * cv).astype(BF16)
    o_ref[...] = h + _dot(act[...], wd_ref[...])


def _ffn_prompt(h2d, gain, wup_bf, conv_w, conv_b, wdown_bf, batch, seq, tm):
    n_i = seq // tm
    row = lambda b, i: (b * n_i + i, 0)
    fixed = lambda b, i: (0, 0)
    once = pl.Buffered(1)
    sds = jax.ShapeDtypeStruct
    return pl.pallas_call(
        functools.partial(_ffn_prompt_kernel, tm=tm),
        grid=(batch, n_i),
        in_specs=[pl.BlockSpec((tm, D_MODEL), row), pl.BlockSpec((1, D_MODEL), fixed),
                  pl.BlockSpec((D_MODEL, 2 * D_FF), fixed, pipeline_mode=once),
                  pl.BlockSpec((CONV_W, 2 * D_FF), fixed), pl.BlockSpec((1, 2 * D_FF), fixed),
                  pl.BlockSpec((D_FF, D_MODEL), fixed, pipeline_mode=once)],
        out_specs=[pl.BlockSpec((tm, D_MODEL), row),
                   pl.BlockSpec((None, None, CARRY_ROWS, 2 * D_FF), lambda b, i: (b, i, 0, 0))],
        out_shape=[sds((batch * seq, D_MODEL), F32), sds((batch, n_i, CARRY_ROWS, 2 * D_FF), F32)],
        scratch_shapes=[pltpu.VMEM((tm, D_FF), BF16), pltpu.VMEM((2, FFN_NCHUNK, CARRY_ROWS, FFN_COLS), F32)],
        compiler_params=_cparams(("arbitrary",) * 2),
        name="ffn_prompt",
    )(h2d, gain, wup_bf, conv_w, conv_b, wdown_bf)


def _ffn_sample_kernel(h_ref, g_ref, sg_ref, sv_ref, wg_ref, wv_ref, cwg_ref, cwv_ref, cbg_ref, cbv_ref, wd_ref,
                       o_ref, ng_ref, nv_ref, xn, bufg, bufv, *, n_seq, rows):
    c = pl.program_id(0)

    @pl.when(c == 0)
    def _():
        xn[...] = _rms_rows(h_ref[...], g_ref[...]).astype(BF16)

    up_g = _dot(xn[...], wg_ref[...])
    up_v = _dot(xn[...], wv_ref[...])
    act = _conv_gate(up_g, up_v, sg_ref[...], sv_ref[...], bufg, bufv, cwg_ref, cwv_ref, cbg_ref, cbv_ref,
                     rows, n_seq)
    hist = (CONV_W - 1) * n_seq
    ng_ref[...] = bufg[rows:rows + hist]
    nv_ref[...] = bufv[rows:rows + hist]
    contrib = _dot(act, wd_ref[...])

    @pl.when(c == 0)
    def _():
        o_ref[...] = h_ref[...] + contrib

    @pl.when(c > 0)
    def _():
        o_ref[...] += contrib


def _ffn_sample(h_tm, gain, state_tm, wup_bf, conv_w, conv_b, wdown_bf, n_seq, n_steps):
    rows = n_seq * n_steps
    hist = (CONV_W - 1) * n_seq
    nc, tf = FFN_CHUNKS, FFN_TF
    gcol = lambda c: (0, c)
    vcol = lambda c: (0, nc + c)
    fixed = lambda c: (0, 0)
    sds = jax.ShapeDtypeStruct
    return pl.pallas_call(
        functools.partial(_ffn_sample_kernel, n_seq=n_seq, rows=rows),
        grid=(nc,),
        in_specs=[pl.BlockSpec((rows, D_MODEL), fixed), pl.BlockSpec((1, D_MODEL), fixed),
                  pl.BlockSpec((hist, tf), gcol), pl.BlockSpec((hist, tf), vcol),
                  pl.BlockSpec((D_MODEL, tf), gcol), pl.BlockSpec((D_MODEL, tf), vcol),
                  pl.BlockSpec((CONV_W, tf), gcol), pl.BlockSpec((CONV_W, tf), vcol),
                  pl.BlockSpec((1, tf), gcol), pl.BlockSpec((1, tf), vcol),
                  pl.BlockSpec((tf, D_MODEL), lambda c: (c, 0))],
        out_specs=[pl.BlockSpec((rows, D_MODEL), fixed), pl.BlockSpec((hist, tf), gcol),
                   pl.BlockSpec((hist, tf), gcol)],
        out_shape=[sds((rows, D_MODEL), F32), sds((hist, D_FF), F32), sds((hist, D_FF), F32)],
        scratch_shapes=[pltpu.VMEM((rows, D_MODEL), BF16), pltpu.VMEM((rows + hist, tf), F32),
                        pltpu.VMEM((rows + hist, tf), F32)],
        compiler_params=_cparams(("arbitrary",)),
        name="ffn_sample",
    )(h_tm, gain, state_tm, state_tm, wup_bf, wup_bf, conv_w, conv_w, conv_b, conv_b, wdown_bf)


def _rope_tables(pos):
    half = ROT_DIM // 2
    inv = jnp.power(jnp.float32(ROPE_THETA), -jnp.arange(half, dtype=F32) * 2.0 / ROT_DIM)
    ang = pos.astype(F32)[:, None] * inv[None, :]
    cos, sin = jnp.cos(ang), jnp.sin(ang)
    ones = jnp.ones((pos.shape[0], HEAD_DIM - ROT_DIM), F32)
    zeros = jnp.zeros((pos.shape[0], HEAD_DIM - ROT_DIM), F32)
    zh = jnp.zeros_like(sin)
    c_head = jnp.concatenate([cos, cos, ones], axis=1)
    s1_head = jnp.concatenate([-sin, zh, zeros], axis=1)
    s2_head = jnp.concatenate([zh, sin, zeros], axis=1)
    tile = lambda a: jnp.tile(a, (1, HEADS_PER_GROUP))
    return tile(c_head), tile(s1_head), tile(s2_head)


def _block_diag(w):
    g, a, b = w.shape
    eye = jnp.eye(g, dtype=w.dtype)
    return (eye[:, None, :, None] * w[:, :, None, :]).reshape(g * a, g * b)


def _layer_params(l, norm_mix, w_in, q_norm, k_norm, ssm_lam_re, ssm_lam_im, ssm_log_dt, ssm_b_re, ssm_b_im,
                  ssm_c_re, ssm_c_im, ssm_d, w_glu, b_glu, attn_out_norm, ssm_out_norm, w_out, norm_x, norm_mem,
                  w_xq, w_xkv, xq_norm, xk_norm, w_xo, norm_ffn, w_up, conv_w, conv_b, w_down):
    row = lambda a: a.reshape(1, -1).astype(F32)
    return dict(
        norm_mix=row(norm_mix[l]), w_in=w_in[l].astype(BF16),
        q_norm=row(jnp.tile(q_norm[l], HEADS_PER_GROUP)), k_norm=row(jnp.tile(k_norm[l], HEADS_PER_GROUP)),
        lam_re=row(ssm_lam_re[l]), lam_im=row(ssm_lam_im[l]),
        log_dt=row(jnp.repeat(ssm_log_dt[l], SSM_STATE)),
        b_re=_block_diag(jnp.swapaxes(ssm_b_re[l], 1, 2)).astype(BF16),
        b_im=_block_diag(jnp.swapaxes(ssm_b_im[l], 1, 2)).astype(BF16),
        c_re=_block_diag(jnp.swapaxes(ssm_c_re[l], 1, 2)).astype(BF16),
        c_im=_block_diag(jnp.swapaxes(ssm_c_im[l], 1, 2)).astype(BF16),
        ssm_d=row(ssm_d[l]), w_glu=w_glu[l].astype(BF16), b_glu=row(b_glu[l]),
        attn_out_norm=row(attn_out_norm[l]), ssm_out_norm=row(ssm_out_norm[l]), w_out=w_out[l].astype(BF16),
        norm_x=row(norm_x[l]), norm_mem=row(norm_mem[l]), w_xq=w_xq[l].astype(BF16),
        w_xkv=w_xkv[l].astype(BF16), xq_norm=row(xq_norm[l]), xk_norm=row(xk_norm[l]),
        w_xo=w_xo[l].astype(BF16), norm_ffn=row(norm_ffn[l]), w_up=w_up[l].astype(BF16),
        conv_w=conv_w[l].astype(F32), conv_b=row(conv_b[l]), w_down=w_down[l].astype(BF16))


def _kv_rows(kv_halves, n_seq, n_steps):
    heads_per_half = LANES // HEAD_DIM
    a = kv_halves.reshape(2, LANE_SPLIT, n_seq, n_steps, heads_per_half, HEAD_DIM)
    return jnp.transpose(a, (2, 3, 0, 1, 4, 5)).reshape(n_seq, n_steps, 2, HEADS_PER_GROUP, HEAD_DIM)


def _to_time_major(a, n_seq, n_steps):
    return a.reshape(n_seq, n_steps, -1).swapaxes(0, 1).reshape(n_steps * n_seq, -1)


def _to_seq_major(a, n_seq, n_steps):
    return a.reshape(n_steps, n_seq, -1).swapaxes(0, 1).reshape(n_seq * n_steps, -1)


def _mixer_front(x2d, p, tables, tm):
    return _in_proj(x2d, p['norm_mix'], p['w_in'], p['q_norm'], p['k_norm'], *tables, tm)


def _run_ssm(u, s0r, s0i, p, n_seq, n_steps, chunk):
    u_tm = _to_time_major(u, n_seq, n_steps)
    y_tm, fr, fi = _ssm(u_tm, s0r, s0i, p['lam_re'], p['lam_im'], p['log_dt'], p['b_re'], p['b_im'],
                        p['c_re'], p['c_im'], p['ssm_d'], n_seq, n_steps, chunk)
    return _to_seq_major(y_tm, n_seq, n_steps), fr, fi


def kernel(x_prompt, x_sample, mem_prompt, cache_kv_dil1, cache_kv_dil4, cache_kv_dil16, state_ssm_re, state_ssm_im, state_ffn_conv, cache_mem_kv, norm_mix, w_in, q_norm, k_norm, ssm_lam_re, ssm_lam_im, ssm_log_dt, ssm_b_re, ssm_b_im, ssm_c_re, ssm_c_im, ssm_d, w_glu, b_glu, attn_out_norm, ssm_out_norm, w_out, norm_x, norm_mem, w_xq, w_xkv, xq_norm, xk_norm, w_xo, norm_ffn, w_up, conv_w, conv_b, w_down):
    bp, lp, _ = x_prompt.shape
    bs, ls, _ = x_sample.shape
    depth = w_in.shape[0]
    past_len = cache_kv_dil16.shape[2]
    caches = (cache_kv_dil1, cache_kv_dil4, cache_kv_dil16)
    tables_p = _rope_tables(jnp.arange(lp))
    tables_s = tuple(jnp.tile(t, (bs, 1)) for t in _rope_tables(past_len + jnp.arange(ls)))

    yp = x_prompt.reshape(bp * lp, D_MODEL)
    ys = x_sample.reshape(bs * ls, D_MODEL)
    kv_p = [[] for _ in range(N_GROUPS)]
    kv_s = [[] for _ in range(N_GROUPS)]
    re_p, im_p, conv_p, mem_p, re_s, im_s, conv_s = [], [], [], [], [], [], []
    for l in range(depth):
        p = _layer_params(l, norm_mix, w_in, q_norm, k_norm, ssm_lam_re, ssm_lam_im, ssm_log_dt, ssm_b_re,
                          ssm_b_im, ssm_c_re, ssm_c_im, ssm_d, w_glu, b_glu, attn_out_norm, ssm_out_norm, w_out,
                          norm_x, norm_mem, w_xq, w_xkv, xq_norm, xk_norm, w_xo, norm_ffn, w_up, conv_w, conv_b,
                          w_down)
        mkv = _memory_kv(mem_prompt.reshape(bp * N_MEM, D_MODEL), p['norm_mem'], p['w_xkv'], p['xk_norm'], 256)
        mkv = mkv.reshape(bp, N_MEM, 2 * D_MODEL)
        mem_p.append(mkv.reshape(bp, N_MEM, 2, X_HEADS, X_HEAD_DIM))
        q0, q1, q2, kv0, kv1, kv2, u = _mixer_front(yp, p, tables_p, 512)
        os_, ls_ = [], []
        for g, (qg, kvg) in enumerate(((q0, kv0), (q1, kv1), (q2, kv2))):
            win, dil = ATTN_GROUPS[g]
            o, lse = _prompt_attn(qg, kvg, bp, lp, dil)
            os_.append(o)
            ls_.append(lse)
            keep = min(win, lp)
            kv_p[g].append(_kv_rows(kvg, bp, lp)[:, lp - keep:])
        zero_state = jnp.zeros((bp, SSM_LANES), F32)
        y, fr, fi = _run_ssm(u, zero_state, zero_state, p, bp, lp, 128)
        re_p.append(fr.reshape(bp, SSM_GROUPS, SSM_STATE))
        im_p.append(fi.reshape(bp, SSM_GROUPS, SSM_STATE))
        h = _mix_cross(os_, ls_, y, yp, mkv, p, lp, 512)
        yp, tail = _ffn_prompt(h, p['norm_ffn'], p['w_up'], p['conv_w'], p['conv_b'], p['w_down'], bp, lp, 512)
        conv_p.append(tail[:, -1, CARRY_ROWS - (CONV_W - 1):])
        q0, q1, q2, kv0, kv1, kv2, u = _mixer_front(ys, p, tables_s, bs * ls)
        os_, ls_ = [], []
        for g, (qg, kvg) in enumerate(((q0, kv0), (q1, kv1), (q2, kv2))):
            win, dil = ATTN_GROUPS[g]
            o, lse = _sample_attn(qg, kvg, caches[g][l], bs, ls, dil, max(1, 32 // dil))
            os_.append(o)
            ls_.append(lse)
            kv_s[g].append(_kv_rows(kvg, bs, ls))
        y, fr, fi = _run_ssm(u, state_ssm_re[l].reshape(bs, SSM_LANES), state_ssm_im[l].reshape(bs, SSM_LANES),
                             p, bs, ls, ls)
        re_s.append(fr.reshape(bs, SSM_GROUPS, SSM_STATE))
        im_s.append(fi.reshape(bs, SSM_GROUPS, SSM_STATE))
        h = _mix_out(os_, ls_, y, ys, p['attn_out_norm'], p['ssm_out_norm'], p['w_glu'], p['b_glu'], p['w_out'],
                     bs * ls)
        h = _cross_sample(h, cache_mem_kv[l], p['norm_x'], p['w_xq'], p['xq_norm'], p['w_xo'], bs, ls, 4)
        state_tm = state_ffn_conv[l].swapaxes(0, 1).reshape((CONV_W - 1) * bs, 2 * D_FF)
        ys_tm, ng, nv = _ffn_sample(_to_time_major(h, bs, ls), p['norm_ffn'], state_tm, p['w_up'], p['conv_w'],
                                    p['conv_b'], p['w_down'], bs, ls)
        ys = _to_seq_major(ys_tm, bs, ls)
        conv_s.append(jnp.concatenate([ng, nv], axis=-1).reshape(CONV_W - 1, bs, 2 * D_FF).swapaxes(0, 1))

    st = jnp.stack
    return (yp.reshape(bp, lp, D_MODEL), ys.reshape(bs, ls, D_MODEL),
            st(kv_p[0]), st(kv_p[1]), st(kv_p[2]), st(re_p), st(im_p), st(conv_p), st(mem_p),
            st(kv_s[0]), st(kv_s[1]), st(kv_s[2]), st(re_s), st(im_s), st(conv_s))
```

```python
import functools
import math

import jax
import jax.numpy as jnp
from jax import lax
from jax.experimental import pallas as pl
from jax.experimental.pallas import tpu as pltpu

F32 = jnp.float32
BF16 = jnp.bfloat16

D_MODEL = 1024
HEAD_DIM = 64
ATTN_GROUPS = ((128, 1), (512, 4), (2048, 16))
HEADS_PER_GROUP = 4
GROUP_W = HEADS_PER_GROUP * HEAD_DIM
N_GROUPS = len(ATTN_GROUPS)
ATTN_WIDTH = N_GROUPS * GROUP_W
SSM_WIDTH = D_MODEL - ATTN_WIDTH
SSM_GROUP = 16
SSM_GROUPS = SSM_WIDTH // SSM_GROUP
SSM_STATE = 64
SSM_LANES = SSM_GROUPS * SSM_STATE
IN_WIDTH = 3 * ATTN_WIDTH + SSM_WIDTH
ROT_DIM = HEAD_DIM // 4
ROPE_THETA = 500000.0
N_MEM = 256
X_HEADS = 4
X_HEAD_DIM = D_MODEL // X_HEADS
D_FF = 2816
CONV_W = 3
EPS = 1e-6
ATTN_BLK = 128
NEG = -1e30

VMEM_LIMIT = 56 * 1024 * 1024


def _cparams(sem):
    return pltpu.CompilerParams(dimension_semantics=sem, vmem_limit_bytes=VMEM_LIMIT)


def _dot(a, b):
    return jnp.dot(a, b, preferred_element_type=F32)


def _dot_t(a, b):
    return lax.dot_general(a, b, (((1,), (1,)), ((), ())), preferred_element_type=F32)


def _rms_rows(x, gain):
    y = x * lax.rsqrt(jnp.mean(x * x, axis=-1, keepdims=True) + EPS)
    return y * gain


LANES = 128
LANE_SPLIT = GROUP_W // LANES


def _halves_spec(n_halves, rows, index_map):
    return pl.BlockSpec((n_halves, rows, LANES), index_map)


def _load_halves(ref, first, rows):
    return jnp.concatenate([ref[first + c, rows, :] for c in range(LANE_SPLIT)], axis=1)


def _store_halves(ref, first, value, rows=slice(None)):
    for c in range(LANE_SPLIT):
        ref[first + c, rows, :] = value[:, c * LANES:(c + 1) * LANES]


def _head_ones():
    r = lax.broadcasted_iota(jnp.int32, (GROUP_W, GROUP_W), 0) // HEAD_DIM
    c = lax.broadcasted_iota(jnp.int32, (GROUP_W, GROUP_W), 1) // HEAD_DIM
    return (r == c).astype(BF16)


def _head_inv_rms(z, ones_head):
    sq = z * z
    hi = sq.astype(BF16)
    lo = (sq - hi.astype(F32)).astype(BF16)
    ssum = _dot(hi, ones_head) + _dot(lo, ones_head)
    return lax.rsqrt(ssum * (1.0 / HEAD_DIM) + EPS)


def _inproj_kernel(x_ref, g_ref, w_ref, qn_ref, kn_ref, c_ref, s1_ref, s2_ref,
                   q0_ref, q1_ref, q2_ref, kv0_ref, kv1_ref, kv2_ref, u_ref, *, n_seq):
    seq = pl.program_id(1)
    xn = _rms_rows(x_ref[...], g_ref[...]).astype(BF16)
    ones_head = _head_ones()
    cosw, sin1, sin2 = c_ref[...], s1_ref[...], s2_ref[...]
    qn, kn = qn_ref[...], kn_ref[...]
    half = ROT_DIM // 2

    def rope(z):
        return z * cosw + pltpu.roll(z, GROUP_W - half, 1) * sin1 + pltpu.roll(z, half, 1) * sin2

    q_refs = (q0_ref, q1_ref, q2_ref)
    kv_refs = (kv0_ref, kv1_ref, kv2_ref)
    proj = lambda start: _dot(xn, w_ref[:, start:start + GROUP_W])
    for g in range(N_GROUPS):
        zq, zk, zv = (proj(part * ATTN_WIDTH + g * GROUP_W) for part in range(3))
        _store_halves(q_refs[g], 0, rope(zq * _head_inv_rms(zq, ones_head) * qn) * (HEAD_DIM ** -0.5))
        _store_halves(kv_refs[g], 0, rope(zk * _head_inv_rms(zk, ones_head) * kn))
        _store_halves(kv_refs[g], LANE_SPLIT, zv)
    tm = x_ref.shape[0]
    u_rows = pl.ds(seq, tm, stride=n_seq) if n_seq > 1 else slice(None)
    _store_halves(u_ref, 0, proj(3 * ATTN_WIDTH), u_rows)


def _in_proj(x2d, gain, w_bf, qn, kn, cos_t, sin1_t, sin2_t, tm):
    n = x2d.shape[0]
    seq_rows = cos_t.shape[0]
    n_pos, n_seq = seq_rows // tm, n // seq_rows
    row = lambda c, b: (b * n_pos + c, 0)
    row3 = lambda c, b: (0, b * n_pos + c, 0)
    fixed = lambda c, b: (0, 0)
    pos = lambda c, b: (c, 0)
    sds = jax.ShapeDtypeStruct
    return pl.pallas_call(
        functools.partial(_inproj_kernel, n_seq=n_seq),
        grid=(n_pos, n_seq),
        in_specs=[pl.BlockSpec((tm, D_MODEL), row), pl.BlockSpec((1, D_MODEL), fixed),
                  pl.BlockSpec((D_MODEL, IN_WIDTH), fixed),
                  pl.BlockSpec((1, GROUP_W), fixed), pl.BlockSpec((1, GROUP_W), fixed),
                  pl.BlockSpec((tm, GROUP_W), pos), pl.BlockSpec((tm, GROUP_W), pos),
                  pl.BlockSpec((tm, GROUP_W), pos)],
        out_specs=[_halves_spec(LANE_SPLIT, tm, row3)] * 3 + [_halves_spec(2 * LANE_SPLIT, tm, row3)] * 3
                  + [_halves_spec(LANE_SPLIT, tm * n_seq, lambda c, b: (0, c, 0))],
        out_shape=[sds((LANE_SPLIT, n, LANES), F32)] * 3 + [sds((2 * LANE_SPLIT, n, LANES), F32)] * 3
                  + [sds((LANE_SPLIT, n, LANES), F32)],
        compiler_params=_cparams(("arbitrary",) * 2),
        name="in_proj",
    )(x2d, gain, w_bf, qn, kn, cos_t, sin1_t, sin2_t)


def _pattn_kernel(*refs, dil, n_sub, has_prev):
    if has_prev:
        q_ref, kvc_ref, kvp_ref, o_ref, lse_ref = refs
    else:
        q_ref, kvc_ref, o_ref, lse_ref = refs
    n = pl.program_id(1)
    blk = ATTN_BLK
    span = blk * dil
    head = lax.broadcasted_iota(jnp.int32, (1, GROUP_W), 1) // HEAD_DIM
    nk = 2 * blk if has_prev else blk
    qi = lax.broadcasted_iota(jnp.int32, (HEADS_PER_GROUP * blk, nk), 0) % blk
    ki = lax.broadcasted_iota(jnp.int32, (HEADS_PER_GROUP * blk, nk), 1)
    band = ((ki >= qi) & (ki <= qi + blk)) if has_prev else (ki <= qi)
    band_first = (band & ((n > 0) | (ki >= blk))) if has_prev else band

    def sub(j, r):
        rows = pl.ds(j * span + r, blk, stride=dil) if dil > 1 else pl.ds(j * span, blk)
        q = _load_halves(q_ref, 0, rows)
        kc, vc = _load_halves(kvc_ref, 0, rows), _load_halves(kvc_ref, LANE_SPLIT, rows)
        if has_prev:
            if j > 0:
                prow = pl.ds((j - 1) * span + r, blk, stride=dil) if dil > 1 else pl.ds((j - 1) * span, blk)
                kp, vp = _load_halves(kvc_ref, 0, prow), _load_halves(kvc_ref, LANE_SPLIT, prow)
                mask = band
            else:
                prow = pl.ds(r, blk, stride=dil) if dil > 1 else pl.ds(0, blk)
                kp, vp = _load_halves(kvp_ref, 0, prow), _load_halves(kvp_ref, LANE_SPLIT, prow)
                mask = band_first
            k = jnp.concatenate([kp, kc], axis=0).astype(BF16)
            v = jnp.concatenate([vp, vc], axis=0).astype(BF16)
        else:
            k, v, mask = kc.astype(BF16), vc.astype(BF16), band
        qs = jnp.concatenate([jnp.where(head == h, q, 0.0) for h in range(HEADS_PER_GROUP)], axis=0).astype(BF16)
        return rows, qs, k, v, mask

    def pick(x):
        out = x[(HEADS_PER_GROUP - 1) * blk:]
        for h in range(HEADS_PER_GROUP - 2, -1, -1):
            out = jnp.where(head == h, x[h * blk:(h + 1) * blk], out)
        return out

    def run(subs):
        loaded = [sub(j, r) for j, r in subs]
        scores = [jnp.where(mask, _dot_t(qs, k), NEG) for _, qs, k, _, mask in loaded]
        stats = []
        for s in scores:
            m = jnp.max(s, axis=-1, keepdims=True)
            p = jnp.exp(s - m)
            stats.append((m, jnp.sum(p, axis=-1, keepdims=True), p.astype(BF16)))
        outs = [_dot(p, v) for (_, _, p), (_, _, _, v, _) in zip(stats, loaded)]
        for (rows, _, _, _, _), (m, l, _), pv in zip(loaded, stats, outs):
            l_c = pick(l)
            _store_halves(o_ref, 0, pick(pv) / l_c, rows)
            _store_halves(lse_ref, 0, pick(m) + jnp.log(l_c), rows)

    if dil * n_sub <= 4:
        run([(j, r) for j in range(n_sub) for r in range(dil)])
    else:
        per_trip = 4
        for j in range(n_sub):
            def body(it, carry, j=j):
                run([(j, it * per_trip + u) for u in range(per_trip)])
                return carry

            lax.fori_loop(0, dil // per_trip, body, 0)


def _prompt_attn(q, kv, batch, seq, dil):
    span = ATTN_BLK * dil
    has_prev = seq > span
    n_sub = max(1, 512 // span) if has_prev else 1
    rows = n_sub * span
    steps = seq // rows
    cur = lambda b, n: (0, b * steps + n, 0)
    in_specs = [_halves_spec(LANE_SPLIT, rows, cur), _halves_spec(2 * LANE_SPLIT, rows, cur)]
    args = [q, kv]
    if has_prev:
        in_specs.append(_halves_spec(2 * LANE_SPLIT, span,
                                     lambda b, n: (0, b * (seq // span) + jnp.maximum(n * n_sub - 1, 0), 0)))
        args.append(kv)
    return pl.pallas_call(
        functools.partial(_pattn_kernel, dil=dil, n_sub=n_sub, has_prev=has_prev),
        grid=(batch, steps),
        in_specs=in_specs,
        out_specs=[_halves_spec(LANE_SPLIT, rows, cur)] * 2,
        out_shape=[jax.ShapeDtypeStruct((LANE_SPLIT, batch * seq, LANES), F32)] * 2,
        compiler_params=_cparams(("arbitrary",) * 2),
        name=f"prompt_attn_d{dil}",
    )(*args)


def _sattn_kernel(q_ref, kvn_ref, ct_ref, o_ref, lse_ref, qm, osel, lsel, *, dil, bb_n, t_n):
    n_buf = ct_ref.shape[-1]
    hp = HEADS_PER_GROUP
    head = lax.broadcasted_iota(jnp.int32, (1, GROUP_W), 1) // HEAD_DIM
    sel = head == lax.broadcasted_iota(jnp.int32, (hp, 1), 0)
    row = lax.broadcasted_iota(jnp.int32, (t_n * hp, 1), 0)
    trow = row // hp
    sel_rows = head == row % hp
    pos = lax.broadcasted_iota(jnp.int32, (1, n_buf), 1)
    cmask = (pos >= trow) if dil == 1 else ((pos & (dil - 1)) == trow)
    qmbs, scores = [], []
    for bb in range(bb_n):
        for t in range(t_n):
            r = bb * t_n + t
            qm[bb, t * hp:(t + 1) * hp, :] = jnp.where(sel, _load_halves(q_ref, 0, slice(r, r + 1)), 0.0)
        qmbs.append(qm[bb].astype(BF16))
        scores.append(jnp.where(cmask, _dot(qmbs[bb], ct_ref[bb, 0].astype(BF16)), NEG))
    stats = []
    for bb in range(bb_n):
        qmr = qmbs[bb].astype(F32)
        m = jnp.max(scores[bb], axis=-1, keepdims=True)
        s_n = []
        for u in range(t_n):
            r = bb * t_n + u
            kn = _load_halves(kvn_ref, 0, slice(r, r + 1)).astype(BF16).astype(F32)
            valid = (trow >= u) if dil == 1 else (trow == u)
            sn = jnp.where(valid, jnp.sum(qmr * kn, axis=-1, keepdims=True), NEG)
            s_n.append(sn)
            m = jnp.maximum(m, sn)
        p_c = jnp.exp(scores[bb] - m)
        stats.append((m, s_n, jnp.sum(p_c, axis=-1, keepdims=True), p_c.astype(BF16)))
    accs = [_dot_t(stats[bb][3], ct_ref[bb, 1].astype(BF16)) for bb in range(bb_n)]
    for bb in range(bb_n):
        m, s_n, l, _ = stats[bb]
        acc = accs[bb]
        for u in range(t_n):
            p_n = jnp.exp(s_n[u] - m)
            l = l + p_n
            r = bb * t_n + u
            vn = _load_halves(kvn_ref, LANE_SPLIT, slice(r, r + 1)).astype(BF16).astype(F32)
            acc = acc + p_n.astype(BF16).astype(F32) * vn
        osel[bb] = jnp.where(sel_rows, acc / l, 0.0)
        lsel[bb] = jnp.where(sel_rows, m + jnp.log(l), 0.0)
        for t in range(t_n):
            r = slice(bb * t_n + t, bb * t_n + t + 1)
            _store_halves(o_ref, 0, jnp.sum(osel[bb, t * hp:(t + 1) * hp, :], axis=0, keepdims=True), r)
            _store_halves(lse_ref, 0, jnp.sum(lsel[bb, t * hp:(t + 1) * hp, :], axis=0, keepdims=True), r)


def _sample_attn(q, kvn, cache, batch, t_n, dil, bb_n):
    n_buf = cache.shape[1]
    assert n_buf == ATTN_BLK * dil and (dil == 1 or t_n <= dil)
    ct = jnp.transpose(cache, (0, 2, 3, 4, 1)).reshape(batch, 2, GROUP_W, n_buf)
    rows_n = t_n * HEADS_PER_GROUP
    blk = lambda i: (0, i, 0)
    return pl.pallas_call(
        functools.partial(_sattn_kernel, dil=dil, bb_n=bb_n, t_n=t_n),
        grid=(batch // bb_n,),
        in_specs=[_halves_spec(LANE_SPLIT, bb_n * t_n, blk), _halves_spec(2 * LANE_SPLIT, bb_n * t_n, blk),
                  pl.BlockSpec((bb_n, 2, GROUP_W, n_buf), lambda i: (i, 0, 0, 0))],
        out_specs=[_halves_spec(LANE_SPLIT, bb_n * t_n, blk)] * 2,
        out_shape=[jax.ShapeDtypeStruct((LANE_SPLIT, batch * t_n, LANES), F32)] * 2,
        scratch_shapes=[pltpu.VMEM((bb_n, rows_n, GROUP_W), F32)] * 3,
        compiler_params=_cparams(("arbitrary",)),
        name=f"sample_attn_d{dil}",
    )(q, kvn, ct)


def _ssm_kernel(u_ref, s0r_ref, s0i_ref, lre_ref, lim_ref, ldt_ref, bre_ref, bim_ref, cre_ref, cim_ref, d_ref,
                y_ref, fr_ref, fi_ref, hre, him, str_, sti, *, n_seq, steps):
    i = pl.program_id(0)

    @pl.when(i == 0)
    def _():
        str_[...] = s0r_ref[...]
        sti[...] = s0i_ref[...]

    lam_re, lam_im = lre_ref[...], lim_ref[...]
    dt = jnp.exp(ldt_ref[...])
    mag = jnp.exp(lam_re * dt)
    a_re = mag * jnp.cos(lam_im * dt)
    a_im = mag * jnp.sin(lam_im * dt)
    den = lam_re * lam_re + lam_im * lam_im
    nr, ni = a_re - 1.0, a_im
    c_re = (nr * lam_re + ni * lam_im) / den
    c_im = (ni * lam_re - nr * lam_im) / den

    u = _load_halves(u_ref, 0, slice(None))
    ub = u.astype(BF16)
    bu_re = _dot(ub, bre_ref[...])
    bu_im = _dot(ub, bim_ref[...])
    hre[...] = c_re * bu_re - c_im * bu_im
    him[...] = c_re * bu_im + c_im * bu_re

    if n_seq <= 8:
        def step(t, carry):
            sr, si = carry
            rows = pl.ds(pl.multiple_of(t * n_seq, n_seq), n_seq)
            nr_ = a_re * sr - a_im * si + hre[rows, :]
            ni_ = a_re * si + a_im * sr + him[rows, :]
            hre[rows, :] = nr_
            him[rows, :] = ni_
            return nr_, ni_

        sr, si = lax.fori_loop(0, steps, step, (str_[...], sti[...]))
        str_[...] = sr
        sti[...] = si
    else:
        for t in range(steps):
            rows = slice(t * n_seq, (t + 1) * n_seq)
            sr, si = str_[...], sti[...]
            nr_ = a_re * sr - a_im * si + hre[rows, :]
            ni_ = a_re * si + a_im * sr + him[rows, :]
            hre[rows, :] = nr_
            him[rows, :] = ni_
            str_[...] = nr_
            sti[...] = ni_

    y = _dot(hre[...].astype(BF16), cre_ref[...]) - _dot(him[...].astype(BF16), cim_ref[...])
    _store_halves(y_ref, 0, y + d_ref[...] * u)
    fr_ref[...] = str_[...]
    fi_ref[...] = sti[...]


def _ssm(u_tm, s0r, s0i, lam_re, lam_im, log_dt, b_re, b_im, c_re, c_im, dvec, n_seq, n_steps, chunk):
    rows = chunk * n_seq
    fixed = lambda i: (0, 0)
    sds = jax.ShapeDtypeStruct
    return pl.pallas_call(
        functools.partial(_ssm_kernel, n_seq=n_seq, steps=chunk),
        grid=(n_steps // chunk,),
        in_specs=[_halves_spec(LANE_SPLIT, rows, lambda i: (0, i, 0)),
                  pl.BlockSpec((n_seq, SSM_LANES), fixed), pl.BlockSpec((n_seq, SSM_LANES), fixed),
                  pl.BlockSpec((1, SSM_LANES), fixed), pl.BlockSpec((1, SSM_LANES), fixed),
                  pl.BlockSpec((1, SSM_LANES), fixed),
                  pl.BlockSpec((SSM_WIDTH, SSM_LANES), fixed), pl.BlockSpec((SSM_WIDTH, SSM_LANES), fixed),
                  pl.BlockSpec((SSM_LANES, SSM_WIDTH), fixed), pl.BlockSpec((SSM_LANES, SSM_WIDTH), fixed),
                  pl.BlockSpec((1, SSM_WIDTH), fixed)],
        out_specs=[_halves_spec(LANE_SPLIT, rows, lambda i: (0, i, 0)),
                   pl.BlockSpec((n_seq, SSM_LANES), fixed), pl.BlockSpec((n_seq, SSM_LANES), fixed)],
        out_shape=[sds((LANE_SPLIT, n_steps * n_seq, LANES), F32), sds((n_seq, SSM_LANES), F32),
                   sds((n_seq, SSM_LANES), F32)],
        scratch_shapes=[pltpu.VMEM((rows, SSM_LANES), F32), pltpu.VMEM((rows, SSM_LANES), F32),
                        pltpu.VMEM((n_seq, SSM_LANES), F32), pltpu.VMEM((n_seq, SSM_LANES), F32)],
        compiler_params=_cparams(("arbitrary",)),
        name=f"ssm_scan_{n_seq}",
    )(u_tm, s0r, s0i, lam_re, lam_im, log_dt, b_re, b_im, c_re, c_im, dvec)


def _mixout_value(o_refs, l_refs, y, x_ref, ga_ref, gs_ref, wglu_ref, bglu_ref, wout_ref):
    full = slice(None)
    zs =0.5 * y * (1.0 + jnp.tanh(math.sqrt(2.0 / math.pi) * (y + 0.044715 * (y * y * y))))
    glu = _dot(zs.astype(BF16), wglu_ref[...])
    l0, l1, l2 = (_load_halves(r, 0, full) for r in l_refs)
    m = jnp.maximum(jnp.maximum(l0, l1), l2)
    e0, e1, e2 = jnp.exp(l0 - m), jnp.exp(l1 - m), jnp.exp(l2 - m)
    den = e0 + e1 + e2
    a = [_load_halves(r, 0, full) * (e / den) for r, e in zip(o_refs, (e0, e1, e2))]
    ssq = sum(jnp.sum(ai * ai, axis=-1, keepdims=True) for ai in a)
    inv = lax.rsqrt(ssq * (1.0 / ATTN_WIDTH) + EPS)
    acc = x_ref[...]
    for g in range(N_GROUPS):
        an = (a[g] * inv * ga_ref[:, g * GROUP_W:(g + 1) * GROUP_W]).astype(BF16)
        acc = acc + _dot(an, wout_ref[g * GROUP_W:(g + 1) * GROUP_W, :])
    gate = jax.nn.sigmoid(glu + bglu_ref[...])
    so = _rms_rows(zs * gate, gs_ref[...]).astype(BF16)
    return acc + _dot(so, wout_ref[ATTN_WIDTH:, :])


def _mixout_kernel(o0_ref, o1_ref, o2_ref, l0_ref, l1_ref, l2_ref, y_ref, x_ref,
                   ga_ref, gs_ref, wglu_ref, bglu_ref, wout_ref, h_ref):
    h_ref[...] = _mixout_value((o0_ref, o1_ref, o2_ref), (l0_ref, l1_ref, l2_ref),
                               _load_halves(y_ref, 0, slice(None)), x_ref,
                               ga_ref, gs_ref, wglu_ref, bglu_ref, wout_ref)


def _mix_out(os_, ls_, y, x2d, g_attn, g_ssm, wglu_bf, bglu, wout_bf, tm):
    n = x2d.shape[0]
    row = lambda i: (i, 0)
    fixed = lambda i: (0, 0)
    return pl.pallas_call(
        _mixout_kernel,
        grid=(n // tm,),
        in_specs=[_halves_spec(LANE_SPLIT, tm, lambda i: (0, i, 0))] * 7 + [pl.BlockSpec((tm, D_MODEL), row)]
                 + [pl.BlockSpec((1, ATTN_WIDTH), fixed), pl.BlockSpec((1, SSM_WIDTH), fixed),
                    pl.BlockSpec((SSM_WIDTH, SSM_WIDTH), fixed), pl.BlockSpec((1, SSM_WIDTH), fixed),
                    pl.BlockSpec((D_MODEL, D_MODEL), fixed)],
        out_specs=pl.BlockSpec((tm, D_MODEL), row),
        out_shape=jax.ShapeDtypeStruct((n, D_MODEL), F32),
        compiler_params=_cparams(("arbitrary",)),
        name="mix_out",
    )(*os_, *ls_, y, x2d, g_attn, g_ssm, wglu_bf, bglu, wout_bf)


def _memkv_kernel(x_ref, g_ref, w_ref, kn_ref, o_ref, rows_ref):
    xn = _rms_rows(x_ref[...], g_ref[...]).astype(BF16)
    kn = kn_ref[...]
    for kv in range(2):
        for h in range(X_HEADS):
            cols = slice(kv * D_MODEL + h * X_HEAD_DIM, kv * D_MODEL + (h + 1) * X_HEAD_DIM)
            z = _dot(xn, w_ref[:, cols])
            if kv == 0:
                z = _rms_rows(z, kn)
            o_ref[:, cols] = z
            for half in range(MEM_HALVES):
                first = (kv * MEM_HALVES + half) * X_HEADS + h
                rows_ref[pl.ds(first, N_MEM, stride=MEM_ROW_STRIDE), :] = z[:, half * LANES:(half + 1) * LANES]


def _memory_kv(mem2d, gain, w_bf, kn, n_seq):
    fixed = lambda i: (0, 0)
    sds = jax.ShapeDtypeStruct
    return pl.pallas_call(
        _memkv_kernel,
        grid=(n_seq,),
        in_specs=[pl.BlockSpec((N_MEM, D_MODEL), lambda i: (i, 0)), pl.BlockSpec((1, D_MODEL), fixed),
                  pl.BlockSpec((D_MODEL, 2 * D_MODEL), fixed), pl.BlockSpec((1, X_HEAD_DIM), fixed)],
        out_specs=[pl.BlockSpec((N_MEM, 2 * D_MODEL), lambda i: (i, 0)),
                   pl.BlockSpec((None, N_MEM * MEM_ROW_STRIDE, LANES), lambda i: (i, 0, 0))],
        out_shape=[sds((n_seq * N_MEM, 2 * D_MODEL), F32), sds((n_seq, N_MEM * MEM_ROW_STRIDE, LANES), F32)],
        compiler_params=_cparams(("arbitrary",)),
        name="memory_kv",
    )(mem2d, gain, w_bf, kn)


def _mix_cross_kernel(o0_ref, o1_ref, o2_ref, l0_ref, l1_ref, l2_ref, y_ref, x_ref,
                      ga_ref, gs_ref, wglu_ref, bglu_ref, wout_ref,
                      mkv_ref, gx_ref, wq_ref, qn_ref, wo_ref, o_ref, att, *, n_seq, blocks_per_seq):
    tm = x_ref.shape[0]
    seq = pl.program_id(0) // blocks_per_seq
    h = _mixout_value((o0_ref, o1_ref, o2_ref), (l0_ref, l1_ref, l2_ref),
                      _load_halves(y_ref, 0, pl.ds(seq, tm, stride=n_seq)), x_ref,
                      ga_ref, gs_ref, wglu_ref, bglu_ref, wout_ref)
    hn = _rms_rows(h, gx_ref[...]).astype(BF16)
    qn = qn_ref[...]
    heads = [slice(hd * X_HEAD_DIM, (hd + 1) * X_HEAD_DIM) for hd in range(X_HEADS)]
    qz = [_dot(hn, wq_ref[:, cols]) for cols in heads]
    qs = [(_rms_rows(z, qn) * (X_HEAD_DIM ** -0.5)).astype(BF16) for z in qz]
    scores = [_dot_t(q, mkv_ref[:, cols].astype(BF16)) for q, cols in zip(qs, heads)]
    probs = []
    for s in scores:
        p = jnp.exp(s - jnp.max(s, axis=-1, keepdims=True))
        probs.append((p.astype(BF16), jnp.sum(p, axis=-1, keepdims=True)))
    outs = [_dot(p, mkv_ref[:, D_MODEL + cols.start:D_MODEL + cols.stop].astype(BF16)) for (p, _), cols in zip(probs, heads)]
    for pv, (_, l), cols in zip(outs, probs, heads):
        att[:, cols] = (pv / l).astype(BF16)
    o_ref[...] = h + _dot(att[...], wo_ref[...])


def _mix_cross(os_, ls_, y, x2d, mkv, p, rows_per_seq, tm):
    n = x2d.shape[0]
    blocks_per_seq = rows_per_seq // tm
    n_seq = n // rows_per_seq
    row = lambda i: (i, 0)
    fixed = lambda i: (0, 0)
    return pl.pallas_call(
        functools.partial(_mix_cross_kernel, n_seq=n_seq, blocks_per_seq=blocks_per_seq),
        grid=(n // tm,),
        in_specs=[_halves_spec(LANE_SPLIT, tm, lambda i: (0, i, 0))] * 6
                 + [_halves_spec(LANE_SPLIT, tm * n_seq, lambda i: (0, i % blocks_per_seq, 0)),
                    pl.BlockSpec((tm, D_MODEL), row)]
                 + [pl.BlockSpec((1, ATTN_WIDTH), fixed), pl.BlockSpec((1, SSM_WIDTH), fixed),
                    pl.BlockSpec((SSM_WIDTH, SSM_WIDTH), fixed), pl.BlockSpec((1, SSM_WIDTH), fixed),
                    pl.BlockSpec((D_MODEL, D_MODEL), fixed)]
                 + [pl.BlockSpec((None, N_MEM, 2 * D_MODEL), lambda i: (i // blocks_per_seq, 0, 0)),
                    pl.BlockSpec((1, D_MODEL), fixed), pl.BlockSpec((D_MODEL, D_MODEL), fixed),
                    pl.BlockSpec((1, X_HEAD_DIM), fixed), pl.BlockSpec((D_MODEL, D_MODEL), fixed)],
        out_specs=pl.BlockSpec((tm, D_MODEL), row),
        out_shape=jax.ShapeDtypeStruct((n, D_MODEL), F32),
        scratch_shapes=[pltpu.VMEM((tm, D_MODEL), BF16)],
        compiler_params=_cparams(("arbitrary",)),
        name="mix_cross",
    )(*os_, *ls_, y, x2d, p['attn_out_norm'], p['ssm_out_norm'], p['w_glu'], p['b_glu'], p['w_out'],
      mkv, p['norm_x'], p['w_xq'], p['xq_norm'], p['w_xo'])


MEM_HALVES = X_HEAD_DIM // 128
MEM_ROW_STRIDE = 2 * MEM_HALVES * X_HEADS


def _cross_sample_kernel(h_ref, mem_ref, g_ref, wq_ref, qn_ref, wo_ref, o_ref, qsc, qm, osel, att, *, bb_n, t_n):
    h = h_ref[...]
    hn = _rms_rows(h, g_ref[...]).astype(BF16)
    qn = qn_ref[...]
    for hd in range(X_HEADS):
        cols = slice(hd * X_HEAD_DIM, (hd + 1) * X_HEAD_DIM)
        qsc[:, cols] = _rms_rows(_dot(hn, wq_ref[:, cols]), qn) * (X_HEAD_DIM ** -0.5)
    xh = X_HEADS
    head = lax.broadcasted_iota(jnp.int32, (1, D_MODEL), 1) // X_HEAD_DIM
    sel = head == lax.broadcasted_iota(jnp.int32, (xh, 1), 0)
    sel_rows = head == lax.broadcasted_iota(jnp.int32, (t_n * xh, 1), 0) % xh

    def gather(bb, kv):
        parts = [mem_ref[bb, pl.ds((kv * MEM_HALVES + half) * xh + hd, N_MEM, stride=MEM_ROW_STRIDE), :]
                 for hd in range(xh) for half in range(MEM_HALVES)]
        return jnp.concatenate(parts, axis=1).astype(BF16)

    scores = []
    for bb in range(bb_n):
        for t in range(t_n):
            r = bb * t_n + t
            qm[bb, t * xh:(t + 1) * xh, :] = jnp.where(sel, qsc[r:r + 1, :], 0.0)
        scores.append(_dot_t(qm[bb].astype(BF16), gather(bb, 0)))
    probs = []
    for s in scores:
        p = jnp.exp(s - jnp.max(s, axis=-1, keepdims=True))
        probs.append((p.astype(BF16), jnp.sum(p, axis=-1, keepdims=True)))
    outs = [_dot(probs[bb][0], gather(bb, 1)) for bb in range(bb_n)]
    for bb in range(bb_n):
        osel[bb] = jnp.where(sel_rows, outs[bb] / probs[bb][1], 0.0)
        for t in range(t_n):
            r = bb * t_n + t
            att[r:r + 1, :] = jnp.sum(osel[bb, t * xh:(t + 1) * xh, :], axis=0, keepdims=True)
    o_ref[...] = h + _dot(att[...].astype(BF16), wo_ref[...])


def _cross_sample(h2d, mem_kv, gain, wq_bf, qn, wo_bf, batch, t_n, bb_n):
    mem = mem_kv.reshape(batch, N_MEM, 2, X_HEADS, MEM_HALVES, 128)
    mem = jnp.transpose(mem, (0, 1, 2, 4, 3, 5)).reshape(batch, N_MEM * MEM_ROW_STRIDE, 128)
    rows = bb_n * t_n
    fixed = lambda i: (0, 0)
    return pl.pallas_call(
        functools.partial(_cross_sample_kernel, bb_n=bb_n, t_n=t_n),
        grid=(batch // bb_n,),
        in_specs=[pl.BlockSpec((rows, D_MODEL), lambda i: (i, 0)),
                  pl.BlockSpec((bb_n, N_MEM * MEM_ROW_STRIDE, 128), lambda i: (i, 0, 0)),
                  pl.BlockSpec((1, D_MODEL), fixed), pl.BlockSpec((D_MODEL, D_MODEL), fixed),
                  pl.BlockSpec((1, X_HEAD_DIM), fixed), pl.BlockSpec((D_MODEL, D_MODEL), fixed)],
        out_specs=pl.BlockSpec((rows, D_MODEL), lambda i: (i, 0)),
        out_shape=jax.ShapeDtypeStruct((batch * t_n, D_MODEL), F32),
        scratch_shapes=[pltpu.VMEM((rows, D_MODEL), F32), pltpu.VMEM((bb_n, t_n * X_HEADS, D_MODEL), F32),
                        pltpu.VMEM((bb_n, t_n * X_HEADS, D_MODEL), F32), pltpu.VMEM((rows, D_MODEL), F32)],
        compiler_params=_cparams(("arbitrary",)),
        name="cross_attn_sample",
    )(h2d, mem, gain, wq_bf, qn, wo_bf)


FFN_CHUNKS = 2
FFN_TF = D_FF // FFN_CHUNKS
CARRY_ROWS = 8


def _conv_gate(up_g, up_v, prev_g, prev_v, bufg, bufv, cwg_ref, cwv_ref, cbg_ref, cbv_ref, tm, lag):
    hist = prev_g.shape[0]
    bufg[0:hist] = prev_g
    bufg[hist:hist + tm] = up_g
    bufv[0:hist] = prev_v
    bufv[hist:hist + tm] = up_v
    cg = cbg_ref[...] + (cwg_ref[0:1, :] * bufg[hist - 2 * lag:hist - 2 * lag + tm]
                         + cwg_ref[1:2, :] * bufg[hist - lag:hist - lag + tm] + cwg_ref[2:3, :] * up_g)
    cv = cbv_ref[...] + (cwv_ref[0:1, :] * bufv[hist - 2 * lag:hist - 2 * lag + tm]
                         + cwv_ref[1:2, :] * bufv[hist - lag:hist - lag + tm] + cwv_ref[2:3, :] * up_v)
    return (cg * jax.nn.sigmoid(cg) * cv).astype(BF16)


FFN_COLS = 256
FFN_NCHUNK = D_FF // FFN_COLS


def _ffn_prompt_kernel(h_ref, g_ref, wup_ref, cw_ref, cb_ref, wd_ref, o_ref, tail_ref, act, carry, *, tm):
    i = pl.program_id(1)
    h = h_ref[...]
    xn = _rms_rows(h, g_ref[...]).astype(BF16)

    @pl.when(i == 0)
    def _():
        carry[...] = jnp.zeros(carry.shape, F32)

    for n in range(FFN_NCHUNK):
        conv = []
        for part in range(2):
            cols = slice(part * D_FF + n * FFN_COLS, part * D_FF + (n + 1) * FFN_COLS)
            up = _dot(xn, wup_ref[:, cols])
            ext = jnp.concatenate([carry[part, n], up], axis=0)
            conv.append(cb_ref[:, cols] + (cw_ref[0:1, cols] * ext[CARRY_ROWS - 2:CARRY_ROWS - 2 + tm]
                                           + cw_ref[1:2, cols] * ext[CARRY_ROWS - 1:CARRY_ROWS - 1 + tm]
                                           + cw_ref[2:3, cols] * up))
            carry[part, n] = up[tm - CARRY_ROWS:]
            tail_ref[:, cols] = up[tm - CARRY_ROWS:]
        cg, cv = conv
        act[:, n * FFN_COLS:(n + 1) * FFN_COLS] = (cg * jax.nn.sigmoid(cg) * cv).astype(BF16)
    o_ref[...] = h + _dot(act[...], wd_ref[...])


def _ffn_prompt(h2d, gain, wup_bf, conv_w, conv_b, wdown_bf, batch, seq, tm):
    n_i = seq // tm
    row = lambda b, i: (b * n_i + i, 0)
    fixed = lambda b, i: (0, 0)
    once = pl.Buffered(1)
    sds = jax.ShapeDtypeStruct
    return pl.pallas_call(
        functools.partial(_ffn_prompt_kernel, tm=tm),
        grid=(batch, n_i),
        in_specs=[pl.BlockSpec((tm, D_MODEL), row), pl.BlockSpec((1, D_MODEL), fixed),
                  pl.BlockSpec((D_MODEL, 2 * D_FF), fixed, pipeline_mode=once),
                  pl.BlockSpec((CONV_W, 2 * D_FF), fixed), pl.BlockSpec((1, 2 * D_FF), fixed),
                  pl.BlockSpec((D_FF, D_MODEL), fixed, pipeline_mode=once)],
        out_specs=[pl.BlockSpec((tm, D_MODEL), row),
                   pl.BlockSpec((None, None, CARRY_ROWS, 2 * D_FF), lambda b, i: (b, i, 0, 0))],
        out_shape=[sds((batch * seq, D_MODEL), F32), sds((batch, n_i, CARRY_ROWS, 2 * D_FF), F32)],
        scratch_shapes=[pltpu.VMEM((tm, D_FF), BF16), pltpu.VMEM((2, FFN_NCHUNK, CARRY_ROWS, FFN_COLS), F32)],
        compiler_params=_cparams(("arbitrary",) * 2),
        name="ffn_prompt",
    )(h2d, gain, wup_bf, conv_w, conv_b, wdown_bf)


def _ffn_sample_kernel(h_ref, g_ref, sg_ref, sv_ref, wg_ref, wv_ref, cwg_ref, cwv_ref, cbg_ref, cbv_ref, wd_ref,
                       o_ref, ng_ref, nv_ref, xn, bufg, bufv, *, n_seq, rows):
    c = pl.program_id(0)

    @pl.when(c == 0)
    def _():
        xn[...] = _rms_rows(h_ref[...], g_ref[...]).astype(BF16)

    up_g = _dot(xn[...], wg_ref[...])
    up_v = _dot(xn[...], wv_ref[...])
    act = _conv_gate(up_g, up_v, sg_ref[...], sv_ref[...], bufg, bufv, cwg_ref, cwv_ref, cbg_ref, cbv_ref,
                     rows, n_seq)
    hist = (CONV_W - 1) * n_seq
    ng_ref[...] = bufg[rows:rows + hist]
    nv_ref[...] = bufv[rows:rows + hist]
    contrib = _dot(act, wd_ref[...])

    @pl.when(c == 0)
    def _():
        o_ref[...] = h_ref[...] + contrib

    @pl.when(c > 0)
    def _():
        o_ref[...] += contrib


def _ffn_sample(h_tm, gain, state_tm, wup_bf, conv_w, conv_b, wdown_bf, n_seq, n_steps):
    rows = n_seq * n_steps
    hist = (CONV_W - 1) * n_seq
    nc, tf = FFN_CHUNKS, FFN_TF
    gcol = lambda c: (0, c)
    vcol = lambda c: (0, nc + c)
    fixed = lambda c: (0, 0)
    sds = jax.ShapeDtypeStruct
    return pl.pallas_call(
        functools.partial(_ffn_sample_kernel, n_seq=n_seq, rows=rows),
        grid=(nc,),
        in_specs=[pl.BlockSpec((rows, D_MODEL), fixed), pl.BlockSpec((1, D_MODEL), fixed),
                  pl.BlockSpec((hist, tf), gcol), pl.BlockSpec((hist, tf), vcol),
                  pl.BlockSpec((D_MODEL, tf), gcol), pl.BlockSpec((D_MODEL, tf), vcol),
                  pl.BlockSpec((CONV_W, tf), gcol), pl.BlockSpec((CONV_W, tf), vcol),
                  pl.BlockSpec((1, tf), gcol), pl.BlockSpec((1, tf), vcol),
                  pl.BlockSpec((tf, D_MODEL), lambda c: (c, 0))],
        out_specs=[pl.BlockSpec((rows, D_MODEL), fixed), pl.BlockSpec((hist, tf), gcol),
                   pl.BlockSpec((hist, tf), gcol)],
        out_shape=[sds((rows, D_MODEL), F32), sds((hist, D_FF), F32), sds((hist, D_FF), F32)],
        scratch_shapes=[pltpu.VMEM((rows, D_MODEL), BF16), pltpu.VMEM((rows + hist, tf), F32),
                        pltpu.VMEM((rows + hist, tf), F32)],
        compiler_params=_cparams(("arbitrary",)),
        name="ffn_sample",
    )(h_tm, gain, state_tm, state_tm, wup_bf, wup_bf, conv_w, conv_w, conv_b, conv_b, wdown_bf)


def _rope_tables(pos):
    half = ROT_DIM // 2
    inv = jnp.power(jnp.float32(ROPE_THETA), -jnp.arange(half, dtype=F32) * 2.0 / ROT_DIM)
    ang = pos.astype(F32)[:, None] * inv[None, :]
    cos, sin = jnp.cos(ang), jnp.sin(ang)
    ones = jnp.ones((pos.shape[0], HEAD_DIM - ROT_DIM), F32)
    zeros = jnp.zeros((pos.shape[0], HEAD_DIM - ROT_DIM), F32)
    zh = jnp.zeros_like(sin)
    c_head = jnp.concatenate([cos, cos, ones], axis=1)
    s1_head = jnp.concatenate([-sin, zh, zeros], axis=1)
    s2_head = jnp.concatenate([zh, sin, zeros], axis=1)
    tile = lambda a: jnp.tile(a, (1, HEADS_PER_GROUP))
    return tile(c_head), tile(s1_head), tile(s2_head)


def _block_diag(w):
    g, a, b = w.shape
    eye = jnp.eye(g, dtype=w.dtype)
    return (eye[:, None, :, None] * w[:, :, None, :]).reshape(g * a, g * b)


def _layer_params(l, norm_mix, w_in, q_norm, k_norm, ssm_lam_re, ssm_lam_im, ssm_log_dt, ssm_b_re, ssm_b_im,
                  ssm_c_re, ssm_c_im, ssm_d, w_glu, b_glu, attn_out_norm, ssm_out_norm, w_out, norm_x, norm_mem,
                  w_xq, w_xkv, xq_norm, xk_norm, w_xo, norm_ffn, w_up, conv_w, conv_b, w_down):
    row = lambda a: a.reshape(1, -1).astype(F32)
    return dict(
        norm_mix=row(norm_mix[l]), w_in=w_in[l].astype(BF16),
        q_norm=row(jnp.tile(q_norm[l], HEADS_PER_GROUP)), k_norm=row(jnp.tile(k_norm[l], HEADS_PER_GROUP)),
        lam_re=row(ssm_lam_re[l]), lam_im=row(ssm_lam_im[l]),
        log_dt=row(jnp.repeat(ssm_log_dt[l], SSM_STATE)),
        b_re=_block_diag(jnp.swapaxes(ssm_b_re[l], 1, 2)).astype(BF16),
        b_im=_block_diag(jnp.swapaxes(ssm_b_im[l], 1, 2)).astype(BF16),
        c_re=_block_diag(jnp.swapaxes(ssm_c_re[l], 1, 2)).astype(BF16),
        c_im=_block_diag(jnp.swapaxes(ssm_c_im[l], 1, 2)).astype(BF16),
        ssm_d=row(ssm_d[l]), w_glu=w_glu[l].astype(BF16), b_glu=row(b_glu[l]),
        attn_out_norm=row(attn_out_norm[l]), ssm_out_norm=row(ssm_out_norm[l]), w_out=w_out[l].astype(BF16),
        norm_x=row(norm_x[l]), norm_mem=row(norm_mem[l]), w_xq=w_xq[l].astype(BF16),
        w_xkv=w_xkv[l].astype(BF16), xq_norm=row(xq_norm[l]), xk_norm=row(xk_norm[l]),
        w_xo=w_xo[l].astype(BF16), norm_ffn=row(norm_ffn[l]), w_up=w_up[l].astype(BF16),
        conv_w=conv_w[l].astype(F32), conv_b=row(conv_b[l]), w_down=w_down[l].astype(BF16))


def _kv_rows(kv_halves, n_seq, n_steps):
    heads_per_half = LANES // HEAD_DIM
    a = kv_halves.reshape(2, LANE_SPLIT, n_seq, n_steps, heads_per_half, HEAD_DIM)
    return jnp.transpose(a, (2, 3, 0, 1, 4, 5)).reshape(n_seq, n_steps, 2, HEADS_PER_GROUP, HEAD_DIM)


def _to_time_major(a, n_seq, n_steps):
    return a.reshape(n_seq, n_steps, -1).swapaxes(0, 1).reshape(n_steps * n_seq, -1)


def _to_seq_major(a, n_seq, n_steps):
    return a.reshape(n_steps, n_seq, -1).swapaxes(0, 1).reshape(n_seq * n_steps, -1)


def _mixer_front(x2d, p, tables, tm):
    return _in_proj(x2d, p['norm_mix'], p['w_in'], p['q_norm'], p['k_norm'], *tables, tm)


def _run_ssm(u_tm, s0r, s0i, p, n_seq, n_steps, chunk):
    return _ssm(u_tm, s0r, s0i, p['lam_re'], p['lam_im'], p['log_dt'], p['b_re'], p['b_im'],
                p['c_re'], p['c_im'], p['ssm_d'], n_seq, n_steps, chunk)


def _halves_to_time_major(a, n_seq, n_steps):
    return a.reshape(LANE_SPLIT, n_seq, n_steps, LANES).swapaxes(1, 2).reshape(LANE_SPLIT, n_steps * n_seq, LANES)


def _halves_to_seq_major(a, n_seq, n_steps):
    return a.reshape(LANE_SPLIT, n_steps, n_seq, LANES).swapaxes(1, 2).reshape(LANE_SPLIT, n_seq * n_steps, LANES)


def kernel(x_prompt, x_sample, mem_prompt, cache_kv_dil1, cache_kv_dil4, cache_kv_dil16, state_ssm_re, state_ssm_im, state_ffn_conv, cache_mem_kv, norm_mix, w_in, q_norm, k_norm, ssm_lam_re, ssm_lam_im, ssm_log_dt, ssm_b_re, ssm_b_im, ssm_c_re, ssm_c_im, ssm_d, w_glu, b_glu, attn_out_norm, ssm_out_norm, w_out, norm_x, norm_mem, w_xq, w_xkv, xq_norm, xk_norm, w_xo, norm_ffn, w_up, conv_w, conv_b, w_down):
    bp, lp, _ = x_prompt.shape
    bs, ls, _ = x_sample.shape
    depth = w_in.shape[0]
    past_len = cache_kv_dil16.shape[2]
    caches = (cache_kv_dil1, cache_kv_dil4, cache_kv_dil16)
    tables_p = _rope_tables(jnp.arange(lp))
    tables_s = tuple(jnp.tile(t, (bs, 1)) for t in _rope_tables(past_len + jnp.arange(ls)))

    yp = x_prompt.reshape(bp * lp, D_MODEL)
    ys = x_sample.reshape(bs * ls, D_MODEL)
    kv_p = [[] for _ in range(N_GROUPS)]
    kv_s = [[] for _ in range(N_GROUPS)]
    re_p, im_p, conv_p, mem_p, re_s, im_s, conv_s = [], [], [], [], [], [], []
    for l in range(depth):
        p = _layer_params(l, norm_mix, w_in, q_norm, k_norm, ssm_lam_re, ssm_lam_im, ssm_log_dt, ssm_b_re,
                          ssm_b_im, ssm_c_re, ssm_c_im, ssm_d, w_glu, b_glu, attn_out_norm, ssm_out_norm, w_out,
                          norm_x, norm_mem, w_xq, w_xkv, xq_norm, xk_norm, w_xo, norm_ffn, w_up, conv_w, conv_b,
                          w_down)
        mkv, mkv_rows = _memory_kv(mem_prompt.reshape(bp * N_MEM, D_MODEL), p['norm_mem'], p['w_xkv'],
                                   p['xk_norm'], bp)
        mkv = mkv.reshape(bp, N_MEM, 2 * D_MODEL)
        mkv_rows = mkv_rows.reshape(bp, N_MEM, 2, MEM_HALVES, X_HEADS, LANES)
        mem_p.append(jnp.transpose(mkv_rows, (0, 1, 2, 4, 3, 5)).reshape(bp, N_MEM, 2, X_HEADS, X_HEAD_DIM))
        q0, q1, q2, kv0, kv1, kv2, u = _mixer_front(yp, p, tables_p, 512)
        os_, ls_ = [], []
        for g, (qg, kvg) in enumerate(((q0, kv0), (q1, kv1), (q2, kv2))):
            win, dil = ATTN_GROUPS[g]
            o, lse = _prompt_attn(qg, kvg, bp, lp, dil)
            os_.append(o)
            ls_.append(lse)
            keep = min(win, lp)
            kv_p[g].append(_kv_rows(kvg, bp, lp)[:, lp - keep:])
        zero_state = jnp.zeros((bp, SSM_LANES), F32)
        y, fr, fi = _run_ssm(u, zero_state, zero_state, p, bp, lp, 128)
        re_p.append(fr.reshape(bp, SSM_GROUPS, SSM_STATE))
        im_p.append(fi.reshape(bp, SSM_GROUPS, SSM_STATE))
        h = _mix_cross(os_, ls_, y, yp, mkv, p, lp, 512)
        yp, tail = _ffn_prompt(h, p['norm_ffn'], p['w_up'], p['conv_w'], p['conv_b'], p['w_down'], bp, lp, 512)
        conv_p.append(tail[:, -1, CARRY_ROWS - (CONV_W - 1):])
        q0, q1, q2, kv0, kv1, kv2, u = _mixer_front(ys, p, tables_s, bs * ls)
        os_, ls_ = [], []
        for g, (qg, kvg) in enumerate(((q0, kv0), (q1, kv1), (q2, kv2))):
            win, dil = ATTN_GROUPS[g]
            o, lse = _sample_attn(qg, kvg, caches[g][l], bs, ls, dil, max(1, 32 // dil))
            os_.append(o)
            ls_.append(lse)
            kv_s[g].append(_kv_rows(kvg, bs, ls))
        y, fr, fi = _run_ssm(_halves_to_time_major(u, bs, ls), state_ssm_re[l].reshape(bs, SSM_LANES),
                             state_ssm_im[l].reshape(bs, SSM_LANES), p, bs, ls, ls)
        re_s.append(fr.reshape(bs, SSM_GROUPS, SSM_STATE))
        im_s.append(fi.reshape(bs, SSM_GROUPS, SSM_STATE))
        h = _mix_out(os_, ls_, _halves_to_seq_major(y, bs, ls), ys, p['attn_out_norm'], p['ssm_out_norm'], p['w_glu'], p['b_glu'], p['w_out'],
                     bs * ls)
        h = _cross_sample(h, cache_mem_kv[l], p['norm_x'], p['w_xq'], p['xq_norm'], p['w_xo'], bs, ls, 4)
        state_tm = state_ffn_conv[l].swapaxes(0, 1).reshape((CONV_W - 1) * bs, 2 * D_FF)
        ys_tm, ng, nv = _ffn_sample(_to_time_major(h, bs, ls), p['norm_ffn'], state_tm, p['w_up'], p['conv_w'],
                                    p['conv_b'], p['w_down'], bs, ls)
        ys = _to_seq_major(ys_tm, bs, ls)
        conv_s.append(jnp.concatenate([ng, nv], axis=-1).reshape(CONV_W - 1, bs, 2 * D_FF).swapaxes(0, 1))

    st = jnp.stack
    return (yp.reshape(bp, lp, D_MODEL), ys.reshape(bs, ls, D_MODEL),
            st(kv_p[0]), st(kv_p[1]), st(kv_p[2]), st(re_p), st(im_p), st(conv_p), st(mem_p),
            st(kv_s[0]), st(kv_s[1]), st(kv_s[2]), st(re_s), st(im_s), st(conv_s))
```

```python
import functools
import math

import jax
import jax.numpy as jnp
from jax import lax
from jax.experimental import pallas as pl
from jax.experimental.pallas import tpu as pltpu

F32 = jnp.float32
BF16 = jnp.bfloat16

D_MODEL = 1024
HEAD_DIM = 64
ATTN_GROUPS = ((128, 1), (512, 4), (2048, 16))
HEADS_PER_GROUP = 4
GROUP_W = HEADS_PER_GROUP * HEAD_DIM
N_GROUPS = len(ATTN_GROUPS)
ATTN_WIDTH = N_GROUPS * GROUP_W
SSM_WIDTH = D_MODEL - ATTN_WIDTH
SSM_GROUP = 16
SSM_GROUPS = SSM_WIDTH // SSM_GROUP
SSM_STATE = 64
SSM_LANES = SSM_GROUPS * SSM_STATE
IN_WIDTH = 3 * ATTN_WIDTH + SSM_WIDTH
ROT_DIM = HEAD_DIM // 4
ROPE_THETA = 500000.0
N_MEM = 256
X_HEADS = 4
X_HEAD_DIM = D_MODEL // X_HEADS
D_FF = 2816
CONV_W = 3
EPS = 1e-6
ATTN_BLK = 128
NEG = -1e30

VMEM_LIMIT = 56 * 1024 * 1024


def _cparams(sem):
    return pltpu.CompilerParams(dimension_semantics=sem, vmem_limit_bytes=VMEM_LIMIT)


def _dot(a, b):
    return jnp.dot(a, b, preferred_element_type=F32)


def _dot_t(a, b):
    return lax.dot_general(a, b, (((1,), (1,)), ((), ())), preferred_element_type=F32)


def _rms_rows(x, gain):
    y = x * lax.rsqrt(jnp.mean(x * x, axis=-1, keepdims=True) + EPS)
    return y * gain


LANES = 128
LANE_SPLIT = GROUP_W // LANES


def _halves_spec(n_halves, rows, index_map):
    return pl.BlockSpec((n_halves, rows, LANES), index_map)


def _load_halves(ref, first, rows):
    return jnp.concatenate([ref[first + c, rows, :] for c in range(LANE_SPLIT)], axis=1)


def _store_halves(ref, first, value, rows=slice(None)):
    for c in range(LANE_SPLIT):
        ref[first + c, rows, :] = value[:, c * LANES:(c + 1) * LANES]


def _head_ones():
    r = lax.broadcasted_iota(jnp.int32, (GROUP_W, GROUP_W), 0) // HEAD_DIM
    c = lax.broadcasted_iota(jnp.int32, (GROUP_W, GROUP_W), 1) // HEAD_DIM
    return (r == c).astype(BF16)


def _head_inv_rms(z, ones_head):
    sq = z * z
    hi = sq.astype(BF16)
    lo = (sq - hi.astype(F32)).astype(BF16)
    ssum = _dot(hi, ones_head) + _dot(lo, ones_head)
    return lax.rsqrt(ssum * (1.0 / HEAD_DIM) + EPS)


def _inproj_kernel(x_ref, g_ref, w_ref, qn_ref, kn_ref, c_ref, s1_ref, s2_ref,
                   q0_ref, q1_ref, q2_ref, kv0_ref, kv1_ref, kv2_ref, u_ref, *kvt_refs, n_seq):
    seq = pl.program_id(1)
    xn = _rms_rows(x_ref[...], g_ref[...]).astype(BF16)
    ones_head = _head_ones()
    cosw, sin1, sin2 = c_ref[...], s1_ref[...], s2_ref[...]
    qn, kn = qn_ref[...], kn_ref[...]
    half = ROT_DIM // 2

    def rope(z):
        return z * cosw + pltpu.roll(z, GROUP_W - half, 1) * sin1 + pltpu.roll(z, half, 1) * sin2

    q_refs = (q0_ref, q1_ref, q2_ref)
    kv_refs = (kv0_ref, kv1_ref, kv2_ref)
    proj = lambda start: _dot(xn, w_ref[:, start:start + GROUP_W])
    for g in range(N_GROUPS):
        zq, zk, zv = (proj(part * ATTN_WIDTH + g * GROUP_W) for part in range(3))
        _store_halves(q_refs[g], 0, rope(zq * _head_inv_rms(zq, ones_head) * qn) * (HEAD_DIM ** -0.5))
        k = rope(zk * _head_inv_rms(zk, ones_head) * kn)
        _store_halves(kv_refs[g], 0, k)
        _store_halves(kv_refs[g], LANE_SPLIT, zv)
        if kvt_refs and g == N_GROUPS - 1:
            kvt_refs[0][0] = k.T
            kvt_refs[0][1] = zv.T
    tm = x_ref.shape[0]
    u_rows = pl.ds(seq, tm, stride=n_seq) if n_seq > 1 else slice(None)
    _store_halves(u_ref, 0, proj(3 * ATTN_WIDTH), u_rows)


def _in_proj(x2d, gain, w_bf, qn, kn, cos_t, sin1_t, sin2_t, tm, feature_major_last):
    n = x2d.shape[0]
    seq_rows = cos_t.shape[0]
    n_pos, n_seq = seq_rows // tm, n // seq_rows
    row = lambda c, b: (b * n_pos + c, 0)
    row3 = lambda c, b: (0, b * n_pos + c, 0)
    fixed = lambda c, b: (0, 0)
    pos = lambda c, b: (c, 0)
    sds = jax.ShapeDtypeStruct
    extra_specs = [pl.BlockSpec((None, 2, GROUP_W, tm), lambda c, b: (b, 0, 0, c))] if feature_major_last else []
    extra_shapes = [sds((n_seq, 2, GROUP_W, seq_rows), F32)] if feature_major_last else []
    return pl.pallas_call(
        functools.partial(_inproj_kernel, n_seq=n_seq),
        grid=(n_pos, n_seq),
        in_specs=[pl.BlockSpec((tm, D_MODEL), row), pl.BlockSpec((1, D_MODEL), fixed),
                  pl.BlockSpec((D_MODEL, IN_WIDTH), fixed),
                  pl.BlockSpec((1, GROUP_W), fixed), pl.BlockSpec((1, GROUP_W), fixed),
                  pl.BlockSpec((tm, GROUP_W), pos), pl.BlockSpec((tm, GROUP_W), pos),
                  pl.BlockSpec((tm, GROUP_W), pos)],
        out_specs=[_halves_spec(LANE_SPLIT, tm, row3)] * 3 + [_halves_spec(2 * LANE_SPLIT, tm, row3)] * 3
                  + [_halves_spec(LANE_SPLIT, tm * n_seq, lambda c, b: (0, c, 0))] + extra_specs,
        out_shape=[sds((LANE_SPLIT, n, LANES), F32)] * 3 + [sds((2 * LANE_SPLIT, n, LANES), F32)] * 3
                  + [sds((LANE_SPLIT, n, LANES), F32)] + extra_shapes,
        compiler_params=_cparams(("arbitrary",) * 2),
        name="in_proj",
    )(x2d, gain, w_bf, qn, kn, cos_t, sin1_t, sin2_t)


def _pattn_kernel(*refs, dil, n_sub, has_prev):
    if has_prev:
        q_ref, kvc_ref, kvp_ref, o_ref, lse_ref = refs
    else:
        q_ref, kvc_ref, o_ref, lse_ref = refs
    n = pl.program_id(1)
    blk = ATTN_BLK
    span = blk * dil
    head = lax.broadcasted_iota(jnp.int32, (1, GROUP_W), 1) // HEAD_DIM
    nk = 2 * blk if has_prev else blk
    qi = lax.broadcasted_iota(jnp.int32, (HEADS_PER_GROUP * blk, nk), 0) % blk
    ki = lax.broadcasted_iota(jnp.int32, (HEADS_PER_GROUP * blk, nk), 1)
    band = ((ki >= qi) & (ki <= qi + blk)) if has_prev else (ki <= qi)
    band_first = (band & ((n > 0) | (ki >= blk))) if has_prev else band

    def sub(j, r):
        rows = pl.ds(j * span + r, blk, stride=dil) if dil > 1 else pl.ds(j * span, blk)
        q = _load_halves(q_ref, 0, rows)
        kc, vc = _load_halves(kvc_ref, 0, rows), _load_halves(kvc_ref, LANE_SPLIT, rows)
        if has_prev:
            if j > 0:
                prow = pl.ds((j - 1) * span + r, blk, stride=dil) if dil > 1 else pl.ds((j - 1) * span, blk)
                kp, vp = _load_halves(kvc_ref, 0, prow), _load_halves(kvc_ref, LANE_SPLIT, prow)
                mask = band
            else:
                prow = pl.ds(r, blk, stride=dil) if dil > 1 else pl.ds(0, blk)
                kp, vp = _load_halves(kvp_ref, 0, prow), _load_halves(kvp_ref, LANE_SPLIT, prow)
                mask = band_first
            k = jnp.concatenate([kp, kc], axis=0).astype(BF16)
            v = jnp.concatenate([vp, vc], axis=0).astype(BF16)
        else:
            k, v, mask = kc.astype(BF16), vc.astype(BF16), band
        qs = jnp.concatenate([jnp.where(head == h, q, 0.0) for h in range(HEADS_PER_GROUP)], axis=0).astype(BF16)
        return rows, qs, k, v, mask

    def pick(x):
        out = x[(HEADS_PER_GROUP - 1) * blk:]
        for h in range(HEADS_PER_GROUP - 2, -1, -1):
            out = jnp.where(head == h, x[h * blk:(h + 1) * blk], out)
        return out

    def run(subs):
        loaded = [sub(j, r) for j, r in subs]
        scores = [jnp.where(mask, _dot_t(qs, k), NEG) for _, qs, k, _, mask in loaded]
        stats = []
        for s in scores:
            m = jnp.max(s, axis=-1, keepdims=True)
            p = jnp.exp(s - m)
            stats.append((m, jnp.sum(p, axis=-1, keepdims=True), p.astype(BF16)))
        outs = [_dot(p, v) for (_, _, p), (_, _, _, v, _) in zip(stats, loaded)]
        for (rows, _, _, _, _), (m, l, _), pv in zip(loaded, stats, outs):
            l_c = pick(l)
            _store_halves(o_ref, 0, pick(pv) / l_c, rows)
            _store_halves(lse_ref, 0, pick(m) + jnp.log(l_c), rows)

    if dil * n_sub <= 4:
        run([(j, r) for j in range(n_sub) for r in range(dil)])
    else:
        per_trip = 4
        for j in range(n_sub):
            def body(it, carry, j=j):
                run([(j, it * per_trip + u) for u in range(per_trip)])
                return carry

            lax.fori_loop(0, dil // per_trip, body, 0)


def _prompt_attn(q, kv, batch, seq, dil):
    span = ATTN_BLK * dil
    has_prev = seq > span
    n_sub = max(1, 512 // span) if has_prev else 1
    rows = n_sub * span
    steps = seq // rows
    cur = lambda b, n: (0, b * steps + n, 0)
    in_specs = [_halves_spec(LANE_SPLIT, rows, cur), _halves_spec(2 * LANE_SPLIT, rows, cur)]
    args = [q, kv]
    if has_prev:
        in_specs.append(_halves_spec(2 * LANE_SPLIT, span,
                                     lambda b, n: (0, b * (seq // span) + jnp.maximum(n * n_sub - 1, 0), 0)))
        args.append(kv)
    return pl.pallas_call(
        functools.partial(_pattn_kernel, dil=dil, n_sub=n_sub, has_prev=has_prev),
        grid=(batch, steps),
        in_specs=in_specs,
        out_specs=[_halves_spec(LANE_SPLIT, rows, cur)] * 2,
        out_shape=[jax.ShapeDtypeStruct((LANE_SPLIT, batch * seq, LANES), F32)] * 2,
        compiler_params=_cparams(("arbitrary",) * 2),
        name=f"prompt_attn_d{dil}",
    )(*args)


def _sattn_kernel(q_ref, kvn_ref, ct_ref, o_ref, lse_ref, qm, osel, lsel, *, dil, bb_n, t_n):
    n_buf = ct_ref.shape[-1]
    hp = HEADS_PER_GROUP
    head = lax.broadcasted_iota(jnp.int32, (1, GROUP_W), 1) // HEAD_DIM
    sel = head == lax.broadcasted_iota(jnp.int32, (hp, 1), 0)
    row = lax.broadcasted_iota(jnp.int32, (t_n * hp, 1), 0)
    trow = row // hp
    sel_rows = head == row % hp
    pos = lax.broadcasted_iota(jnp.int32, (1, n_buf), 1)
    cmask = (pos >= trow) if dil == 1 else ((pos & (dil - 1)) == trow)
    qmbs, scores = [], []
    for bb in range(bb_n):
        for t in range(t_n):
            r = bb * t_n + t
            qm[bb, t * hp:(t + 1) * hp, :] = jnp.where(sel, _load_halves(q_ref, 0, slice(r, r + 1)), 0.0)
        qmbs.append(qm[bb].astype(BF16))
        scores.append(jnp.where(cmask, _dot(qmbs[bb], ct_ref[bb, 0].astype(BF16)), NEG))
    stats = []
    for bb in range(bb_n):
        qmr = qmbs[bb].astype(F32)
        m = jnp.max(scores[bb], axis=-1, keepdims=True)
        s_n = []
        for u in range(t_n):
            r = bb * t_n + u
            kn = _load_halves(kvn_ref, 0, slice(r, r + 1)).astype(BF16).astype(F32)
            valid = (trow >= u) if dil == 1 else (trow == u)
            sn = jnp.where(valid, jnp.sum(qmr * kn, axis=-1, keepdims=True), NEG)
            s_n.append(sn)
            m = jnp.maximum(m, sn)
        p_c = jnp.exp(scores[bb] - m)
        stats.append((m, s_n, jnp.sum(p_c, axis=-1, keepdims=True), p_c.astype(BF16)))
    accs = [_dot_t(stats[bb][3], ct_ref[bb, 1].astype(BF16)) for bb in range(bb_n)]
    for bb in range(bb_n):
        m, s_n, l, _ = stats[bb]
        acc = accs[bb]
        for u in range(t_n):
            p_n = jnp.exp(s_n[u] - m)
            l = l + p_n
            r = bb * t_n + u
            vn = _load_halves(kvn_ref, LANE_SPLIT, slice(r, r + 1)).astype(BF16).astype(F32)
            acc = acc + p_n.astype(BF16).astype(F32) * vn
        osel[bb] = jnp.where(sel_rows, acc / l, 0.0)
        lsel[bb] = jnp.where(sel_rows, m + jnp.log(l), 0.0)
        for t in range(t_n):
            r = slice(bb * t_n + t, bb * t_n + t + 1)
            _store_halves(o_ref, 0, jnp.sum(osel[bb, t * hp:(t + 1) * hp, :], axis=0, keepdims=True), r)
            _store_halves(lse_ref, 0, jnp.sum(lsel[bb, t * hp:(t + 1) * hp, :], axis=0, keepdims=True), r)


def _sample_attn(q, kvn, cache, batch, t_n, dil, bb_n):
    n_buf = cache.shape[1]
    assert n_buf == ATTN_BLK * dil and (dil == 1 or t_n <= dil)
    ct = jnp.transpose(cache, (0, 2, 3, 4, 1)).reshape(batch, 2, GROUP_W, n_buf)
    rows_n = t_n * HEADS_PER_GROUP
    blk = lambda i: (0, i, 0)
    return pl.pallas_call(
        functools.partial(_sattn_kernel, dil=dil, bb_n=bb_n, t_n=t_n),
        grid=(batch // bb_n,),
        in_specs=[_halves_spec(LANE_SPLIT, bb_n * t_n, blk), _halves_spec(2 * LANE_SPLIT, bb_n * t_n, blk),
                  pl.BlockSpec((bb_n, 2, GROUP_W, n_buf), lambda i: (i, 0, 0, 0))],
        out_specs=[_halves_spec(LANE_SPLIT, bb_n * t_n, blk)] * 2,
        out_shape=[jax.ShapeDtypeStruct((LANE_SPLIT, batch * t_n, LANES), F32)] * 2,
        scratch_shapes=[pltpu.VMEM((bb_n, rows_n, GROUP_W), F32)] * 3,
        compiler_params=_cparams(("arbitrary",)),
        name=f"sample_attn_d{dil}",
    )(q, kvn, ct)


def _ssm_kernel(u_ref, s0r_ref, s0i_ref, lre_ref, lim_ref, ldt_ref, bre_ref, bim_ref, cre_ref, cim_ref, d_ref,
                y_ref, fr_ref, fi_ref, hre, him, str_, sti, *, n_seq, steps):
    i = pl.program_id(0)

    @pl.when(i == 0)
    def _():
        str_[...] = s0r_ref[...]
        sti[...] = s0i_ref[...]

    lam_re, lam_im = lre_ref[...], lim_ref[...]
    dt = jnp.exp(ldt_ref[...])
    mag = jnp.exp(lam_re * dt)
    a_re = mag * jnp.cos(lam_im * dt)
    a_im = mag * jnp.sin(lam_im * dt)
    den = lam_re * lam_re + lam_im * lam_im
    nr, ni = a_re - 1.0, a_im
    c_re = (nr * lam_re + ni * lam_im) / den
    c_im = (ni * lam_re - nr * lam_im) / den

    u = _load_halves(u_ref, 0, slice(None))
    ub = u.astype(BF16)
    bu_re = _dot(ub, bre_ref[...])
    bu_im = _dot(ub, bim_ref[...])
    hre[...] = c_re * bu_re - c_im * bu_im
    him[...] = c_re * bu_im + c_im * bu_re

    if n_seq <= 8:
        def step(t, carry):
            sr, si = carry
            rows = pl.ds(pl.multiple_of(t * n_seq, n_seq), n_seq)
            nr_ = a_re * sr - a_im * si + hre[rows, :]
            ni_ = a_re * si + a_im * sr + him[rows, :]
            hre[rows, :] = nr_
            him[rows, :] = ni_
            return nr_, ni_

        sr, si = lax.fori_loop(0, steps, step, (str_[...], sti[...]))
        str_[...] = sr
        sti[...] = si
    else:
        for t in range(steps):
            rows = slice(t * n_seq, (t + 1) * n_seq)
            sr, si = str_[...], sti[...]
            nr_ = a_re * sr - a_im * si + hre[rows, :]
            ni_ = a_re * si + a_im * sr + him[rows, :]
            hre[rows, :] = nr_
            him[rows, :] = ni_
            str_[...] = nr_
            sti[...] = ni_

    y = _dot(hre[...].astype(BF16), cre_ref[...]) - _dot(him[...].astype(BF16), cim_ref[...])
    _store_halves(y_ref, 0, y + d_ref[...] * u)
    fr_ref[...] = str_[...]
    fi_ref[...] = sti[...]


def _ssm(u_tm, s0r, s0i, lam_re, lam_im, log_dt, b_re, b_im, c_re, c_im, dvec, n_seq, n_steps, chunk):
    rows = chunk * n_seq
    fixed = lambda i: (0, 0)
    sds = jax.ShapeDtypeStruct
    return pl.pallas_call(
        functools.partial(_ssm_kernel, n_seq=n_seq, steps=chunk),
        grid=(n_steps // chunk,),
        in_specs=[_halves_spec(LANE_SPLIT, rows, lambda i: (0, i, 0)),
                  pl.BlockSpec((n_seq, SSM_LANES), fixed), pl.BlockSpec((n_seq, SSM_LANES), fixed),
                  pl.BlockSpec((1, SSM_LANES), fixed), pl.BlockSpec((1, SSM_LANES), fixed),
                  pl.BlockSpec((1, SSM_LANES), fixed),
                  pl.BlockSpec((SSM_WIDTH, SSM_LANES), fixed), pl.BlockSpec((SSM_WIDTH, SSM_LANES), fixed),
                  pl.BlockSpec((SSM_LANES, SSM_WIDTH), fixed), pl.BlockSpec((SSM_LANES, SSM_WIDTH), fixed),
                  pl.BlockSpec((1, SSM_WIDTH), fixed)],
        out_specs=[_halves_spec(LANE_SPLIT, rows, lambda i: (0, i, 0)),
                   pl.BlockSpec((n_seq, SSM_LANES), fixed), pl.BlockSpec((n_seq, SSM_LANES), fixed)],
        out_shape=[sds((LANE_SPLIT, n_steps * n_seq, LANES), F32), sds((n_seq, SSM_LANES), F32),
                   sds((n_seq, SSM_LANES), F32)],
        scratch_shapes=[pltpu.VMEM((rows, SSM_LANES), F32), pltpu.VMEM((rows, SSM_LANES), F32),
                        pltpu.VMEM((n_seq, SSM_LANES), F32), pltpu.VMEM((n_seq, SSM_LANES), F32)],
        compiler_params=_cparams(("arbitrary",)),
        name=f"ssm_scan_{n_seq}",
    )(u_tm, s0r, s0i, lam_re, lam_im, log_dt, b_re, b_im, c_re, c_im, dvec)


def _mixout_value(o_refs, l_refs, y, x_ref, ga_ref, gs_ref, wglu_ref, bglu_ref, wout_ref):
    full = slice(None)
    zs =0.5 * y * (1.0 + jnp.tanh(math.sqrt(2.0 / math.pi) * (y + 0.044715 * (y * y * y))))
    glu = _dot(zs.astype(BF16), wglu_ref[...])
    l0, l1, l2 = (_load_halves(r, 0, full) for r in l_refs)
    m = jnp.maximum(jnp.maximum(l0, l1), l2)
    e0, e1, e2 = jnp.exp(l0 - m), jnp.exp(l1 - m), jnp.exp(l2 - m)
    den = e0 + e1 + e2
    a = [_load_halves(r, 0, full) * (e / den) for r, e in zip(o_refs, (e0, e1, e2))]
    ssq = sum(jnp.sum(ai * ai, axis=-1, keepdims=True) for ai in a)
    inv = lax.rsqrt(ssq * (1.0 / ATTN_WIDTH) + EPS)
    acc = x_ref[...]
    for g in range(N_GROUPS):
        an = (a[g] * inv * ga_ref[:, g * GROUP_W:(g + 1) * GROUP_W]).astype(BF16)
        acc = acc + _dot(an, wout_ref[g * GROUP_W:(g + 1) * GROUP_W, :])
    gate = jax.nn.sigmoid(glu + bglu_ref[...])
    so = _rms_rows(zs * gate, gs_ref[...]).astype(BF16)
    return acc + _dot(so, wout_ref[ATTN_WIDTH:, :])


def _mixout_kernel(o0_ref, o1_ref, o2_ref, l0_ref, l1_ref, l2_ref, y_ref, x_ref,
                   ga_ref, gs_ref, wglu_ref, bglu_ref, wout_ref, h_ref):
    h_ref[...] = _mixout_value((o0_ref, o1_ref, o2_ref), (l0_ref, l1_ref, l2_ref),
                               _load_halves(y_ref, 0, slice(None)), x_ref,
                               ga_ref, gs_ref, wglu_ref, bglu_ref, wout_ref)


def _mix_out(os_, ls_, y, x2d, g_attn, g_ssm, wglu_bf, bglu, wout_bf, tm):
    n = x2d.shape[0]
    row = lambda i: (i, 0)
    fixed = lambda i: (0, 0)
    return pl.pallas_call(
        _mixout_kernel,
        grid=(n // tm,),
        in_specs=[_halves_spec(LANE_SPLIT, tm, lambda i: (0, i, 0))] * 7 + [pl.BlockSpec((tm, D_MODEL), row)]
                 + [pl.BlockSpec((1, ATTN_WIDTH), fixed), pl.BlockSpec((1, SSM_WIDTH), fixed),
                    pl.BlockSpec((SSM_WIDTH, SSM_WIDTH), fixed), pl.BlockSpec((1, SSM_WIDTH), fixed),
                    pl.BlockSpec((D_MODEL, D_MODEL), fixed)],
        out_specs=pl.BlockSpec((tm, D_MODEL), row),
        out_shape=jax.ShapeDtypeStruct((n, D_MODEL), F32),
        compiler_params=_cparams(("arbitrary",)),
        name="mix_out",
    )(*os_, *ls_, y, x2d, g_attn, g_ssm, wglu_bf, bglu, wout_bf)


def _memkv_kernel(x_ref, g_ref, w_ref, kn_ref, o_ref, rows_ref):
    xn = _rms_rows(x_ref[...], g_ref[...]).astype(BF16)
    kn = kn_ref[...]
    for kv in range(2):
        for h in range(X_HEADS):
            cols = slice(kv * D_MODEL + h * X_HEAD_DIM, kv * D_MODEL + (h + 1) * X_HEAD_DIM)
            z = _dot(xn, w_ref[:, cols])
            if kv == 0:
                z = _rms_rows(z, kn)
            o_ref[:, cols] = z
            for half in range(MEM_HALVES):
                first = (kv * MEM_HALVES + half) * X_HEADS + h
                rows_ref[pl.ds(first, N_MEM, stride=MEM_ROW_STRIDE), :] = z[:, half * LANES:(half + 1) * LANES]


def _memory_kv(mem2d, gain, w_bf, kn, n_seq):
    fixed = lambda i: (0, 0)
    sds = jax.ShapeDtypeStruct
    return pl.pallas_call(
        _memkv_kernel,
        grid=(n_seq,),
        in_specs=[pl.BlockSpec((N_MEM, D_MODEL), lambda i: (i, 0)), pl.BlockSpec((1, D_MODEL), fixed),
                  pl.BlockSpec((D_MODEL, 2 * D_MODEL), fixed), pl.BlockSpec((1, X_HEAD_DIM), fixed)],
        out_specs=[pl.BlockSpec((N_MEM, 2 * D_MODEL), lambda i: (i, 0)),
                   pl.BlockSpec((None, N_MEM * MEM_ROW_STRIDE, LANES), lambda i: (i, 0, 0))],
        out_shape=[sds((n_seq * N_MEM, 2 * D_MODEL), F32), sds((n_seq, N_MEM * MEM_ROW_STRIDE, LANES), F32)],
        compiler_params=_cparams(("arbitrary",)),
        name="memory_kv",
    )(mem2d, gain, w_bf, kn)


def _mix_cross_kernel(o0_ref, o1_ref, o2_ref, l0_ref, l1_ref, l2_ref, y_ref, x_ref,
                      ga_ref, gs_ref, wglu_ref, bglu_ref, wout_ref,
                      mkv_ref, gx_ref, wq_ref, qn_ref, wo_ref, o_ref, att, *, n_seq, blocks_per_seq):
    tm = x_ref.shape[0]
    seq = pl.program_id(0) // blocks_per_seq
    h = _mixout_value((o0_ref, o1_ref, o2_ref), (l0_ref, l1_ref, l2_ref),
                      _load_halves(y_ref, 0, pl.ds(seq, tm, stride=n_seq)), x_ref,
                      ga_ref, gs_ref, wglu_ref, bglu_ref, wout_ref)
    hn = _rms_rows(h, gx_ref[...]).astype(BF16)
    qn = qn_ref[...]
    heads = [slice(hd * X_HEAD_DIM, (hd + 1) * X_HEAD_DIM) for hd in range(X_HEADS)]
    qz = [_dot(hn, wq_ref[:, cols]) for cols in heads]
    qs = [(_rms_rows(z, qn) * (X_HEAD_DIM ** -0.5)).astype(BF16) for z in qz]
    scores = [_dot_t(q, mkv_ref[:, cols].astype(BF16)) for q, cols in zip(qs, heads)]
    probs = []
    for s in scores:
        p = jnp.exp(s - jnp.max(s, axis=-1, keepdims=True))
        probs.append((p.astype(BF16), jnp.sum(p, axis=-1, keepdims=True)))
    outs = [_dot(p, mkv_ref[:, D_MODEL + cols.start:D_MODEL + cols.stop].astype(BF16)) for (p, _), cols in zip(probs, heads)]
    for pv, (_, l), cols in zip(outs, probs, heads):
        att[:, cols] = (pv / l).astype(BF16)
    o_ref[...] = h + _dot(att[...], wo_ref[...])


def _mix_cross(os_, ls_, y, x2d, mkv, p, rows_per_seq, tm):
    n = x2d.shape[0]
    blocks_per_seq = rows_per_seq // tm
    n_seq = n // rows_per_seq
    row = lambda i: (i, 0)
    fixed = lambda i: (0, 0)
    return pl.pallas_call(
        functools.partial(_mix_cross_kernel, n_seq=n_seq, blocks_per_seq=blocks_per_seq),
        grid=(n // tm,),
        in_specs=[_halves_spec(LANE_SPLIT, tm, lambda i: (0, i, 0))] * 6
                 + [_halves_spec(LANE_SPLIT, tm * n_seq, lambda i: (0, i % blocks_per_seq, 0)),
                    pl.BlockSpec((tm, D_MODEL), row)]
                 + [pl.BlockSpec((1, ATTN_WIDTH), fixed), pl.BlockSpec((1, SSM_WIDTH), fixed),
                    pl.BlockSpec((SSM_WIDTH, SSM_WIDTH), fixed), pl.BlockSpec((1, SSM_WIDTH), fixed),
                    pl.BlockSpec((D_MODEL, D_MODEL), fixed)]
                 + [pl.BlockSpec((None, N_MEM, 2 * D_MODEL), lambda i: (i // blocks_per_seq, 0, 0)),
                    pl.BlockSpec((1, D_MODEL), fixed), pl.BlockSpec((D_MODEL, D_MODEL), fixed),
                    pl.BlockSpec((1, X_HEAD_DIM), fixed), pl.BlockSpec((D_MODEL, D_MODEL), fixed)],
        out_specs=pl.BlockSpec((tm, D_MODEL), row),
        out_shape=jax.ShapeDtypeStruct((n, D_MODEL), F32),
        scratch_shapes=[pltpu.VMEM((tm, D_MODEL), BF16)],
        compiler_params=_cparams(("arbitrary",)),
        name="mix_cross",
    )(*os_, *ls_, y, x2d, p['attn_out_norm'], p['ssm_out_norm'], p['w_glu'], p['b_glu'], p['w_out'],
      mkv, p['norm_x'], p['w_xq'], p['xq_norm'], p['w_xo'])


MEM_HALVES = X_HEAD_DIM // 128
MEM_ROW_STRIDE = 2 * MEM_HALVES * X_HEADS


def _cross_sample_kernel(h_ref, mem_ref, g_ref, wq_ref, qn_ref, wo_ref, o_ref, qsc, qm, osel, att, *, bb_n, t_n):
    h = h_ref[...]
    hn = _rms_rows(h, g_ref[...]).astype(BF16)
    qn = qn_ref[...]
    for hd in range(X_HEADS):
        cols = slice(hd * X_HEAD_DIM, (hd + 1) * X_HEAD_DIM)
        qsc[:, cols] = _rms_rows(_dot(hn, wq_ref[:, cols]), qn) * (X_HEAD_DIM ** -0.5)
    xh = X_HEADS
    head = lax.broadcasted_iota(jnp.int32, (1, D_MODEL), 1) // X_HEAD_DIM
    sel = head == lax.broadcasted_iota(jnp.int32, (xh, 1), 0)
    sel_rows = head == lax.broadcasted_iota(jnp.int32, (t_n * xh, 1), 0) % xh

    def gather(bb, kv):
        parts = [mem_ref[bb, pl.ds((kv * MEM_HALVES + half) * xh + hd, N_MEM, stride=MEM_ROW_STRIDE), :]
                 for hd in range(xh) for half in range(MEM_HALVES)]
        return jnp.concatenate(parts, axis=1).astype(BF16)

    scores = []
    for bb in range(bb_n):
        for t in range(t_n):
            r = bb * t_n + t
            qm[bb, t * xh:(t + 1) * xh, :] = jnp.where(sel, qsc[r:r + 1, :], 0.0)
        scores.append(_dot_t(qm[bb].astype(BF16), gather(bb, 0)))
    probs = []
    for s in scores:
        p = jnp.exp(s - jnp.max(s, axis=-1, keepdims=True))
        probs.append((p.astype(BF16), jnp.sum(p, axis=-1, keepdims=True)))
    outs = [_dot(probs[bb][0], gather(bb, 1)) for bb in range(bb_n)]
    for bb in range(bb_n):
        osel[bb] = jnp.where(sel_rows, outs[bb] / probs[bb][1], 0.0)
        for t in range(t_n):
            r = bb * t_n + t
            att[r:r + 1, :] = jnp.sum(osel[bb, t * xh:(t + 1) * xh, :], axis=0, keepdims=True)
    o_ref[...] = h + _dot(att[...].astype(BF16), wo_ref[...])


def _cross_sample(h2d, mem_kv, gain, wq_bf, qn, wo_bf, batch, t_n, bb_n):
    mem = mem_kv.reshape(batch, N_MEM, 2, X_HEADS, MEM_HALVES, 128)
    mem = jnp.transpose(mem, (0, 1, 2, 4, 3, 5)).reshape(batch, N_MEM * MEM_ROW_STRIDE, 128)
    rows = bb_n * t_n
    fixed = lambda i: (0, 0)
    return pl.pallas_call(
        functools.partial(_cross_sample_kernel, bb_n=bb_n, t_n=t_n),
        grid=(batch // bb_n,),
        in_specs=[pl.BlockSpec((rows, D_MODEL), lambda i: (i, 0)),
                  pl.BlockSpec((bb_n, N_MEM * MEM_ROW_STRIDE, 128), lambda i: (i, 0, 0)),
                  pl.BlockSpec((1, D_MODEL), fixed), pl.BlockSpec((D_MODEL, D_MODEL), fixed),
                  pl.BlockSpec((1, X_HEAD_DIM), fixed), pl.BlockSpec((D_MODEL, D_MODEL), fixed)],
        out_specs=pl.BlockSpec((rows, D_MODEL), lambda i: (i, 0)),
        out_shape=jax.ShapeDtypeStruct((batch * t_n, D_MODEL), F32),
        scratch_shapes=[pltpu.VMEM((rows, D_MODEL), F32), pltpu.VMEM((bb_n, t_n * X_HEADS, D_MODEL), F32),
                        pltpu.VMEM((bb_n, t_n * X_HEADS, D_MODEL), F32), pltpu.VMEM((rows, D_MODEL), F32)],
        compiler_params=_cparams(("arbitrary",)),
        name="cross_attn_sample",
    )(h2d, mem, gain, wq_bf, qn, wo_bf)


FFN_CHUNKS = 2
FFN_TF = D_FF // FFN_CHUNKS
CARRY_ROWS = 8


def _conv_gate(up_g, up_v, prev_g, prev_v, bufg, bufv, cwg_ref, cwv_ref, cbg_ref, cbv_ref, tm, lag):
    hist = prev_g.shape[0]
    bufg[0:hist] = prev_g
    bufg[hist:hist + tm] = up_g
    bufv[0:hist] = prev_v
    bufv[hist:hist + tm] = up_v
    cg = cbg_ref[...] + (cwg_ref[0:1, :] * bufg[hist - 2 * lag:hist - 2 * lag + tm]
                         + cwg_ref[1:2, :] * bufg[hist - lag:hist - lag + tm] + cwg_ref[2:3, :] * up_g)
    cv = cbv_ref[...] + (cwv_ref[0:1, :] * bufv[hist - 2 * lag:hist - 2 * lag + tm]
                         + cwv_ref[1:2, :] * bufv[hist - lag:hist - lag + tm] + cwv_ref[2:3, :] * up_v)
    return (cg * jax.nn.sigmoid(cg) * cv).astype(BF16)


FFN_COLS = 256
FFN_NCHUNK = D_FF // FFN_COLS


def _ffn_prompt_kernel(h_ref, g_ref, wup_ref, cw_ref, cb_ref, wd_ref, o_ref, tail_ref, act, carry, *, tm):
    i = pl.program_id(1)
    h = h_ref[...]
    xn = _rms_rows(h, g_ref[...]).astype(BF16)

    @pl.when(i == 0)
    def _():
        carry[...] = jnp.zeros(carry.shape, F32)

    for n in range(FFN_NCHUNK):
        conv = []
        for part in range(2):
            cols = slice(part * D_FF + n * FFN_COLS, part * D_FF + (n + 1) * FFN_COLS)
            up = _dot(xn, wup_ref[:, cols])
            ext = jnp.concatenate([carry[part, n], up], axis=0)
            conv.append(cb_ref[:, cols] + (cw_ref[0:1, cols] * ext[CARRY_ROWS - 2:CARRY_ROWS - 2 + tm]
                                           + cw_ref[1:2, cols] * ext[CARRY_ROWS - 1:CARRY_ROWS - 1 + tm]
                                           + cw_ref[2:3, cols] * up))
            carry[part, n] = up[tm - CARRY_ROWS:]
            tail_ref[:, cols] = up[tm - CARRY_ROWS:]
        cg, cv = conv
        act[:, n * FFN_COLS:(n + 1) * FFN_COLS] = (cg * jax.nn.sigmoid(cg) * cv).astype(BF16)
    o_ref[...] = h + _dot(act[...], wd_ref[...])


def _ffn_prompt(h2d, gain, wup_bf, conv_w, conv_b, wdown_bf, batch, seq, tm):
    n_i = seq // tm
    row = lambda b, i: (b * n_i + i, 0)
    fixed = lambda b, i: (0, 0)
    once = pl.Buffered(1)
    sds = jax.ShapeDtypeStruct
    return pl.pallas_call(
        functools.partial(_ffn_prompt_kernel, tm=tm),
        grid=(batch, n_i),
        in_specs=[pl.BlockSpec((tm, D_MODEL), row), pl.BlockSpec((1, D_MODEL), fixed),
                  pl.BlockSpec((D_MODEL, 2 * D_FF), fixed, pipeline_mode=once),
                  pl.BlockSpec((CONV_W, 2 * D_FF), fixed), pl.BlockSpec((1, 2 * D_FF), fixed),
                  pl.BlockSpec((D_FF, D_MODEL), fixed, pipeline_mode=once)],
        out_specs=[pl.BlockSpec((tm, D_MODEL), row),
                   pl.BlockSpec((None, None, CARRY_ROWS, 2 * D_FF), lambda b, i: (b, i, 0, 0))],
        out_shape=[sds((batch * seq, D_MODEL), F32), sds((batch, n_i, CARRY_ROWS, 2 * D_FF), F32)],
        scratch_shapes=[pltpu.VMEM((tm, D_FF), BF16), pltpu.VMEM((2, FFN_NCHUNK, CARRY_ROWS, FFN_COLS), F32)],
        compiler_params=_cparams(("arbitrary",) * 2),
        name="ffn_prompt",
    )(h2d, gain, wup_bf, conv_w, conv_b, wdown_bf)


def _ffn_sample_kernel(h_ref, g_ref, sg_ref, sv_ref, wg_ref, wv_ref, cwg_ref, cwv_ref, cbg_ref, cbv_ref, wd_ref,
                       o_ref, ng_ref, nv_ref, xn, bufg, bufv, *, n_seq, rows):
    c = pl.program_id(0)

    @pl.when(c == 0)
    def _():
        xn[...] = _rms_rows(h_ref[...], g_ref[...]).astype(BF16)

    up_g = _dot(xn[...], wg_ref[...])
    up_v = _dot(xn[...], wv_ref[...])
    act = _conv_gate(up_g, up_v, sg_ref[...], sv_ref[...], bufg, bufv, cwg_ref, cwv_ref, cbg_ref, cbv_ref,
                     rows, n_seq)
    hist = (CONV_W - 1) * n_seq
    ng_ref[...] = bufg[rows:rows + hist]
    nv_ref[...] = bufv[rows:rows + hist]
    contrib = _dot(act, wd_ref[...])

    @pl.when(c == 0)
    def _():
        o_ref[...] = h_ref[...] + contrib

    @pl.when(c > 0)
    def _():
        o_ref[...] += contrib


def _ffn_sample(h_tm, gain, state_tm, wup_bf, conv_w, conv_b, wdown_bf, n_seq, n_steps):
    rows = n_seq * n_steps
    hist = (CONV_W - 1) * n_seq
    nc, tf = FFN_CHUNKS, FFN_TF
    gcol = lambda c: (0, c)
    vcol = lambda c: (0, nc + c)
    fixed = lambda c: (0, 0)
    sds = jax.ShapeDtypeStruct
    return pl.pallas_call(
        functools.partial(_ffn_sample_kernel, n_seq=n_seq, rows=rows),
        grid=(nc,),
        in_specs=[pl.BlockSpec((rows, D_MODEL), fixed), pl.BlockSpec((1, D_MODEL), fixed),
                  pl.BlockSpec((hist, tf), gcol), pl.BlockSpec((hist, tf), vcol),
                  pl.BlockSpec((D_MODEL, tf), gcol), pl.BlockSpec((D_MODEL, tf), vcol),
                  pl.BlockSpec((CONV_W, tf), gcol), pl.BlockSpec((CONV_W, tf), vcol),
                  pl.BlockSpec((1, tf), gcol), pl.BlockSpec((1, tf), vcol),
                  pl.BlockSpec((tf, D_MODEL), lambda c: (c, 0))],
        out_specs=[pl.BlockSpec((rows, D_MODEL), fixed), pl.BlockSpec((hist, tf), gcol),
                   pl.BlockSpec((hist, tf), gcol)],
        out_shape=[sds((rows, D_MODEL), F32), sds((hist, D_FF), F32), sds((hist, D_FF), F32)],
        scratch_shapes=[pltpu.VMEM((rows, D_MODEL), BF16), pltpu.VMEM((rows + hist, tf), F32),
                        pltpu.VMEM((rows + hist, tf), F32)],
        compiler_params=_cparams(("arbitrary",)),
        name="ffn_sample",
    )(h_tm, gain, state_tm, state_tm, wup_bf, wup_bf, conv_w, conv_w, conv_b, conv_b, wdown_bf)


def _rope_tables(pos):
    half = ROT_DIM // 2
    inv = jnp.power(jnp.float32(ROPE_THETA), -jnp.arange(half, dtype=F32) * 2.0 / ROT_DIM)
    ang = pos.astype(F32)[:, None] * inv[None, :]
    cos, sin = jnp.cos(ang), jnp.sin(ang)
    ones = jnp.ones((pos.shape[0], HEAD_DIM - ROT_DIM), F32)
    zeros = jnp.zeros((pos.shape[0], HEAD_DIM - ROT_DIM), F32)
    zh = jnp.zeros_like(sin)
    c_head = jnp.concatenate([cos, cos, ones], axis=1)
    s1_head = jnp.concatenate([-sin, zh, zeros], axis=1)
    s2_head = jnp.concatenate([zh, sin, zeros], axis=1)
    tile = lambda a: jnp.tile(a, (1, HEADS_PER_GROUP))
    return tile(c_head), tile(s1_head), tile(s2_head)


def _block_diag(w):
    g, a, b = w.shape
    eye = jnp.eye(g, dtype=w.dtype)
    return (eye[:, None, :, None] * w[:, :, None, :]).reshape(g * a, g * b)


def _layer_params(l, norm_mix, w_in, q_norm, k_norm, ssm_lam_re, ssm_lam_im, ssm_log_dt, ssm_b_re, ssm_b_im,
                  ssm_c_re, ssm_c_im, ssm_d, w_glu, b_glu, attn_out_norm, ssm_out_norm, w_out, norm_x, norm_mem,
                  w_xq, w_xkv, xq_norm, xk_norm, w_xo, norm_ffn, w_up, conv_w, conv_b, w_down):
    row = lambda a: a.reshape(1, -1).astype(F32)
    return dict(
        norm_mix=row(norm_mix[l]), w_in=w_in[l].astype(BF16),
        q_norm=row(jnp.tile(q_norm[l], HEADS_PER_GROUP)), k_norm=row(jnp.tile(k_norm[l], HEADS_PER_GROUP)),
        lam_re=row(ssm_lam_re[l]), lam_im=row(ssm_lam_im[l]),
        log_dt=row(jnp.repeat(ssm_log_dt[l], SSM_STATE)),
        b_re=_block_diag(jnp.swapaxes(ssm_b_re[l], 1, 2)).astype(BF16),
        b_im=_block_diag(jnp.swapaxes(ssm_b_im[l], 1, 2)).astype(BF16),
        c_re=_block_diag(jnp.swapaxes(ssm_c_re[l], 1, 2)).astype(BF16),
        c_im=_block_diag(jnp.swapaxes(ssm_c_im[l], 1, 2)).astype(BF16),
        ssm_d=row(ssm_d[l]), w_glu=w_glu[l].astype(BF16), b_glu=row(b_glu[l]),
        attn_out_norm=row(attn_out_norm[l]), ssm_out_norm=row(ssm_out_norm[l]), w_out=w_out[l].astype(BF16),
        norm_x=row(norm_x[l]), norm_mem=row(norm_mem[l]), w_xq=w_xq[l].astype(BF16),
        w_xkv=w_xkv[l].astype(BF16), xq_norm=row(xq_norm[l]), xk_norm=row(xk_norm[l]),
        w_xo=w_xo[l].astype(BF16), norm_ffn=row(norm_ffn[l]), w_up=w_up[l].astype(BF16),
        conv_w=conv_w[l].astype(F32), conv_b=row(conv_b[l]), w_down=w_down[l].astype(BF16))


def _kv_rows(kv_halves, n_seq, n_steps):
    heads_per_half = LANES // HEAD_DIM
    a = kv_halves.reshape(2, LANE_SPLIT, n_seq, n_steps, heads_per_half, HEAD_DIM)
    return jnp.transpose(a, (2, 3, 0, 1, 4, 5)).reshape(n_seq, n_steps, 2, HEADS_PER_GROUP, HEAD_DIM)


def _to_time_major(a, n_seq, n_steps):
    return a.reshape(n_seq, n_steps, -1).swapaxes(0, 1).reshape(n_steps * n_seq, -1)


def _to_seq_major(a, n_seq, n_steps):
    return a.reshape(n_steps, n_seq, -1).swapaxes(0, 1).reshape(n_seq * n_steps, -1)


def _mixer_front(x2d, p, tables, tm, feature_major_last=False):
    return _in_proj(x2d, p['norm_mix'], p['w_in'], p['q_norm'], p['k_norm'], *tables, tm, feature_major_last)


def _run_ssm(u_tm, s0r, s0i, p, n_seq, n_steps, chunk):
    return _ssm(u_tm, s0r, s0i, p['lam_re'], p['lam_im'], p['log_dt'], p['b_re'], p['b_im'],
                p['c_re'], p['c_im'], p['ssm_d'], n_seq, n_steps, chunk)


def _halves_to_time_major(a, n_seq, n_steps):
    return a.reshape(LANE_SPLIT, n_seq, n_steps, LANES).swapaxes(1, 2).reshape(LANE_SPLIT, n_steps * n_seq, LANES)


def _halves_to_seq_major(a, n_seq, n_steps):
    return a.reshape(LANE_SPLIT, n_steps, n_seq, LANES).swapaxes(1, 2).reshape(LANE_SPLIT, n_seq * n_steps, LANES)


def kernel(x_prompt, x_sample, mem_prompt, cache_kv_dil1, cache_kv_dil4, cache_kv_dil16, state_ssm_re, state_ssm_im, state_ffn_conv, cache_mem_kv, norm_mix, w_in, q_norm, k_norm, ssm_lam_re, ssm_lam_im, ssm_log_dt, ssm_b_re, ssm_b_im, ssm_c_re, ssm_c_im, ssm_d, w_glu, b_glu, attn_out_norm, ssm_out_norm, w_out, norm_x, norm_mem, w_xq, w_xkv, xq_norm, xk_norm, w_xo, norm_ffn, w_up, conv_w, conv_b, w_down):
    bp, lp, _ = x_prompt.shape
    bs, ls, _ = x_sample.shape
    depth = w_in.shape[0]
    past_len = cache_kv_dil16.shape[2]
    caches = (cache_kv_dil1, cache_kv_dil4, cache_kv_dil16)
    tables_p = _rope_tables(jnp.arange(lp))
    tables_s = tuple(jnp.tile(t, (bs, 1)) for t in _rope_tables(past_len + jnp.arange(ls)))

    yp = x_prompt.reshape(bp * lp, D_MODEL)
    ys = x_sample.reshape(bs * ls, D_MODEL)
    kv_p = [[] for _ in range(N_GROUPS)]
    kv_s = [[] for _ in range(N_GROUPS)]
    re_p, im_p, conv_p, mem_p, re_s, im_s, conv_s = [], [], [], [], [], [], []
    for l in range(depth):
        p = _layer_params(l, norm_mix, w_in, q_norm, k_norm, ssm_lam_re, ssm_lam_im, ssm_log_dt, ssm_b_re,
                          ssm_b_im, ssm_c_re, ssm_c_im, ssm_d, w_glu, b_glu, attn_out_norm, ssm_out_norm, w_out,
                          norm_x, norm_mem, w_xq, w_xkv, xq_norm, xk_norm, w_xo, norm_ffn, w_up, conv_w, conv_b,
                          w_down)
        mkv, mkv_rows = _memory_kv(mem_prompt.reshape(bp * N_MEM, D_MODEL), p['norm_mem'], p['w_xkv'],
                                   p['xk_norm'], bp)
        mkv = mkv.reshape(bp, N_MEM, 2 * D_MODEL)
        mkv_rows = mkv_rows.reshape(bp, N_MEM, 2, MEM_HALVES, X_HEADS, LANES)
        mem_p.append(jnp.transpose(mkv_rows, (0, 1, 2, 4, 3, 5)).reshape(bp, N_MEM, 2, X_HEADS, X_HEAD_DIM))
        assert ATTN_GROUPS[-1][0] >= lp
        q0, q1, q2, kv0, kv1, kv2, u, kvt = _mixer_front(yp, p, tables_p, 512, True)
        os_, ls_ = [], []
        for g, (qg, kvg) in enumerate(((q0, kv0), (q1, kv1), (q2, kv2))):
            win, dil = ATTN_GROUPS[g]
            o, lse = _prompt_attn(qg, kvg, bp, lp, dil)
            os_.append(o)
            ls_.append(lse)
            keep = min(win, lp)
            if g == N_GROUPS - 1:
                kv_p[g].append(jnp.transpose(kvt.reshape(bp, 2, HEADS_PER_GROUP, HEAD_DIM, lp), (0, 4, 1, 2, 3)))
            else:
                tail = kvg.reshape(2 * LANE_SPLIT, bp, lp, LANES)[:, :, lp - keep:].reshape(2 * LANE_SPLIT, bp * keep, LANES)
                kv_p[g].append(_kv_rows(tail, bp, keep))
        zero_state = jnp.zeros((bp, SSM_LANES), F32)
        y, fr, fi = _run_ssm(u, zero_state, zero_state, p, bp, lp, 128)
        re_p.append(fr.reshape(bp, SSM_GROUPS, SSM_STATE))
        im_p.append(fi.reshape(bp, SSM_GROUPS, SSM_STATE))
        h = _mix_cross(os_, ls_, y, yp, mkv, p, lp, 512)
        yp, tail = _ffn_prompt(h, p['norm_ffn'], p['w_up'], p['conv_w'], p['conv_b'], p['w_down'], bp, lp, 512)
        conv_p.append(tail[:, -1, CARRY_ROWS - (CONV_W - 1):])
        q0, q1, q2, kv0, kv1, kv2, u = _mixer_front(ys, p, tables_s, bs * ls)
        os_, ls_ = [], []
        for g, (qg, kvg) in enumerate(((q0, kv0), (q1, kv1), (q2, kv2))):
            win, dil = ATTN_GROUPS[g]
            o, lse = _sample_attn(qg, kvg, caches[g][l], bs, ls, dil, max(1, 32 // dil))
            os_.append(o)
            ls_.append(lse)
            kv_s[g].append(_kv_rows(kvg, bs, ls))
        y, fr, fi = _run_ssm(_halves_to_time_major(u, bs, ls), state_ssm_re[l].reshape(bs, SSM_LANES),
                             state_ssm_im[l].reshape(bs, SSM_LANES), p, bs, ls, ls)
        re_s.append(fr.reshape(bs, SSM_GROUPS, SSM_STATE))
        im_s.append(fi.reshape(bs, SSM_GROUPS, SSM_STATE))
        h = _mix_out(os_, ls_, _halves_to_seq_major(y, bs, ls), ys, p['attn_out_norm'], p['ssm_out_norm'], p['w_glu'], p['b_glu'], p['w_out'],
                     bs * ls)
        h = _cross_sample(h, cache_mem_kv[l], p['norm_x'], p['w_xq'], p['xq_norm'], p['w_xo'], bs, ls, 4)
        state_tm = state_ffn_conv[l].swapaxes(0, 1).reshape((CONV_W - 1) * bs, 2 * D_FF)
        ys_tm, ng, nv = _ffn_sample(_to_time_major(h, bs, ls), p['norm_ffn'], state_tm, p['w_up'], p['conv_w'],
                                    p['conv_b'], p['w_down'], bs, ls)
        ys = _to_seq_major(ys_tm, bs, ls)
        conv_s.append(jnp.concatenate([ng, nv], axis=-1).reshape(CONV_W - 1, bs, 2 * D_FF).swapaxes(0, 1))

    st = jnp.stack
    return (yp.reshape(bp, lp, D_MODEL), ys.reshape(bs, ls, D_MODEL),
            st(kv_p[0]), st(kv_p[1]), st(kv_p[2]), st(re_p), st(im_p), st(conv_p), st(mem_p),
            st(kv_s[0]), st(kv_s[1]), st(kv_s[2]), st(re_s), st(im_s), st(conv_s))
```

```python
import functools
import math

import jax
import jax.numpy as jnp
from jax import lax
from jax.experimental import pallas as pl
from jax.experimental.pallas import tpu as pltpu

F32 = jnp.float32
BF16 = jnp.bfloat16

D_MODEL = 1024
HEAD_DIM = 64
ATTN_GROUPS = ((128, 1), (512, 4), (2048, 16))
HEADS_PER_GROUP = 4
GROUP_W = HEADS_PER_GROUP * HEAD_DIM
N_GROUPS = len(ATTN_GROUPS)
ATTN_WIDTH = N_GROUPS * GROUP_W
SSM_WIDTH = D_MODEL - ATTN_WIDTH
SSM_GROUP = 16
SSM_GROUPS = SSM_WIDTH // SSM_GROUP
SSM_STATE = 64
SSM_LANES = SSM_GROUPS * SSM_STATE
IN_WIDTH = 3 * ATTN_WIDTH + SSM_WIDTH
ROT_DIM = HEAD_DIM // 4
ROPE_THETA = 500000.0
N_MEM = 256
X_HEADS = 4
X_HEAD_DIM = D_MODEL // X_HEADS
D_FF = 2816
CONV_W = 3
EPS = 1e-6
ATTN_BLK = 128
NEG = -1e30

VMEM_LIMIT = 56 * 1024 * 1024


def _cparams(sem):
    return pltpu.CompilerParams(dimension_semantics=sem, vmem_limit_bytes=VMEM_LIMIT)


def _dot(a, b):
    return jnp.dot(a, b, preferred_element_type=F32)


def _dot_t(a, b):
    return lax.dot_general(a, b, (((1,), (1,)), ((), ())), preferred_element_type=F32)


def _rms_rows(x, gain):
    y = x * lax.rsqrt(jnp.mean(x * x, axis=-1, keepdims=True) + EPS)
    return y * gain


LANES = 128
LANE_SPLIT = GROUP_W // LANES


def _halves_spec(n_halves, rows, index_map):
    return pl.BlockSpec((n_halves, rows, LANES), index_map)


def _load_halves(ref, first, rows):
    return jnp.concatenate([ref[first + c, rows, :] for c in range(LANE_SPLIT)], axis=1)


def _store_halves(ref, first, value, rows=slice(None)):
    for c in range(LANE_SPLIT):
        ref[first + c, rows, :] = value[:, c * LANES:(c + 1) * LANES]


def _head_ones():
    r = lax.broadcasted_iota(jnp.int32, (GROUP_W, GROUP_W), 0) // HEAD_DIM
    c = lax.broadcasted_iota(jnp.int32, (GROUP_W, GROUP_W), 1) // HEAD_DIM
    return (r == c).astype(BF16)


def _head_inv_rms(z, ones_head):
    sq = z * z
    hi = sq.astype(BF16)
    lo = (sq - hi.astype(F32)).astype(BF16)
    ssum = _dot(hi, ones_head) + _dot(lo, ones_head)
    return lax.rsqrt(ssum * (1.0 / HEAD_DIM) + EPS)


def _inproj_sattn_kernel(x_ref, g_ref, w_ref, qn_ref, kn_ref, c_ref, s1_ref, s2_ref, sq_ref, skvn_ref, ct_ref,
                         q0_ref, q1_ref, q2_ref, kv0_ref, kv1_ref, kv2_ref, u_ref, kvt_ref, so_ref, slse_ref,
                         qm, osel, lsel, *, n_seq, dil, bb_n, t_n):
    ctx = _sattn_scores(sq_ref, ct_ref, qm, dil, bb_n, t_n)
    _inproj_kernel(x_ref, g_ref, w_ref, qn_ref, kn_ref, c_ref, s1_ref, s2_ref,
                   q0_ref, q1_ref, q2_ref, kv0_ref, kv1_ref, kv2_ref, u_ref, kvt_ref, n_seq=n_seq,
                   after_first_group=lambda: _sattn_finish(ctx, skvn_ref, ct_ref, so_ref, slse_ref, osel, lsel))


def _inproj_kernel(x_ref, g_ref, w_ref, qn_ref, kn_ref, c_ref, s1_ref, s2_ref,
                   q0_ref, q1_ref, q2_ref, kv0_ref, kv1_ref, kv2_ref, u_ref, *kvt_refs, n_seq, after_first_group=None):
    seq = pl.program_id(1)
    xn = _rms_rows(x_ref[...], g_ref[...]).astype(BF16)
    ones_head = _head_ones()
    cosw, sin1, sin2 = c_ref[...], s1_ref[...], s2_ref[...]
    qn, kn = qn_ref[...], kn_ref[...]
    half = ROT_DIM // 2

    def rope(z):
        return z * cosw + pltpu.roll(z, GROUP_W - half, 1) * sin1 + pltpu.roll(z, half, 1) * sin2

    q_refs = (q0_ref, q1_ref, q2_ref)
    kv_refs = (kv0_ref, kv1_ref, kv2_ref)
    proj = lambda start: _dot(xn, w_ref[:, start:start + GROUP_W])
    for g in range(N_GROUPS):
        zq, zk, zv = (proj(part * ATTN_WIDTH + g * GROUP_W) for part in range(3))
        _store_halves(q_refs[g], 0, rope(zq * _head_inv_rms(zq, ones_head) * qn) * (HEAD_DIM ** -0.5))
        k = rope(zk * _head_inv_rms(zk, ones_head) * kn)
        _store_halves(kv_refs[g], 0, k)
        _store_halves(kv_refs[g], LANE_SPLIT, zv)
        if kvt_refs and g == N_GROUPS - 1:
            kvt_refs[0][0] = k.T
            kvt_refs[0][1] = zv.T
        if g == 0 and after_first_group is not None:
            after_first_group()
    tm = x_ref.shape[0]
    u_rows = pl.ds(seq, tm, stride=n_seq) if n_seq > 1 else slice(None)
    _store_halves(u_ref, 0, proj(3 * ATTN_WIDTH), u_rows)


def _in_proj(x2d, gain, w_bf, qn, kn, cos_t, sin1_t, sin2_t, tm, feature_major_last, sample_attn=None):
    n = x2d.shape[0]
    seq_rows = cos_t.shape[0]
    n_pos, n_seq = seq_rows // tm, n // seq_rows
    row = lambda c, b: (b * n_pos + c, 0)
    row3 = lambda c, b: (0, b * n_pos + c, 0)
    fixed = lambda c, b: (0, 0)
    pos = lambda c, b: (c, 0)
    sds = jax.ShapeDtypeStruct
    in_specs = [pl.BlockSpec((tm, D_MODEL), row), pl.BlockSpec((1, D_MODEL), fixed),
                pl.BlockSpec((D_MODEL, IN_WIDTH), fixed),
                pl.BlockSpec((1, GROUP_W), fixed), pl.BlockSpec((1, GROUP_W), fixed),
                pl.BlockSpec((tm, GROUP_W), pos), pl.BlockSpec((tm, GROUP_W), pos), pl.BlockSpec((tm, GROUP_W), pos)]
    out_specs = ([_halves_spec(LANE_SPLIT, tm, row3)] * 3 + [_halves_spec(2 * LANE_SPLIT, tm, row3)] * 3
                 + [_halves_spec(LANE_SPLIT, tm * n_seq, lambda c, b: (0, c, 0))])
    out_shape = ([sds((LANE_SPLIT, n, LANES), F32)] * 3 + [sds((2 * LANE_SPLIT, n, LANES), F32)] * 3
                 + [sds((LANE_SPLIT, n, LANES), F32)])
    if feature_major_last:
        out_specs.append(pl.BlockSpec((None, 2, GROUP_W, tm), lambda c, b: (b, 0, 0, c)))
        out_shape.append(sds((n_seq, 2, GROUP_W, seq_rows), F32))
    args = [x2d, gain, w_bf, qn, kn, cos_t, sin1_t, sin2_t]
    scratch = []
    if sample_attn is None:
        body, name = functools.partial(_inproj_kernel, n_seq=n_seq), "in_proj"
    else:
        assert feature_major_last
        q, kvn, cache, batch, t_n, dil = sample_attn
        steps = n_pos * n_seq
        n_buf = cache.shape[1]
        assert n_buf == ATTN_BLK * dil and t_n <= dil and batch % steps == 0
        bb_n = batch // steps
        ct = jnp.transpose(cache, (0, 2, 3, 4, 1)).reshape(batch, 2, GROUP_W, n_buf)
        blk = lambda c, b: (0, c * n_seq + b, 0)
        in_specs += [_halves_spec(LANE_SPLIT, bb_n * t_n, blk), _halves_spec(2 * LANE_SPLIT, bb_n * t_n, blk),
                     pl.BlockSpec((bb_n, 2, GROUP_W, n_buf), lambda c, b: (c * n_seq + b, 0, 0, 0))]
        out_specs += [_halves_spec(LANE_SPLIT, bb_n * t_n, blk)] * 2
        out_shape += [sds((LANE_SPLIT, batch * t_n, LANES), F32)] * 2
        scratch = [pltpu.VMEM((bb_n, t_n * HEADS_PER_GROUP, GROUP_W), F32)] * 3
        args += [q, kvn, ct]
        body = functools.partial(_inproj_sattn_kernel, n_seq=n_seq, dil=dil, bb_n=bb_n, t_n=t_n)
        name = f"in_proj_sample_attn_d{dil}"
    return pl.pallas_call(
        body,
        grid=(n_pos, n_seq),
        in_specs=in_specs,
        out_specs=out_specs,
        out_shape=out_shape,
        scratch_shapes=scratch,
        compiler_params=_cparams(("arbitrary",) * 2),
        name=name,
    )(*args)


def _pattn_kernel(*refs, dil, n_sub, has_prev):
    if has_prev:
        q_ref, kvc_ref, kvp_ref, o_ref, lse_ref = refs
    else:
        q_ref, kvc_ref, o_ref, lse_ref = refs
    n = pl.program_id(1)
    blk = ATTN_BLK
    span = blk * dil
    head = lax.broadcasted_iota(jnp.int32, (1, GROUP_W), 1) // HEAD_DIM
    nk = 2 * blk if has_prev else blk
    qi = lax.broadcasted_iota(jnp.int32, (HEADS_PER_GROUP * blk, nk), 0) % blk
    ki = lax.broadcasted_iota(jnp.int32, (HEADS_PER_GROUP * blk, nk), 1)
    band = ((ki >= qi) & (ki <= qi + blk)) if has_prev else (ki <= qi)
    band_first = (band & ((n > 0) | (ki >= blk))) if has_prev else band

    def sub(j, r):
        rows = pl.ds(j * span + r, blk, stride=dil) if dil > 1 else pl.ds(j * span, blk)
        q = _load_halves(q_ref, 0, rows)
        kc, vc = _load_halves(kvc_ref, 0, rows), _load_halves(kvc_ref, LANE_SPLIT, rows)
        if has_prev:
            if j > 0:
                prow = pl.ds((j - 1) * span + r, blk, stride=dil) if dil > 1 else pl.ds((j - 1) * span, blk)
                kp, vp = _load_halves(kvc_ref, 0, prow), _load_halves(kvc_ref, LANE_SPLIT, prow)
                mask = band
            else:
                prow = pl.ds(r, blk, stride=dil) if dil > 1 else pl.ds(0, blk)
                kp, vp = _load_halves(kvp_ref, 0, prow), _load_halves(kvp_ref, LANE_SPLIT, prow)
                mask = band_first
            k = jnp.concatenate([kp, kc], axis=0).astype(BF16)
            v = jnp.concatenate([vp, vc], axis=0).astype(BF16)
        else:
            k, v, mask = kc.astype(BF16), vc.astype(BF16), band
        qs = jnp.concatenate([jnp.where(head == h, q, 0.0) for h in range(HEADS_PER_GROUP)], axis=0).astype(BF16)
        return rows, qs, k, v, mask

    def pick(x):
        out = x[(HEADS_PER_GROUP - 1) * blk:]
        for h in range(HEADS_PER_GROUP - 2, -1, -1):
            out = jnp.where(head == h, x[h * blk:(h + 1) * blk], out)
        return out

    def run(subs):
        loaded = [sub(j, r) for j, r in subs]
        scores = [jnp.where(mask, _dot_t(qs, k), NEG) for _, qs, k, _, mask in loaded]
        stats = []
        for s in scores:
            m = jnp.max(s, axis=-1, keepdims=True)
            p = jnp.exp(s - m)
            stats.append((m, jnp.sum(p, axis=-1, keepdims=True), p.astype(BF16)))
        outs = [_dot(p, v) for (_, _, p), (_, _, _, v, _) in zip(stats, loaded)]
        for (rows, _, _, _, _), (m, l, _), pv in zip(loaded, stats, outs):
            l_c = pick(l)
            _store_halves(o_ref, 0, pick(pv) / l_c, rows)
            _store_halves(lse_ref, 0, pick(m) + jnp.log(l_c), rows)

    if dil * n_sub <= 4:
        run([(j, r) for j in range(n_sub) for r in range(dil)])
    else:
        per_trip = 4
        for j in range(n_sub):
            def body(it, carry, j=j):
                run([(j, it * per_trip + u) for u in range(per_trip)])
                return carry

            lax.fori_loop(0, dil // per_trip, body, 0)


def _prompt_attn(q, kv, batch, seq, dil):
    span = ATTN_BLK * dil
    has_prev = seq > span
    n_sub = max(1, 512 // span) if has_prev else 1
    rows = n_sub * span
    steps = seq // rows
    cur = lambda b, n: (0, b * steps + n, 0)
    in_specs = [_halves_spec(LANE_SPLIT, rows, cur), _halves_spec(2 * LANE_SPLIT, rows, cur)]
    args = [q, kv]
    if has_prev:
        in_specs.append(_halves_spec(2 * LANE_SPLIT, span,
                                     lambda b, n: (0, b * (seq // span) + jnp.maximum(n * n_sub - 1, 0), 0)))
        args.append(kv)
    return pl.pallas_call(
        functools.partial(_pattn_kernel, dil=dil, n_sub=n_sub, has_prev=has_prev),
        grid=(batch, steps),
        in_specs=in_specs,
        out_specs=[_halves_spec(LANE_SPLIT, rows, cur)] * 2,
        out_shape=[jax.ShapeDtypeStruct((LANE_SPLIT, batch * seq, LANES), F32)] * 2,
        compiler_params=_cparams(("arbitrary",) * 2),
        name=f"prompt_attn_d{dil}",
    )(*args)


def _sattn_kernel(q_ref, kvn_ref, ct_ref, o_ref, lse_ref, qm, osel, lsel, *, dil, bb_n, t_n):
    _sattn_finish(_sattn_scores(q_ref, ct_ref, qm, dil, bb_n, t_n), kvn_ref, ct_ref, o_ref, lse_ref, osel, lsel)


def _sattn_scores(q_ref, ct_ref, qm, dil, bb_n, t_n):
    n_buf = ct_ref.shape[-1]
    hp = HEADS_PER_GROUP
    head = lax.broadcasted_iota(jnp.int32, (1, GROUP_W), 1) // HEAD_DIM
    sel = head == lax.broadcasted_iota(jnp.int32, (hp, 1), 0)
    row = lax.broadcasted_iota(jnp.int32, (t_n * hp, 1), 0)
    trow = row // hp
    sel_rows = head == row % hp
    pos = lax.broadcasted_iota(jnp.int32, (1, n_buf), 1)
    cmask = (pos >= trow) if dil == 1 else ((pos & (dil - 1)) == trow)
    qmbs, scores = [], []
    for bb in range(bb_n):
        for t in range(t_n):
            r = bb * t_n + t
            qm[bb, t * hp:(t + 1) * hp, :] = jnp.where(sel, _load_halves(q_ref, 0, slice(r, r + 1)), 0.0)
        qmbs.append(qm[bb].astype(BF16))
        scores.append(jnp.where(cmask, _dot(qmbs[bb], ct_ref[bb, 0].astype(BF16)), NEG))
    return dict(qmbs=qmbs, scores=scores, trow=trow, sel_rows=sel_rows, dil=dil, bb_n=bb_n, t_n=t_n)


def _sattn_finish(ctx, kvn_ref, ct_ref, o_ref, lse_ref, osel, lsel):
    qmbs, scores, trow, sel_rows = ctx['qmbs'], ctx['scores'], ctx['trow'], ctx['sel_rows']
    dil, bb_n, t_n, hp = ctx['dil'], ctx['bb_n'], ctx['t_n'], HEADS_PER_GROUP
    stats = []
    for bb in range(bb_n):
        qmr = qmbs[bb].astype(F32)
        m = jnp.max(scores[bb], axis=-1, keepdims=True)
        s_n = []
        for u in range(t_n):
            r = bb * t_n + u
            kn = _load_halves(kvn_ref, 0, slice(r, r + 1)).astype(BF16).astype(F32)
            valid = (trow >= u) if dil == 1 else (trow == u)
            sn = jnp.where(valid, jnp.sum(qmr * kn, axis=-1, keepdims=True), NEG)
            s_n.append(sn)
            m = jnp.maximum(m, sn)
        p_c = jnp.exp(scores[bb] - m)
        stats.append((m, s_n, jnp.sum(p_c, axis=-1, keepdims=True), p_c.astype(BF16)))
    accs = [_dot_t(stats[bb][3], ct_ref[bb, 1].astype(BF16)) for bb in range(bb_n)]
    for bb in range(bb_n):
        m, s_n, l, _ = stats[bb]
        acc = accs[bb]
        for u in range(t_n):
            p_n = jnp.exp(s_n[u] - m)
            l = l + p_n
            r = bb * t_n + u
            vn = _load_halves(kvn_ref, LANE_SPLIT, slice(r, r + 1)).astype(BF16).astype(F32)
            acc = acc + p_n.astype(BF16).astype(F32) * vn
        osel[bb] = jnp.where(sel_rows, acc / l, 0.0)
        lsel[bb] = jnp.where(sel_rows, m + jnp.log(l), 0.0)
        for t in range(t_n):
            r = slice(bb * t_n + t, bb * t_n + t + 1)
            _store_halves(o_ref, 0, jnp.sum(osel[bb, t * hp:(t + 1) * hp, :], axis=0, keepdims=True), r)
            _store_halves(lse_ref, 0, jnp.sum(lsel[bb, t * hp:(t + 1) * hp, :], axis=0, keepdims=True), r)


def _sample_attn(q, kvn, cache, batch, t_n, dil, bb_n):
    n_buf = cache.shape[1]
    assert n_buf == ATTN_BLK * dil and (dil == 1 or t_n <= dil)
    ct = jnp.transpose(cache, (0, 2, 3, 4, 1)).reshape(batch, 2, GROUP_W, n_buf)
    rows_n = t_n * HEADS_PER_GROUP
    blk = lambda i: (0, i, 0)
    return pl.pallas_call(
        functools.partial(_sattn_kernel, dil=dil, bb_n=bb_n, t_n=t_n),
        grid=(batch // bb_n,),
        in_specs=[_halves_spec(LANE_SPLIT, bb_n * t_n, blk), _halves_spec(2 * LANE_SPLIT, bb_n * t_n, blk),
                  pl.BlockSpec((bb_n, 2, GROUP_W, n_buf), lambda i: (i, 0, 0, 0))],
        out_specs=[_halves_spec(LANE_SPLIT, bb_n * t_n, blk)] * 2,
        out_shape=[jax.ShapeDtypeStruct((LANE_SPLIT, batch * t_n, LANES), F32)] * 2,
        scratch_shapes=[pltpu.VMEM((bb_n, rows_n, GROUP_W), F32)] * 3,
        compiler_params=_cparams(("arbitrary",)),
        name=f"sample_attn_d{dil}",
    )(q, kvn, ct)


def _ssm_sattn_kernel(u_ref, s0r_ref, s0i_ref, lre_ref, lim_ref, ldt_ref, bre_ref, bim_ref, cre_ref, cim_ref, d_ref,
                      sq_ref, skvn_ref, ct_ref, y_ref, fr_ref, fi_ref, so_ref, slse_ref,
                      hre, him, str_, sti, qm, osel, lsel, *, n_seq, steps, dil, bb_n, t_n):
    ctx = _sattn_scores(sq_ref, ct_ref, qm, dil, bb_n, t_n)
    _ssm_kernel(u_ref, s0r_ref, s0i_ref, lre_ref, lim_ref, ldt_ref, bre_ref, bim_ref, cre_ref, cim_ref, d_ref,
                y_ref, fr_ref, fi_ref, hre, him, str_, sti, n_seq=n_seq, steps=steps,
                before_scan=lambda: _sattn_finish(ctx, skvn_ref, ct_ref, so_ref, slse_ref, osel, lsel))


def _ssm_kernel(u_ref, s0r_ref, s0i_ref, lre_ref, lim_ref, ldt_ref, bre_ref, bim_ref, cre_ref, cim_ref, d_ref,
                y_ref, fr_ref, fi_ref, hre, him, str_, sti, *, n_seq, steps, before_scan=None):
    i = pl.program_id(0)

    @pl.when(i == 0)
    def _():
        str_[...] = s0r_ref[...]
        sti[...] = s0i_ref[...]

    lam_re, lam_im = lre_ref[...], lim_ref[...]
    dt = jnp.exp(ldt_ref[...])
    mag = jnp.exp(lam_re * dt)
    a_re = mag * jnp.cos(lam_im * dt)
    a_im = mag * jnp.sin(lam_im * dt)
    den = lam_re * lam_re + lam_im * lam_im
    nr, ni = a_re - 1.0, a_im
    c_re = (nr * lam_re + ni * lam_im) / den
    c_im = (ni * lam_re - nr * lam_im) / den

    u = _load_halves(u_ref, 0, slice(None))
    ub = u.astype(BF16)
    bu_re = _dot(ub, bre_ref[...])
    bu_im = _dot(ub, bim_ref[...])
    hre[...] = c_re * bu_re - c_im * bu_im
    him[...] = c_re * bu_im + c_im * bu_re
    if before_scan is not None:
        before_scan()

    if n_seq <= 8:
        def step(t, carry):
            sr, si = carry
            rows = pl.ds(pl.multiple_of(t * n_seq, n_seq), n_seq)
            nr_ = a_re * sr - a_im * si + hre[rows, :]
            ni_ = a_re * si + a_im * sr + him[rows, :]
            hre[rows, :] = nr_
            him[rows, :] = ni_
            return nr_, ni_

        sr, si = lax.fori_loop(0, steps, step, (str_[...], sti[...]))
        str_[...] = sr
        sti[...] = si
    else:
        for t in range(steps):
            rows = slice(t * n_seq, (t + 1) * n_seq)
            sr, si = str_[...], sti[...]
            nr_ = a_re * sr - a_im * si + hre[rows, :]
            ni_ = a_re * si + a_im * sr + him[rows, :]
            hre[rows, :] = nr_
            him[rows, :] = ni_
            str_[...] = nr_
            sti[...] = ni_

    y = _dot(hre[...].astype(BF16), cre_ref[...]) - _dot(him[...].astype(BF16), cim_ref[...])
    _store_halves(y_ref, 0, y + d_ref[...] * u)
    fr_ref[...] = str_[...]
    fi_ref[...] = sti[...]


def _ssm(u_tm, s0r, s0i, lam_re, lam_im, log_dt, b_re, b_im, c_re, c_im, dvec, n_seq, n_steps, chunk,
         sample_attn=None):
    rows = chunk * n_seq
    steps = n_steps // chunk
    fixed = lambda i: (0, 0)
    blk = lambda i: (0, i, 0)
    sds = jax.ShapeDtypeStruct
    in_specs = [_halves_spec(LANE_SPLIT, rows, blk),
                pl.BlockSpec((n_seq, SSM_LANES), fixed), pl.BlockSpec((n_seq, SSM_LANES), fixed),
                pl.BlockSpec((1, SSM_LANES), fixed), pl.BlockSpec((1, SSM_LANES), fixed),
                pl.BlockSpec((1, SSM_LANES), fixed),
                pl.BlockSpec((SSM_WIDTH, SSM_LANES), fixed), pl.BlockSpec((SSM_WIDTH, SSM_LANES), fixed),
                pl.BlockSpec((SSM_LANES, SSM_WIDTH), fixed), pl.BlockSpec((SSM_LANES, SSM_WIDTH), fixed),
                pl.BlockSpec((1, SSM_WIDTH), fixed)]
    out_specs = [_halves_spec(LANE_SPLIT, rows, blk),
                 pl.BlockSpec((n_seq, SSM_LANES), fixed), pl.BlockSpec((n_seq, SSM_LANES), fixed)]
    out_shape = [sds((LANE_SPLIT, n_steps * n_seq, LANES), F32), sds((n_seq, SSM_LANES), F32),
                 sds((n_seq, SSM_LANES), F32)]
    scratch = [pltpu.VMEM((rows, SSM_LANES), F32), pltpu.VMEM((rows, SSM_LANES), F32),
               pltpu.VMEM((n_seq, SSM_LANES), F32), pltpu.VMEM((n_seq, SSM_LANES), F32)]
    args = [u_tm, s0r, s0i, lam_re, lam_im, log_dt, b_re, b_im, c_re, c_im, dvec]
    if sample_attn is None:
        body = functools.partial(_ssm_kernel, n_seq=n_seq, steps=chunk)
        name = f"ssm_scan_{n_seq}"
    else:
        q, kvn, cache, batch, t_n, dil = sample_attn
        n_buf = cache.shape[1]
        assert n_buf == ATTN_BLK * dil and t_n <= dil and batch % steps == 0
        bb_n = batch // steps
        ct = jnp.transpose(cache, (0, 2, 3, 4, 1)).reshape(batch, 2, GROUP_W, n_buf)
        in_specs += [_halves_spec(LANE_SPLIT, bb_n * t_n, blk), _halves_spec(2 * LANE_SPLIT, bb_n * t_n, blk),
                     pl.BlockSpec((bb_n, 2, GROUP_W, n_buf), lambda i: (i, 0, 0, 0))]
        out_specs += [_halves_spec(LANE_SPLIT, bb_n * t_n, blk)] * 2
        out_shape += [sds((LANE_SPLIT, batch * t_n, LANES), F32)] * 2
        scratch += [pltpu.VMEM((bb_n, t_n * HEADS_PER_GROUP, GROUP_W), F32)] * 3
        args += [q, kvn, ct]
        body = functools.partial(_ssm_sattn_kernel, n_seq=n_seq, steps=chunk, dil=dil, bb_n=bb_n, t_n=t_n)
        name = f"ssm_scan_{n_seq}_sample_attn_d{dil}"
    return pl.pallas_call(
        body,
        grid=(steps,),
        in_specs=in_specs,
        out_specs=out_specs,
        out_shape=out_shape,
        scratch_shapes=scratch,
        compiler_params=_cparams(("arbitrary",)),
        name=name,
    )(*args)


def _mix_pre(o_refs, l_refs, y, ga_ref, wglu_ref, rows):
    zs = 0.5 * y * (1.0 + jnp.tanh(math.sqrt(2.0 / math.pi) * (y + 0.044715 * (y * y * y))))
    glu = _dot(zs.astype(BF16), wglu_ref[...])
    l0, l1, l2 = (_load_halves(r, 0, rows) for r in l_refs)
    m = jnp.maximum(jnp.maximum(l0, l1), l2)
    e0, e1, e2 = jnp.exp(l0 - m), jnp.exp(l1 - m), jnp.exp(l2 - m)
    den = e0 + e1 + e2
    a = [_load_halves(r, 0, rows) * (e / den) for r, e in zip(o_refs, (e0, e1, e2))]
    ssq = sum(jnp.sum(ai * ai, axis=-1, keepdims=True) for ai in a)
    inv = lax.rsqrt(ssq * (1.0 / ATTN_WIDTH) + EPS)
    an = [(a[g] * inv * ga_ref[:, g * GROUP_W:(g + 1) * GROUP_W]).astype(BF16) for g in range(N_GROUPS)]
    return zs, glu, an


def _mix_post(x, zs, glu, an, gs_ref, bglu_ref, wout_ref):
    acc = x
    for g in range(N_GROUPS):
        acc = acc + _dot(an[g], wout_ref[g * GROUP_W:(g + 1) * GROUP_W, :])
    gate = jax.nn.sigmoid(glu + bglu_ref[...])
    so = _rms_rows(zs * gate, gs_ref[...]).astype(BF16)
    return acc + _dot(so, wout_ref[ATTN_WIDTH:, :])


def _mixout_value(o_refs, l_refs, y, x_ref, ga_ref, gs_ref, wglu_ref, bglu_ref, wout_ref):
    zs, glu, an = _mix_pre(o_refs, l_refs, y, ga_ref, wglu_ref, slice(None))
    return _mix_post(x_ref[...], zs, glu, an, gs_ref, bglu_ref, wout_ref)


def _mixout_kernel(o0_ref, o1_ref, o2_ref, l0_ref, l1_ref, l2_ref, y_ref, x_ref,
                   ga_ref, gs_ref, wglu_ref, bglu_ref, wout_ref, h_ref):
    h_ref[...] = _mixout_value((o0_ref, o1_ref, o2_ref), (l0_ref, l1_ref, l2_ref),
                               _load_halves(y_ref, 0, slice(None)), x_ref,
                               ga_ref, gs_ref, wglu_ref, bglu_ref, wout_ref)


def _mix_out(os_, ls_, y, x2d, g_attn, g_ssm, wglu_bf, bglu, wout_bf, tm):
    n = x2d.shape[0]
    row = lambda i: (i, 0)
    fixed = lambda i: (0, 0)
    return pl.pallas_call(
        _mixout_kernel,
        grid=(n // tm,),
        in_specs=[_halves_spec(LANE_SPLIT, tm, lambda i: (0, i, 0))] * 7 + [pl.BlockSpec((tm, D_MODEL), row)]
                 + [pl.BlockSpec((1, ATTN_WIDTH), fixed), pl.BlockSpec((1, SSM_WIDTH), fixed),
                    pl.BlockSpec((SSM_WIDTH, SSM_WIDTH), fixed), pl.BlockSpec((1, SSM_WIDTH), fixed),
                    pl.BlockSpec((D_MODEL, D_MODEL), fixed)],
        out_specs=pl.BlockSpec((tm, D_MODEL), row),
        out_shape=jax.ShapeDtypeStruct((n, D_MODEL), F32),
        compiler_params=_cparams(("arbitrary",)),
        name="mix_out",
    )(*os_, *ls_, y, x2d, g_attn, g_ssm, wglu_bf, bglu, wout_bf)


def _memkv_kernel(x_ref, g_ref, w_ref, kn_ref, o_ref, rows_ref):
    xn = _rms_rows(x_ref[...], g_ref[...]).astype(BF16)
    kn = kn_ref[...]
    for kv in range(2):
        for h in range(X_HEADS):
            cols = slice(kv * D_MODEL + h * X_HEAD_DIM, kv * D_MODEL + (h + 1) * X_HEAD_DIM)
            z = _dot(xn, w_ref[:, cols])
            if kv == 0:
                z = _rms_rows(z, kn)
            o_ref[:, cols] = z
            for half in range(MEM_HALVES):
                first = (kv * MEM_HALVES + half) * X_HEADS + h
                rows_ref[pl.ds(first, N_MEM, stride=MEM_ROW_STRIDE), :] = z[:, half * LANES:(half + 1) * LANES]


def _memory_kv(mem2d, gain, w_bf, kn, n_seq):
    fixed = lambda i: (0, 0)
    sds = jax.ShapeDtypeStruct
    return pl.pallas_call(
        _memkv_kernel,
        grid=(n_seq,),
        in_specs=[pl.BlockSpec((N_MEM, D_MODEL), lambda i: (i, 0)), pl.BlockSpec((1, D_MODEL), fixed),
                  pl.BlockSpec((D_MODEL, 2 * D_MODEL), fixed), pl.BlockSpec((1, X_HEAD_DIM), fixed)],
        out_specs=[pl.BlockSpec((N_MEM, 2 * D_MODEL), lambda i: (i, 0)),
                   pl.BlockSpec((None, N_MEM * MEM_ROW_STRIDE, LANES), lambda i: (i, 0, 0))],
        out_shape=[sds((n_seq * N_MEM, 2 * D_MODEL), F32), sds((n_seq, N_MEM * MEM_ROW_STRIDE, LANES), F32)],
        compiler_params=_cparams(("arbitrary",)),
        name="memory_kv",
    )(mem2d, gain, w_bf, kn)


def _mix_cross_kernel(o0_ref, o1_ref, o2_ref, l0_ref, l1_ref, l2_ref, y_ref, x_ref,
                      ga_ref, gs_ref, wglu_ref, bglu_ref, wout_ref,
                      mkv_ref, gx_ref, wq_ref, qn_ref, wo_ref, o_ref, att, *, n_seq, blocks_per_seq):
    tm = x_ref.shape[0]
    seq = pl.program_id(0) // blocks_per_seq
    h = _mixout_value((o0_ref, o1_ref, o2_ref), (l0_ref, l1_ref, l2_ref),
                      _load_halves(y_ref, 0, pl.ds(seq, tm, stride=n_seq)), x_ref,
                      ga_ref, gs_ref, wglu_ref, bglu_ref, wout_ref)
    hn = _rms_rows(h, gx_ref[...]).astype(BF16)
    qn = qn_ref[...]
    heads = [slice(hd * X_HEAD_DIM, (hd + 1) * X_HEAD_DIM) for hd in range(X_HEADS)]
    qz = [_dot(hn, wq_ref[:, cols]) for cols in heads]
    qs = [(_rms_rows(z, qn) * (X_HEAD_DIM ** -0.5)).astype(BF16) for z in qz]
    scores = [_dot_t(q, mkv_ref[:, cols].astype(BF16)) for q, cols in zip(qs, heads)]
    probs = []
    for s in scores:
        p = jnp.exp(s - jnp.max(s, axis=-1, keepdims=True))
        probs.append((p.astype(BF16), jnp.sum(p, axis=-1, keepdims=True)))
    outs = [_dot(p, mkv_ref[:, D_MODEL + cols.start:D_MODEL + cols.stop].astype(BF16)) for (p, _), cols in zip(probs, heads)]
    for pv, (_, l), cols in zip(outs, probs, heads):
        att[:, cols] = (pv / l).astype(BF16)
    o_ref[...] = h + _dot(att[...], wo_ref[...])


def _mix_cross(os_, ls_, y, x2d, mkv, p, rows_per_seq, tm):
    n = x2d.shape[0]
    blocks_per_seq = rows_per_seq // tm
    n_seq = n // rows_per_seq
    row = lambda i: (i, 0)
    fixed = lambda i: (0, 0)
    return pl.pallas_call(
        functools.partial(_mix_cross_kernel, n_seq=n_seq, blocks_per_seq=blocks_per_seq),
        grid=(n // tm,),
        in_specs=[_halves_spec(LANE_SPLIT, tm, lambda i: (0, i, 0))] * 6
                 + [_halves_spec(LANE_SPLIT, tm * n_seq, lambda i: (0, i % blocks_per_seq, 0)),
                    pl.BlockSpec((tm, D_MODEL), row)]
                 + [pl.BlockSpec((1, ATTN_WIDTH), fixed), pl.BlockSpec((1, SSM_WIDTH), fixed),
                    pl.BlockSpec((SSM_WIDTH, SSM_WIDTH), fixed), pl.BlockSpec((1, SSM_WIDTH), fixed),
                    pl.BlockSpec((D_MODEL, D_MODEL), fixed)]
                 + [pl.BlockSpec((None, N_MEM, 2 * D_MODEL), lambda i: (i // blocks_per_seq, 0, 0)),
                    pl.BlockSpec((1, D_MODEL), fixed), pl.BlockSpec((D_MODEL, D_MODEL), fixed),
                    pl.BlockSpec((1, X_HEAD_DIM), fixed), pl.BlockSpec((D_MODEL, D_MODEL), fixed)],
        out_specs=pl.BlockSpec((tm, D_MODEL), row),
        out_shape=jax.ShapeDtypeStruct((n, D_MODEL), F32),
        scratch_shapes=[pltpu.VMEM((tm, D_MODEL), BF16)],
        compiler_params=_cparams(("arbitrary",)),
        name="mix_cross",
    )(*os_, *ls_, y, x2d, p['attn_out_norm'], p['ssm_out_norm'], p['w_glu'], p['b_glu'], p['w_out'],
      mkv, p['norm_x'], p['w_xq'], p['xq_norm'], p['w_xo'])


MEM_HALVES = X_HEAD_DIM // 128
MEM_ROW_STRIDE = 2 * MEM_HALVES * X_HEADS


def _cross_sample_kernel(h_ref, mem_ref, g_ref, wq_ref, qn_ref, wo_ref, o_ref, qsc, qm, osel, att, *, bb_n, t_n):
    h = h_ref[...]
    hn = _rms_rows(h, g_ref[...]).astype(BF16)
    qn = qn_ref[...]
    for hd in range(X_HEADS):
        cols = slice(hd * X_HEAD_DIM, (hd + 1) * X_HEAD_DIM)
        qsc[:, cols] = _rms_rows(_dot(hn, wq_ref[:, cols]), qn) * (X_HEAD_DIM ** -0.5)
    xh = X_HEADS
    head = lax.broadcasted_iota(jnp.int32, (1, D_MODEL), 1) // X_HEAD_DIM
    sel = head == lax.broadcasted_iota(jnp.int32, (xh, 1), 0)
    sel_rows = head == lax.broadcasted_iota(jnp.int32, (t_n * xh, 1), 0) % xh

    def gather(bb, kv):
        parts = [mem_ref[bb, pl.ds((kv * MEM_HALVES + half) * xh + hd, N_MEM, stride=MEM_ROW_STRIDE), :]
                 for hd in range(xh) for half in range(MEM_HALVES)]
        return jnp.concatenate(parts, axis=1).astype(BF16)

    scores = []
    for bb in range(bb_n):
        for t in range(t_n):
            r = bb * t_n + t
            qm[bb, t * xh:(t + 1) * xh, :] = jnp.where(sel, qsc[r:r + 1, :], 0.0)
        scores.append(_dot_t(qm[bb].astype(BF16), gather(bb, 0)))
    probs = []
    for s in scores:
        p = jnp.exp(s - jnp.max(s, axis=-1, keepdims=True))
        probs.append((p.astype(BF16), jnp.sum(p, axis=-1, keepdims=True)))
    outs = [_dot(probs[bb][0], gather(bb, 1)) for bb in range(bb_n)]
    for bb in range(bb_n):
        osel[bb] = jnp.where(sel_rows, outs[bb] / probs[bb][1], 0.0)
        for t in range(t_n):
            r = bb * t_n + t
            att[r:r + 1, :] = jnp.sum(osel[bb, t * xh:(t + 1) * xh, :], axis=0, keepdims=True)
    o_ref[...] = h + _dot(att[...].astype(BF16), wo_ref[...])


def _cross_sample(h2d, mem_kv, gain, wq_bf, qn, wo_bf, batch, t_n, bb_n):
    mem = mem_kv.reshape(batch, N_MEM, 2, X_HEADS, MEM_HALVES, 128)
    mem = jnp.transpose(mem, (0, 1, 2, 4, 3, 5)).reshape(batch, N_MEM * MEM_ROW_STRIDE, 128)
    rows = bb_n * t_n
    fixed = lambda i: (0, 0)
    return pl.pallas_call(
        functools.partial(_cross_sample_kernel, bb_n=bb_n, t_n=t_n),
        grid=(batch // bb_n,),
        in_specs=[pl.BlockSpec((rows, D_MODEL), lambda i: (i, 0)),
                  pl.BlockSpec((bb_n, N_MEM * MEM_ROW_STRIDE, 128), lambda i: (i, 0, 0)),
                  pl.BlockSpec((1, D_MODEL), fixed), pl.BlockSpec((D_MODEL, D_MODEL), fixed),
                  pl.BlockSpec((1, X_HEAD_DIM), fixed), pl.BlockSpec((D_MODEL, D_MODEL), fixed)],
        out_specs=pl.BlockSpec((rows, D_MODEL), lambda i: (i, 0)),
        out_shape=jax.ShapeDtypeStruct((batch * t_n, D_MODEL), F32),
        scratch_shapes=[pltpu.VMEM((rows, D_MODEL), F32), pltpu.VMEM((bb_n, t_n * X_HEADS, D_MODEL), F32),
                        pltpu.VMEM((bb_n, t_n * X_HEADS, D_MODEL), F32), pltpu.VMEM((rows, D_MODEL), F32)],
        compiler_params=_cparams(("arbitrary",)),
        name="cross_attn_sample",
    )(h2d, mem, gain, wq_bf, qn, wo_bf)


FFN_CHUNKS = 2
FFN_TF = D_FF // FFN_CHUNKS
CARRY_ROWS = 8


def _conv_gate(up_g, up_v, prev_g, prev_v, bufg, bufv, cwg_ref, cwv_ref, cbg_ref, cbv_ref, tm, lag):
    hist = prev_g.shape[0]
    bufg[0:hist] = prev_g
    bufg[hist:hist + tm] = up_g
    bufv[0:hist] = prev_v
    bufv[hist:hist + tm] = up_v
    cg = cbg_ref[...] + (cwg_ref[0:1, :] * bufg[hist - 2 * lag:hist - 2 * lag + tm]
                         + cwg_ref[1:2, :] * bufg[hist - lag:hist - lag + tm] + cwg_ref[2:3, :] * up_g)
    cv = cbv_ref[...] + (cwv_ref[0:1, :] * bufv[hist - 2 * lag:hist - 2 * lag + tm]
                         + cwv_ref[1:2, :] * bufv[hist - lag:hist - lag + tm] + cwv_ref[2:3, :] * up_v)
    return (cg * jax.nn.sigmoid(cg) * cv).astype(BF16)


FFN_COLS = 256
FFN_NCHUNK = D_FF // FFN_COLS


def _ffn_prompt_kernel(h_ref, g_ref, wup_ref, cw_ref, cb_ref, wd_ref, o_ref, tail_ref, act, carry, *, tm):
    i = pl.program_id(1)
    h = h_ref[...]
    xn = _rms_rows(h, g_ref[...]).astype(BF16)

    @pl.when(i == 0)
    def _():
        carry[...] = jnp.zeros(carry.shape, F32)

    for n in range(FFN_NCHUNK):
        conv = []
        for part in range(2):
            cols = slice(part * D_FF + n * FFN_COLS, part * D_FF + (n + 1) * FFN_COLS)
            up = _dot(xn, wup_ref[:, cols])
            ext = jnp.concatenate([carry[part, n], up], axis=0)
            conv.append(cb_ref[:, cols] + (cw_ref[0:1, cols] * ext[CARRY_ROWS - 2:CARRY_ROWS - 2 + tm]
                                           + cw_ref[1:2, cols] * ext[CARRY_ROWS - 1:CARRY_ROWS - 1 + tm]
                                           + cw_ref[2:3, cols] * up))
            carry[part, n] = up[tm - CARRY_ROWS:]
            tail_ref[:, cols] = up[tm - CARRY_ROWS:]
        cg, cv = conv
        act[:, n * FFN_COLS:(n + 1) * FFN_COLS] = (cg * jax.nn.sigmoid(cg) * cv).astype(BF16)
    split = (FFN_NCHUNK // 2) * FFN_COLS
    o_ref[...] = h + (_dot(act[:, :split], wd_ref[:split, :]) + _dot(act[:, split:], wd_ref[split:, :]))


def _ffn_prompt(h2d, gain, wup_bf, conv_w, conv_b, wdown_bf, batch, seq, tm):
    n_i = seq // tm
    row = lambda b, i: (b * n_i + i, 0)
    fixed = lambda b, i: (0, 0)
    once = pl.Buffered(1)
    sds = jax.ShapeDtypeStruct
    return pl.pallas_call(
        functools.partial(_ffn_prompt_kernel, tm=tm),
        grid=(batch, n_i),
        in_specs=[pl.BlockSpec((tm, D_MODEL), row), pl.BlockSpec((1, D_MODEL), fixed),
                  pl.BlockSpec((D_MODEL, 2 * D_FF), fixed, pipeline_mode=once),
                  pl.BlockSpec((CONV_W, 2 * D_FF), fixed), pl.BlockSpec((1, 2 * D_FF), fixed),
                  pl.BlockSpec((D_FF, D_MODEL), fixed, pipeline_mode=once)],
        out_specs=[pl.BlockSpec((tm, D_MODEL), row),
                   pl.BlockSpec((None, None, CARRY_ROWS, 2 * D_FF), lambda b, i: (b, i, 0, 0))],
        out_shape=[sds((batch * seq, D_MODEL), F32), sds((batch, n_i, CARRY_ROWS, 2 * D_FF), F32)],
        scratch_shapes=[pltpu.VMEM((tm, D_FF), BF16), pltpu.VMEM((2, FFN_NCHUNK, CARRY_ROWS, FFN_COLS), F32)],
        compiler_params=_cparams(("arbitrary",) * 2),
        name="ffn_prompt",
    )(h2d, gain, wup_bf, conv_w, conv_b, wdown_bf)


def _ffn_sample_kernel(h_ref, g_ref, sg_ref, sv_ref, wg_ref, wv_ref, cwg_ref, cwv_ref, cbg_ref, cbv_ref, wd_ref,
                       o_ref, ng_ref, nv_ref, xn, bufg, bufv, *, n_seq, rows):
    c = pl.program_id(0)

    @pl.when(c == 0)
    def _():
        xn[...] = _rms_rows(h_ref[...], g_ref[...]).astype(BF16)

    up_g = _dot(xn[...], wg_ref[...])
    up_v = _dot(xn[...], wv_ref[...])
    act = _conv_gate(up_g, up_v, sg_ref[...], sv_ref[...], bufg, bufv, cwg_ref, cwv_ref, cbg_ref, cbv_ref,
                     rows, n_seq)
    hist = (CONV_W - 1) * n_seq
    ng_ref[...] = bufg[rows:rows + hist]
    nv_ref[...] = bufv[rows:rows + hist]
    contrib = _dot(act, wd_ref[...])

    @pl.when(c == 0)
    def _():
        o_ref[...] = h_ref[...] + contrib

    @pl.when(c > 0)
    def _():
        o_ref[...] += contrib


def _ffn_sample(h_tm, gain, state_tm, wup_bf, conv_w, conv_b, wdown_bf, n_seq, n_steps):
    rows = n_seq * n_steps
    hist = (CONV_W - 1) * n_seq
    nc, tf = FFN_CHUNKS, FFN_TF
    gcol = lambda c: (0, c)
    vcol = lambda c: (0, nc + c)
    fixed = lambda c: (0, 0)
    sds = jax.ShapeDtypeStruct
    return pl.pallas_call(
        functools.partial(_ffn_sample_kernel, n_seq=n_seq, rows=rows),
        grid=(nc,),
        in_specs=[pl.BlockSpec((rows, D_MODEL), fixed), pl.BlockSpec((1, D_MODEL), fixed),
                  pl.BlockSpec((hist, tf), gcol), pl.BlockSpec((hist, tf), vcol),
                  pl.BlockSpec((D_MODEL, tf), gcol), pl.BlockSpec((D_MODEL, tf), vcol),
                  pl.BlockSpec((CONV_W, tf), gcol), pl.BlockSpec((CONV_W, tf), vcol),
                  pl.BlockSpec((1, tf), gcol), pl.BlockSpec((1, tf), vcol),
                  pl.BlockSpec((tf, D_MODEL), lambda c: (c, 0))],
        out_specs=[pl.BlockSpec((rows, D_MODEL), fixed), pl.BlockSpec((hist, tf), gcol),
                   pl.BlockSpec((hist, tf), gcol)],
        out_shape=[sds((rows, D_MODEL), F32), sds((hist, D_FF), F32), sds((hist, D_FF), F32)],
        scratch_shapes=[pltpu.VMEM((rows, D_MODEL), BF16), pltpu.VMEM((rows + hist, tf), F32),
                        pltpu.VMEM((rows + hist, tf), F32)],
        compiler_params=_cparams(("arbitrary",)),
        name="ffn_sample",
    )(h_tm, gain, state_tm, state_tm, wup_bf, wup_bf, conv_w, conv_w, conv_b, conv_b, wdown_bf)


def _rope_tables(pos):
    half = ROT_DIM // 2
    inv = jnp.power(jnp.float32(ROPE_THETA), -jnp.arange(half, dtype=F32) * 2.0 / ROT_DIM)
    ang = pos.astype(F32)[:, None] * inv[None, :]
    cos, sin = jnp.cos(ang), jnp.sin(ang)
    ones = jnp.ones((pos.shape[0], HEAD_DIM - ROT_DIM), F32)
    zeros = jnp.zeros((pos.shape[0], HEAD_DIM - ROT_DIM), F32)
    zh = jnp.zeros_like(sin)
    c_head = jnp.concatenate([cos, cos, ones], axis=1)
    s1_head = jnp.concatenate([-sin, zh, zeros], axis=1)
    s2_head = jnp.concatenate([zh, sin, zeros], axis=1)
    tile = lambda a: jnp.tile(a, (1, HEADS_PER_GROUP))
    return tile(c_head), tile(s1_head), tile(s2_head)


def _block_diag(w):
    g, a, b = w.shape
    eye = jnp.eye(g, dtype=w.dtype)
    return (eye[:, None, :, None] * w[:, :, None, :]).reshape(g * a, g * b)


def _layer_params(l, norm_mix, w_in, q_norm, k_norm, ssm_lam_re, ssm_lam_im, ssm_log_dt, ssm_b_re, ssm_b_im,
                  ssm_c_re, ssm_c_im, ssm_d, w_glu, b_glu, attn_out_norm, ssm_out_norm, w_out, norm_x, norm_mem,
                  w_xq, w_xkv, xq_norm, xk_norm, w_xo, norm_ffn, w_up, conv_w, conv_b, w_down):
    row = lambda a: a.reshape(1, -1).astype(F32)
    return dict(
        norm_mix=row(norm_mix[l]), w_in=w_in[l].astype(BF16),
        q_norm=row(jnp.tile(q_norm[l], HEADS_PER_GROUP)), k_norm=row(jnp.tile(k_norm[l], HEADS_PER_GROUP)),
        lam_re=row(ssm_lam_re[l]), lam_im=row(ssm_lam_im[l]),
        log_dt=row(jnp.repeat(ssm_log_dt[l], SSM_STATE)),
        b_re=_block_diag(jnp.swapaxes(ssm_b_re[l], 1, 2)).astype(BF16),
        b_im=_block_diag(jnp.swapaxes(ssm_b_im[l], 1, 2)).astype(BF16),
        c_re=_block_diag(jnp.swapaxes(ssm_c_re[l], 1, 2)).astype(BF16),
        c_im=_block_diag(jnp.swapaxes(ssm_c_im[l], 1, 2)).astype(BF16),
        ssm_d=row(ssm_d[l]), w_glu=w_glu[l].astype(BF16), b_glu=row(b_glu[l]),
        attn_out_norm=row(attn_out_norm[l]), ssm_out_norm=row(ssm_out_norm[l]), w_out=w_out[l].astype(BF16),
        norm_x=row(norm_x[l]), norm_mem=row(norm_mem[l]), w_xq=w_xq[l].astype(BF16),
        w_xkv=w_xkv[l].astype(BF16), xq_norm=row(xq_norm[l]), xk_norm=row(xk_norm[l]),
        w_xo=w_xo[l].astype(BF16), norm_ffn=row(norm_ffn[l]), w_up=w_up[l].astype(BF16),
        conv_w=conv_w[l].astype(F32), conv_b=row(conv_b[l]), w_down=w_down[l].astype(BF16))


def _kv_rows(kv_halves, n_seq, n_steps):
    heads_per_half = LANES // HEAD_DIM
    a = kv_halves.reshape(2, LANE_SPLIT, n_seq, n_steps, heads_per_half, HEAD_DIM)
    return jnp.transpose(a, (2, 3, 0, 1, 4, 5)).reshape(n_seq, n_steps, 2, HEADS_PER_GROUP, HEAD_DIM)


def _to_time_major(a, n_seq, n_steps):
    return a.reshape(n_seq, n_steps, -1).swapaxes(0, 1).reshape(n_steps * n_seq, -1)


def _to_seq_major(a, n_seq, n_steps):
    return a.reshape(n_steps, n_seq, -1).swapaxes(0, 1).reshape(n_seq * n_steps, -1)


def _mixer_front(x2d, p, tables, tm, feature_major_last=False, sample_attn=None):
    return _in_proj(x2d, p['norm_mix'], p['w_in'], p['q_norm'], p['k_norm'], *tables, tm, feature_major_last,
                    sample_attn)


def _run_ssm(u_tm, s0r, s0i, p, n_seq, n_steps, chunk, sample_attn=None):
    return _ssm(u_tm, s0r, s0i, p['lam_re'], p['lam_im'], p['log_dt'], p['b_re'], p['b_im'],
                p['c_re'], p['c_im'], p['ssm_d'], n_seq, n_steps, chunk, sample_attn)


def _halves_to_time_major(a, n_seq, n_steps):
    return a.reshape(LANE_SPLIT, n_seq, n_steps, LANES).swapaxes(1, 2).reshape(LANE_SPLIT, n_steps * n_seq, LANES)


def _halves_to_seq_major(a, n_seq, n_steps):
    return a.reshape(LANE_SPLIT, n_steps, n_seq, LANES).swapaxes(1, 2).reshape(LANE_SPLIT, n_seq * n_steps, LANES)


def kernel(x_prompt, x_sample, mem_prompt, cache_kv_dil1, cache_kv_dil4, cache_kv_dil16, state_ssm_re, state_ssm_im, state_ffn_conv, cache_mem_kv, norm_mix, w_in, q_norm, k_norm, ssm_lam_re, ssm_lam_im, ssm_log_dt, ssm_b_re, ssm_b_im, ssm_c_re, ssm_c_im, ssm_d, w_glu, b_glu, attn_out_norm, ssm_out_norm, w_out, norm_x, norm_mem, w_xq, w_xkv, xq_norm, xk_norm, w_xo, norm_ffn, w_up, conv_w, conv_b, w_down):
    bp, lp, _ = x_prompt.shape
    bs, ls, _ = x_sample.shape
    depth = w_in.shape[0]
    past_len = cache_kv_dil16.shape[2]
    caches = (cache_kv_dil1, cache_kv_dil4, cache_kv_dil16)
    tables_p = _rope_tables(jnp.arange(lp))
    tables_s = tuple(jnp.tile(t, (bs, 1)) for t in _rope_tables(past_len + jnp.arange(ls)))

    yp = x_prompt.reshape(bp * lp, D_MODEL)
    ys = x_sample.reshape(bs * ls, D_MODEL)
    kv_p = [[] for _ in range(N_GROUPS)]
    kv_s = [[] for _ in range(N_GROUPS)]
    re_p, im_p, conv_p, mem_p, re_s, im_s, conv_s = [], [], [], [], [], [], []
    for l in range(depth):
        p = _layer_params(l, norm_mix, w_in, q_norm, k_norm, ssm_lam_re, ssm_lam_im, ssm_log_dt, ssm_b_re,
                          ssm_b_im, ssm_c_re, ssm_c_im, ssm_d, w_glu, b_glu, attn_out_norm, ssm_out_norm, w_out,
                          norm_x, norm_mem, w_xq, w_xkv, xq_norm, xk_norm, w_xo, norm_ffn, w_up, conv_w, conv_b,
                          w_down)
        sample_front = _mixer_front(ys, p, tables_s, bs * ls)
        mkv, mkv_rows = _memory_kv(mem_prompt.reshape(bp * N_MEM, D_MODEL), p['norm_mem'], p['w_xkv'],
                                   p['xk_norm'], bp)
        mkv = mkv.reshape(bp, N_MEM, 2 * D_MODEL)
        mkv_rows = mkv_rows.reshape(bp, N_MEM, 2, MEM_HALVES, X_HEADS, LANES)
        mem_p.append(jnp.transpose(mkv_rows, (0, 1, 2, 4, 3, 5)).reshape(bp, N_MEM, 2, X_HEADS, X_HEAD_DIM))
        assert ATTN_GROUPS[-1][0] >= lp
        mid, wide = N_GROUPS - 2, N_GROUPS - 1
        q0, q1, q2, kv0, kv1, kv2, u, kvt, o_mid, lse_mid = _mixer_front(
            yp, p, tables_p, 512, True,
            sample_attn=(sample_front[mid], sample_front[N_GROUPS + mid], caches[mid][l], bs, ls, ATTN_GROUPS[mid][1]))
        os_, ls_ = [], []
        for g, (qg, kvg) in enumerate(((q0, kv0), (q1, kv1), (q2, kv2))):
            win, dil = ATTN_GROUPS[g]
            o, lse = _prompt_attn(qg, kvg, bp, lp, dil)
            os_.append(o)
            ls_.append(lse)
            keep = min(win, lp)
            if g == N_GROUPS - 1:
                kv_p[g].append(jnp.transpose(kvt.reshape(bp, 2, HEADS_PER_GROUP, HEAD_DIM, lp), (0, 4, 1, 2, 3)))
            else:
                tail = kvg.reshape(2 * LANE_SPLIT, bp, lp, LANES)[:, :, lp - keep:].reshape(2 * LANE_SPLIT, bp * keep, LANES)
                kv_p[g].append(_kv_rows(tail, bp, keep))
        zero_state = jnp.zeros((bp, SSM_LANES), F32)
        y, fr, fi, o_wide, lse_wide = _run_ssm(
            u, zero_state, zero_state, p, bp, lp, 64,
            sample_attn=(sample_front[wide], sample_front[N_GROUPS + wide], caches[wide][l], bs, ls,
                         ATTN_GROUPS[wide][1]))
        re_p.append(fr.reshape(bp, SSM_GROUPS, SSM_STATE))
        im_p.append(fi.reshape(bp, SSM_GROUPS, SSM_STATE))
        h = _mix_cross(os_, ls_, y, yp, mkv, p, lp, 512)
        yp, tail = _ffn_prompt(h, p['norm_ffn'], p['w_up'], p['conv_w'], p['conv_b'], p['w_down'], bp, lp, 512)
        conv_p.append(tail[:, -1, CARRY_ROWS - (CONV_W - 1):])
        q0, q1, q2, kv0, kv1, kv2, u = sample_front
        os_, ls_ = [], []
        for g, (qg, kvg) in enumerate(((q0, kv0), (q1, kv1), (q2, kv2))):
            win, dil = ATTN_GROUPS[g]
            if g == wide:
                o, lse = o_wide, lse_wide
            elif g == mid:
                o, lse = o_mid, lse_mid
            else:
                o, lse = _sample_attn(qg, kvg, caches[g][l], bs, ls, dil, max(1, 32 // dil))
            os_.append(o)
            ls_.append(lse)
            kv_s[g].append(_kv_rows(kvg, bs, ls))
        y, fr, fi = _run_ssm(_halves_to_time_major(u, bs, ls), state_ssm_re[l].reshape(bs, SSM_LANES),
                             state_ssm_im[l].reshape(bs, SSM_LANES), p, bs, ls, ls)
        re_s.append(fr.reshape(bs, SSM_GROUPS, SSM_STATE))
        im_s.append(fi.reshape(bs, SSM_GROUPS, SSM_STATE))
        h = _mix_out(os_, ls_, _halves_to_seq_major(y, bs, ls), ys, p['attn_out_norm'], p['ssm_out_norm'], p['w_glu'], p['b_glu'], p['w_out'],
                     bs * ls)
        h = _cross_sample(h, cache_mem_kv[l], p['norm_x'], p['w_xq'], p['xq_norm'], p['w_xo'], bs, ls, 4)
        state_tm = state_ffn_conv[l].swapaxes(0, 1).reshape((CONV_W - 1) * bs, 2 * D_FF)
        ys_tm, ng, nv = _ffn_sample(_to_time_major(h, bs, ls), p['norm_ffn'], state_tm, p['w_up'], p['conv_w'],
                                    p['conv_b'], p['w_down'], bs, ls)
        ys = _to_seq_major(ys_tm, bs, ls)
        conv_s.append(jnp.concatenate([ng, nv], axis=-1).reshape(CONV_W - 1, bs, 2 * D_FF).swapaxes(0, 1))

    st = jnp.stack
    return (yp.reshape(bp, lp, D_MODEL), ys.reshape(bs, ls, D_MODEL),
            st(kv_p[0]), st(kv_p[1]), st(kv_p[2]), st(re_p), st(im_p), st(conv_p), st(mem_p),
            st(kv_s[0]), st(kv_s[1]), st(kv_s[2]), st(re_s), st(im_s), st(conv_s))
```

```python
import functools
import math

import jax
import jax.numpy as jnp
from jax import lax
from jax.experimental import pallas as pl
from jax.experimental.pallas import tpu as pltpu

F32 = jnp.float32
BF16 = jnp.bfloat16

D_MODEL = 1024
HEAD_DIM = 64
ATTN_GROUPS = ((128, 1), (512, 4), (2048, 16))
HEADS_PER_GROUP = 4
GROUP_W = HEADS_PER_GROUP * HEAD_DIM
N_GROUPS = len(ATTN_GROUPS)
ATTN_WIDTH = N_GROUPS * GROUP_W
SSM_WIDTH = D_MODEL - ATTN_WIDTH
SSM_GROUP = 16
SSM_GROUPS = SSM_WIDTH // SSM_GROUP
SSM_STATE = 64
SSM_LANES = SSM_GROUPS * SSM_STATE
IN_WIDTH = 3 * ATTN_WIDTH + SSM_WIDTH
ROT_DIM = HEAD_DIM // 4
ROPE_THETA = 500000.0
N_MEM = 256
X_HEADS = 4
X_HEAD_DIM = D_MODEL // X_HEADS
D_FF = 2816
CONV_W = 3
EPS = 1e-6
ATTN_BLK = 128
NEG = -1e30

VMEM_LIMIT = 56 * 1024 * 1024


def _cparams(sem):
    return pltpu.CompilerParams(dimension_semantics=sem, vmem_limit_bytes=VMEM_LIMIT)


def _dot(a, b):
    return jnp.dot(a, b, preferred_element_type=F32)


def _dot_t(a, b):
    return lax.dot_general(a, b, (((1,), (1,)), ((), ())), preferred_element_type=F32)


def _rms_rows(x, gain):
    y = x * lax.rsqrt(jnp.mean(x * x, axis=-1, keepdims=True) + EPS)
    return y * gain


LANES = 128
LANE_SPLIT = GROUP_W // LANES


def _halves_spec(n_halves, rows, index_map):
    return pl.BlockSpec((n_halves, rows, LANES), index_map)


def _load_halves(ref, first, rows):
    return jnp.concatenate([ref[first + c, rows, :] for c in range(LANE_SPLIT)], axis=1)


def _store_halves(ref, first, value, rows=slice(None)):
    for c in range(LANE_SPLIT):
        ref[first + c, rows, :] = value[:, c * LANES:(c + 1) * LANES]


def _head_ones():
    r = lax.broadcasted_iota(jnp.int32, (GROUP_W, GROUP_W), 0) // HEAD_DIM
    c = lax.broadcasted_iota(jnp.int32, (GROUP_W, GROUP_W), 1) // HEAD_DIM
    return (r == c).astype(BF16)


def _head_inv_rms(z, ones_head):
    sq = z * z
    hi = sq.astype(BF16)
    lo = (sq - hi.astype(F32)).astype(BF16)
    ssum = _dot(hi, ones_head) + _dot(lo, ones_head)
    return lax.rsqrt(ssum * (1.0 / HEAD_DIM) + EPS)


def _inproj_sattn_kernel(x_ref, g_ref, w_ref, qn_ref, kn_ref, c_ref, s1_ref, s2_ref, sq_ref, skvn_ref, ct_ref,
                         q0_ref, q1_ref, q2_ref, kv0_ref, kv1_ref, kv2_ref, u_ref, kvt_ref, so_ref, slse_ref,
                         qm, osel, lsel, *, n_seq, dil, bb_n, t_n):
    ctx = _sattn_scores(sq_ref, ct_ref, qm, dil, bb_n, t_n)
    _inproj_kernel(x_ref, g_ref, w_ref, qn_ref, kn_ref, c_ref, s1_ref, s2_ref,
                   q0_ref, q1_ref, q2_ref, kv0_ref, kv1_ref, kv2_ref, u_ref, kvt_ref, n_seq=n_seq,
                   after_first_group=lambda: _sattn_finish(ctx, skvn_ref, ct_ref, so_ref, slse_ref, osel, lsel))


def _inproj_kernel(x_ref, g_ref, w_ref, qn_ref, kn_ref, c_ref, s1_ref, s2_ref,
                   q0_ref, q1_ref, q2_ref, kv0_ref, kv1_ref, kv2_ref, u_ref, *kvt_refs, n_seq, after_first_group=None):
    seq = pl.program_id(1)
    xn = _rms_rows(x_ref[...], g_ref[...]).astype(BF16)
    ones_head = _head_ones()
    cosw, sin1, sin2 = c_ref[...], s1_ref[...], s2_ref[...]
    qn, kn = qn_ref[...], kn_ref[...]
    half = ROT_DIM // 2

    def rope(z):
        return z * cosw + pltpu.roll(z, GROUP_W - half, 1) * sin1 + pltpu.roll(z, half, 1) * sin2

    q_refs = (q0_ref, q1_ref, q2_ref)
    kv_refs = (kv0_ref, kv1_ref, kv2_ref)
    proj = lambda start: _dot(xn, w_ref[:, start:start + GROUP_W])
    for g in range(N_GROUPS):
        zq, zk, zv = (proj(part * ATTN_WIDTH + g * GROUP_W) for part in range(3))
        _store_halves(q_refs[g], 0, rope(zq * _head_inv_rms(zq, ones_head) * qn) * (HEAD_DIM ** -0.5))
        k = rope(zk * _head_inv_rms(zk, ones_head) * kn)
        _store_halves(kv_refs[g], 0, k)
        _store_halves(kv_refs[g], LANE_SPLIT, zv)
        if kvt_refs and g == N_GROUPS - 1:
            kvt_refs[0][0] = k.T
            kvt_refs[0][1] = zv.T
        if g == 0 and after_first_group is not None:
            after_first_group()
    tm = x_ref.shape[0]
    u_rows = pl.ds(seq, tm, stride=n_seq) if n_seq > 1 else slice(None)
    _store_halves(u_ref, 0, proj(3 * ATTN_WIDTH), u_rows)


def _in_proj(x2d, gain, w_bf, qn, kn, cos_t, sin1_t, sin2_t, tm, feature_major_last, sample_attn=None):
    n = x2d.shape[0]
    seq_rows = cos_t.shape[0]
    n_pos, n_seq = seq_rows // tm, n // seq_rows
    row = lambda c, b: (b * n_pos + c, 0)
    row3 = lambda c, b: (0, b * n_pos + c, 0)
    fixed = lambda c, b: (0, 0)
    pos = lambda c, b: (c, 0)
    sds = jax.ShapeDtypeStruct
    in_specs = [pl.BlockSpec((tm, D_MODEL), row), pl.BlockSpec((1, D_MODEL), fixed),
                pl.BlockSpec((D_MODEL, IN_WIDTH), fixed),
                pl.BlockSpec((1, GROUP_W), fixed), pl.BlockSpec((1, GROUP_W), fixed),
                pl.BlockSpec((tm, GROUP_W), pos), pl.BlockSpec((tm, GROUP_W), pos), pl.BlockSpec((tm, GROUP_W), pos)]
    out_specs = ([_halves_spec(LANE_SPLIT, tm, row3)] * 3 + [_halves_spec(2 * LANE_SPLIT, tm, row3)] * 3
                 + [_halves_spec(LANE_SPLIT, tm * n_seq, lambda c, b: (0, c, 0))])
    out_shape = ([sds((LANE_SPLIT, n, LANES), F32)] * 3 + [sds((2 * LANE_SPLIT, n, LANES), F32)] * 3
                 + [sds((LANE_SPLIT, n, LANES), F32)])
    if feature_major_last:
        out_specs.append(pl.BlockSpec((None, 2, GROUP_W, tm), lambda c, b: (b, 0, 0, c)))
        out_shape.append(sds((n_seq, 2, GROUP_W, seq_rows), F32))
    args = [x2d, gain, w_bf, qn, kn, cos_t, sin1_t, sin2_t]
    scratch = []
    if sample_attn is None:
        body, name = functools.partial(_inproj_kernel, n_seq=n_seq), "in_proj"
    else:
        assert feature_major_last
        q, kvn, cache, batch, t_n, dil = sample_attn
        steps = n_pos * n_seq
        n_buf = cache.shape[1]
        assert n_buf == ATTN_BLK * dil and t_n <= dil and batch % steps == 0
        bb_n = batch // steps
        ct = jnp.transpose(cache, (0, 2, 3, 4, 1)).reshape(batch, 2, GROUP_W, n_buf)
        blk = lambda c, b: (0, c * n_seq + b, 0)
        in_specs += [_halves_spec(LANE_SPLIT, bb_n * t_n, blk), _halves_spec(2 * LANE_SPLIT, bb_n * t_n, blk),
                     pl.BlockSpec((bb_n, 2, GROUP_W, n_buf), lambda c, b: (c * n_seq + b, 0, 0, 0))]
        out_specs += [_halves_spec(LANE_SPLIT, bb_n * t_n, blk)] * 2
        out_shape += [sds((LANE_SPLIT, batch * t_n, LANES), F32)] * 2
        scratch = [pltpu.VMEM((bb_n, t_n * HEADS_PER_GROUP, GROUP_W), F32)] * 3
        args += [q, kvn, ct]
        body = functools.partial(_inproj_sattn_kernel, n_seq=n_seq, dil=dil, bb_n=bb_n, t_n=t_n)
        name = f"in_proj_sample_attn_d{dil}"
    return pl.pallas_call(
        body,
        grid=(n_pos, n_seq),
        in_specs=in_specs,
        out_specs=out_specs,
        out_shape=out_shape,
        scratch_shapes=scratch,
        compiler_params=_cparams(("arbitrary",) * 2),
        name=name,
    )(*args)


def _drain(*phase_generators):
    live = list(phase_generators)
    while live:
        for gen in list(live):
            try:
                next(gen)
            except StopIteration:
                live.remove(gen)


def _pattn_kernel(*refs, dil, n_sub, has_prev):
    _drain(_pattn_phases(*refs, dil=dil, n_sub=n_sub, has_prev=has_prev))


def _pattn_cross_kernel(q_ref, kvc_ref, kvp_ref, h_ref, mem_ref, g_ref, wq_ref, qn_ref, wo_ref,
                        o_ref, lse_ref, xo_ref, qsc, qm, osel, att, *, dil, n_sub, bb_n, t_n):
    _drain(_cross_sample_phases(h_ref, mem_ref, g_ref, wq_ref, qn_ref, wo_ref, xo_ref, qsc, qm, osel, att,
                                bb_n=bb_n, t_n=t_n),
           _pattn_phases(q_ref, kvc_ref, kvp_ref, o_ref, lse_ref, dil=dil, n_sub=n_sub, has_prev=True))


def _pattn_phases(*refs, dil, n_sub, has_prev):
    if has_prev:
        q_ref, kvc_ref, kvp_ref, o_ref, lse_ref = refs
    else:
        q_ref, kvc_ref, o_ref, lse_ref = refs
    n = pl.program_id(1)
    blk = ATTN_BLK
    span = blk * dil
    head = lax.broadcasted_iota(jnp.int32, (1, GROUP_W), 1) // HEAD_DIM
    nk = 2 * blk if has_prev else blk
    qi = lax.broadcasted_iota(jnp.int32, (HEADS_PER_GROUP * blk, nk), 0) % blk
    ki = lax.broadcasted_iota(jnp.int32, (HEADS_PER_GROUP * blk, nk), 1)
    band = ((ki >= qi) & (ki <= qi + blk)) if has_prev else (ki <= qi)
    band_first = (band & ((n > 0) | (ki >= blk))) if has_prev else band

    def sub(j, r):
        rows = pl.ds(j * span + r, blk, stride=dil) if dil > 1 else pl.ds(j * span, blk)
        q = _load_halves(q_ref, 0, rows)
        kc, vc = _load_halves(kvc_ref, 0, rows), _load_halves(kvc_ref, LANE_SPLIT, rows)
        if has_prev:
            if j > 0:
                prow = pl.ds((j - 1) * span + r, blk, stride=dil) if dil > 1 else pl.ds((j - 1) * span, blk)
                kp, vp = _load_halves(kvc_ref, 0, prow), _load_halves(kvc_ref, LANE_SPLIT, prow)
                mask = band
            else:
                prow = pl.ds(r, blk, stride=dil) if dil > 1 else pl.ds(0, blk)
                kp, vp = _load_halves(kvp_ref, 0, prow), _load_halves(kvp_ref, LANE_SPLIT, prow)
                mask = band_first
            k = jnp.concatenate([kp, kc], axis=0).astype(BF16)
            v = jnp.concatenate([vp, vc], axis=0).astype(BF16)
        else:
            k, v, mask = kc.astype(BF16), vc.astype(BF16), band
        qs = jnp.concatenate([jnp.where(head == h, q, 0.0) for h in range(HEADS_PER_GROUP)], axis=0).astype(BF16)
        return rows, qs, k, v, mask

    def pick(x):
        out = x[(HEADS_PER_GROUP - 1) * blk:]
        for h in range(HEADS_PER_GROUP - 2, -1, -1):
            out = jnp.where(head == h, x[h * blk:(h + 1) * blk], out)
        return out

    def run(subs):
        loaded = [sub(j, r) for j, r in subs]
        scores = [jnp.where(mask, _dot_t(qs, k), NEG) for _, qs, k, _, mask in loaded]
        yield
        stats = []
        for s in scores:
            m = jnp.max(s, axis=-1, keepdims=True)
            p = jnp.exp(s - m)
            stats.append((m, jnp.sum(p, axis=-1, keepdims=True), p.astype(BF16)))
        yield
        outs = [_dot(p, v) for (_, _, p), (_, _, _, v, _) in zip(stats, loaded)]
        yield
        for (rows, _, _, _, _), (m, l, _), pv in zip(loaded, stats, outs):
            l_c = pick(l)
            _store_halves(o_ref, 0, pick(pv) / l_c, rows)
            _store_halves(lse_ref, 0, pick(m) + jnp.log(l_c), rows)

    if dil * n_sub <= 4:
        yield from run([(j, r) for j in range(n_sub) for r in range(dil)])
    else:
        per_trip = 4
        for j in range(n_sub):
            def body(it, carry, j=j):
                _drain(run([(j, it * per_trip + u) for u in range(per_trip)]))
                return carry

            lax.fori_loop(0, dil // per_trip, body, 0)


def _prompt_attn(q, kv, batch, seq, dil, cross_sample=None):
    span = ATTN_BLK * dil
    has_prev = seq > span
    n_sub = max(1, 512 // span) if has_prev else 1
    rows = n_sub * span
    steps = seq // rows
    cur = lambda b, n: (0, b * steps + n, 0)
    sds = jax.ShapeDtypeStruct
    in_specs = [_halves_spec(LANE_SPLIT, rows, cur), _halves_spec(2 * LANE_SPLIT, rows, cur)]
    args = [q, kv]
    if has_prev:
        in_specs.append(_halves_spec(2 * LANE_SPLIT, span,
                                     lambda b, n: (0, b * (seq // span) + jnp.maximum(n * n_sub - 1, 0), 0)))
        args.append(kv)
    out_specs = [_halves_spec(LANE_SPLIT, rows, cur)] * 2
    out_shape = [sds((LANE_SPLIT, batch * seq, LANES), F32)] * 2
    scratch = []
    if cross_sample is None:
        body, name = functools.partial(_pattn_kernel, dil=dil, n_sub=n_sub, has_prev=has_prev), f"prompt_attn_d{dil}"
    else:
        assert has_prev
        h2d, mem_kv, gain, wq_bf, qn, wo_bf, n_seq, t_n = cross_sample
        assert n_seq % (batch * steps) == 0
        bb_n = n_seq // (batch * steps)
        xrows = bb_n * t_n
        mem = mem_kv.reshape(n_seq, N_MEM, 2, X_HEADS, MEM_HALVES, LANES)
        mem = jnp.transpose(mem, (0, 1, 2, 4, 3, 5)).reshape(n_seq, N_MEM * MEM_ROW_STRIDE, LANES)
        fixed = lambda b, n: (0, 0)
        in_specs += [pl.BlockSpec((xrows, D_MODEL), lambda b, n: (b * steps + n, 0)),
                     pl.BlockSpec((bb_n, N_MEM * MEM_ROW_STRIDE, LANES), lambda b, n: (b * steps + n, 0, 0)),
                     pl.BlockSpec((1, D_MODEL), fixed), pl.BlockSpec((D_MODEL, D_MODEL), fixed),
                     pl.BlockSpec((1, X_HEAD_DIM), fixed), pl.BlockSpec((D_MODEL, D_MODEL), fixed)]
        args += [h2d, mem, gain, wq_bf, qn, wo_bf]
        out_specs.append(pl.BlockSpec((xrows, D_MODEL), lambda b, n: (b * steps + n, 0)))
        out_shape.append(sds((n_seq * t_n, D_MODEL), F32))
        scratch = [pltpu.VMEM((xrows, D_MODEL), F32), pltpu.VMEM((bb_n, t_n * X_HEADS, D_MODEL), F32),
                   pltpu.VMEM((bb_n, t_n * X_HEADS, D_MODEL), F32), pltpu.VMEM((xrows, D_MODEL), F32)]
        body = functools.partial(_pattn_cross_kernel, dil=dil, n_sub=n_sub, bb_n=bb_n, t_n=t_n)
        name = f"prompt_attn_d{dil}_cross_attn_sample"
    return pl.pallas_call(
        body,
        grid=(batch, steps),
        in_specs=in_specs,
        out_specs=out_specs,
        out_shape=out_shape,
        scratch_shapes=scratch,
        compiler_params=_cparams(("arbitrary",) * 2),
        name=name,
    )(*args)


def _sattn_kernel(q_ref, kvn_ref, ct_ref, o_ref, lse_ref, qm, osel, lsel, *, dil, bb_n, t_n):
    _sattn_finish(_sattn_scores(q_ref, ct_ref, qm, dil, bb_n, t_n), kvn_ref, ct_ref, o_ref, lse_ref, osel, lsel)


def _sattn_scores(q_ref, ct_ref, qm, dil, bb_n, t_n):
    n_buf = ct_ref.shape[-1]
    hp = HEADS_PER_GROUP
    head = lax.broadcasted_iota(jnp.int32, (1, GROUP_W), 1) // HEAD_DIM
    sel = head == lax.broadcasted_iota(jnp.int32, (hp, 1), 0)
    row = lax.broadcasted_iota(jnp.int32, (t_n * hp, 1), 0)
    trow = row // hp
    sel_rows = head == row % hp
    pos = lax.broadcasted_iota(jnp.int32, (1, n_buf), 1)
    cmask = (pos >= trow) if dil == 1 else ((pos & (dil - 1)) == trow)
    qmbs, scores = [], []
    for bb in range(bb_n):
        for t in range(t_n):
            r = bb * t_n + t
            qm[bb, t * hp:(t + 1) * hp, :] = jnp.where(sel, _load_halves(q_ref, 0, slice(r, r + 1)), 0.0)
        qmbs.append(qm[bb].astype(BF16))
        scores.append(jnp.where(cmask, _dot(qmbs[bb], ct_ref[bb, 0].astype(BF16)), NEG))
    return dict(qmbs=qmbs, scores=scores, trow=trow, sel_rows=sel_rows, dil=dil, bb_n=bb_n, t_n=t_n)


def _sattn_finish(ctx, kvn_ref, ct_ref, o_ref, lse_ref, osel, lsel):
    qmbs, scores, trow, sel_rows = ctx['qmbs'], ctx['scores'], ctx['trow'], ctx['sel_rows']
    dil, bb_n, t_n, hp = ctx['dil'], ctx['bb_n'], ctx['t_n'], HEADS_PER_GROUP
    stats = []
    for bb in range(bb_n):
        qmr = qmbs[bb].astype(F32)
        m = jnp.max(scores[bb], axis=-1, keepdims=True)
        s_n = []
        for u in range(t_n):
            r = bb * t_n + u
            kn = _load_halves(kvn_ref, 0, slice(r, r + 1)).astype(BF16).astype(F32)
            valid = (trow >= u) if dil == 1 else (trow == u)
            sn = jnp.where(valid, jnp.sum(qmr * kn, axis=-1, keepdims=True), NEG)
            s_n.append(sn)
            m = jnp.maximum(m, sn)
        p_c = jnp.exp(scores[bb] - m)
        stats.append((m, s_n, jnp.sum(p_c, axis=-1, keepdims=True), p_c.astype(BF16)))
    accs = [_dot_t(stats[bb][3], ct_ref[bb, 1].astype(BF16)) for bb in range(bb_n)]
    for bb in range(bb_n):
        m, s_n, l, _ = stats[bb]
        acc = accs[bb]
        for u in range(t_n):
            p_n = jnp.exp(s_n[u] - m)
            l = l + p_n
            r = bb * t_n + u
            vn = _load_halves(kvn_ref, LANE_SPLIT, slice(r, r + 1)).astype(BF16).astype(F32)
            acc = acc + p_n.astype(BF16).astype(F32) * vn
        osel[bb] = jnp.where(sel_rows, acc / l, 0.0)
        lsel[bb] = jnp.where(sel_rows, m + jnp.log(l), 0.0)
        for t in range(t_n):
            r = slice(bb * t_n + t, bb * t_n + t + 1)
            _store_halves(o_ref, 0, jnp.sum(osel[bb, t * hp:(t + 1) * hp, :], axis=0, keepdims=True), r)
            _store_halves(lse_ref, 0, jnp.sum(lsel[bb, t * hp:(t + 1) * hp, :], axis=0, keepdims=True), r)


def _sample_attn(q, kvn, cache, batch, t_n, dil, bb_n):
    n_buf = cache.shape[1]
    assert n_buf == ATTN_BLK * dil and (dil == 1 or t_n <= dil)
    ct = jnp.transpose(cache, (0, 2, 3, 4, 1)).reshape(batch, 2, GROUP_W, n_buf)
    rows_n = t_n * HEADS_PER_GROUP
    blk = lambda i: (0, i, 0)
    return pl.pallas_call(
        functools.partial(_sattn_kernel, dil=dil, bb_n=bb_n, t_n=t_n),
        grid=(batch // bb_n,),
        in_specs=[_halves_spec(LANE_SPLIT, bb_n * t_n, blk), _halves_spec(2 * LANE_SPLIT, bb_n * t_n, blk),
                  pl.BlockSpec((bb_n, 2, GROUP_W, n_buf), lambda i: (i, 0, 0, 0))],
        out_specs=[_halves_spec(LANE_SPLIT, bb_n * t_n, blk)] * 2,
        out_shape=[jax.ShapeDtypeStruct((LANE_SPLIT, batch * t_n, LANES), F32)] * 2,
        scratch_shapes=[pltpu.VMEM((bb_n, rows_n, GROUP_W), F32)] * 3,
        compiler_params=_cparams(("arbitrary",)),
        name=f"sample_attn_d{dil}",
    )(q, kvn, ct)


def _ssm_sattn_kernel(u_ref, s0r_ref, s0i_ref, lre_ref, lim_ref, ldt_ref, bre_ref, bim_ref, cre_ref, cim_ref, d_ref,
                      sq_ref, skvn_ref, ct_ref, y_ref, fr_ref, fi_ref, so_ref, slse_ref,
                      hre, him, str_, sti, qm, osel, lsel, *, n_seq, steps, dil, bb_n, t_n):
    ctx = _sattn_scores(sq_ref, ct_ref, qm, dil, bb_n, t_n)
    _ssm_kernel(u_ref, s0r_ref, s0i_ref, lre_ref, lim_ref, ldt_ref, bre_ref, bim_ref, cre_ref, cim_ref, d_ref,
                y_ref, fr_ref, fi_ref, hre, him, str_, sti, n_seq=n_seq, steps=steps,
                before_scan=lambda: _sattn_finish(ctx, skvn_ref, ct_ref, so_ref, slse_ref, osel, lsel))


def _ssm_kernel(u_ref, s0r_ref, s0i_ref, lre_ref, lim_ref, ldt_ref, bre_ref, bim_ref, cre_ref, cim_ref, d_ref,
                y_ref, fr_ref, fi_ref, hre, him, str_, sti, *, n_seq, steps, before_scan=None):
    i = pl.program_id(0)

    @pl.when(i == 0)
    def _():
        str_[...] = s0r_ref[...]
        sti[...] = s0i_ref[...]

    lam_re, lam_im = lre_ref[...], lim_ref[...]
    dt = jnp.exp(ldt_ref[...])
    mag = jnp.exp(lam_re * dt)
    a_re = mag * jnp.cos(lam_im * dt)
    a_im = mag * jnp.sin(lam_im * dt)
    den = lam_re * lam_re + lam_im * lam_im
    nr, ni = a_re - 1.0, a_im
    c_re = (nr * lam_re + ni * lam_im) / den
    c_im = (ni * lam_re - nr * lam_im) / den

    u = _load_halves(u_ref, 0, slice(None))
    ub = u.astype(BF16)
    bu_re = _dot(ub, bre_ref[...])
    bu_im = _dot(ub, bim_ref[...])
    hre[...] = c_re * bu_re - c_im * bu_im
    him[...] = c_re * bu_im + c_im * bu_re
    if before_scan is not None:
        before_scan()

    if n_seq <= 8:
        def step(t, carry):
            sr, si = carry
            rows = pl.ds(pl.multiple_of(t * n_seq, n_seq), n_seq)
            nr_ = a_re * sr - a_im * si + hre[rows, :]
            ni_ = a_re * si + a_im * sr + him[rows, :]
            hre[rows, :] = nr_
            him[rows, :] = ni_
            return nr_, ni_

        sr, si = lax.fori_loop(0, steps, step, (str_[...], sti[...]))
        str_[...] = sr
        sti[...] = si
    else:
        for t in range(steps):
            rows = slice(t * n_seq, (t + 1) * n_seq)
            sr, si = str_[...], sti[...]
            nr_ = a_re * sr - a_im * si + hre[rows, :]
            ni_ = a_re * si + a_im * sr + him[rows, :]
            hre[rows, :] = nr_
            him[rows, :] = ni_
            str_[...] = nr_
            sti[...] = ni_

    y = _dot(hre[...].astype(BF16), cre_ref[...]) - _dot(him[...].astype(BF16), cim_ref[...])
    _store_halves(y_ref, 0, y + d_ref[...] * u)
    fr_ref[...] = str_[...]
    fi_ref[...] = sti[...]


def _ssm(u_tm, s0r, s0i, lam_re, lam_im, log_dt, b_re, b_im, c_re, c_im, dvec, n_seq, n_steps, chunk,
         sample_attn=None):
    rows = chunk * n_seq
    steps = n_steps // chunk
    fixed = lambda i: (0, 0)
    blk = lambda i: (0, i, 0)
    sds = jax.ShapeDtypeStruct
    in_specs = [_halves_spec(LANE_SPLIT, rows, blk),
                pl.BlockSpec((n_seq, SSM_LANES), fixed), pl.BlockSpec((n_seq, SSM_LANES), fixed),
                pl.BlockSpec((1, SSM_LANES), fixed), pl.BlockSpec((1, SSM_LANES), fixed),
                pl.BlockSpec((1, SSM_LANES), fixed),
                pl.BlockSpec((SSM_WIDTH, SSM_LANES), fixed), pl.BlockSpec((SSM_WIDTH, SSM_LANES), fixed),
                pl.BlockSpec((SSM_LANES, SSM_WIDTH), fixed), pl.BlockSpec((SSM_LANES, SSM_WIDTH), fixed),
                pl.BlockSpec((1, SSM_WIDTH), fixed)]
    out_specs = [_halves_spec(LANE_SPLIT, rows, blk),
                 pl.BlockSpec((n_seq, SSM_LANES), fixed), pl.BlockSpec((n_seq, SSM_LANES), fixed)]
    out_shape = [sds((LANE_SPLIT, n_steps * n_seq, LANES), F32), sds((n_seq, SSM_LANES), F32),
                 sds((n_seq, SSM_LANES), F32)]
    scratch = [pltpu.VMEM((rows, SSM_LANES), F32), pltpu.VMEM((rows, SSM_LANES), F32),
               pltpu.VMEM((n_seq, SSM_LANES), F32), pltpu.VMEM((n_seq, SSM_LANES), F32)]
    args = [u_tm, s0r, s0i, lam_re, lam_im, log_dt, b_re, b_im, c_re, c_im, dvec]
    if sample_attn is None:
        body = functools.partial(_ssm_kernel, n_seq=n_seq, steps=chunk)
        name = f"ssm_scan_{n_seq}"
    else:
        q, kvn, cache, batch, t_n, dil = sample_attn
        n_buf = cache.shape[1]
        assert n_buf == ATTN_BLK * dil and t_n <= dil and batch % steps == 0
        bb_n = batch // steps
        ct = jnp.transpose(cache, (0, 2, 3, 4, 1)).reshape(batch, 2, GROUP_W, n_buf)
        in_specs += [_halves_spec(LANE_SPLIT, bb_n * t_n, blk), _halves_spec(2 * LANE_SPLIT, bb_n * t_n, blk),
                     pl.BlockSpec((bb_n, 2, GROUP_W, n_buf), lambda i: (i, 0, 0, 0))]
        out_specs += [_halves_spec(LANE_SPLIT, bb_n * t_n, blk)] * 2
        out_shape += [sds((LANE_SPLIT, batch * t_n, LANES), F32)] * 2
        scratch += [pltpu.VMEM((bb_n, t_n * HEADS_PER_GROUP, GROUP_W), F32)] * 3
        args += [q, kvn, ct]
        body = functools.partial(_ssm_sattn_kernel, n_seq=n_seq, steps=chunk, dil=dil, bb_n=bb_n, t_n=t_n)
        name = f"ssm_scan_{n_seq}_sample_attn_d{dil}"
    return pl.pallas_call(
        body,
        grid=(steps,),
        in_specs=in_specs,
        out_specs=out_specs,
        out_shape=out_shape,
        scratch_shapes=scratch,
        compiler_params=_cparams(("arbitrary",)),
        name=name,
    )(*args)


def _mix_pre(o_refs, l_refs, y, ga_ref, wglu_ref, rows):
    zs = 0.5 * y * (1.0 + jnp.tanh(math.sqrt(2.0 / math.pi) * (y + 0.044715 * (y * y * y))))
    glu = _dot(zs.astype(BF16), wglu_ref[...])
    l0, l1, l2 = (_load_halves(r, 0, rows) for r in l_refs)
    m = jnp.maximum(jnp.maximum(l0, l1), l2)
    e0, e1, e2 = jnp.exp(l0 - m), jnp.exp(l1 - m), jnp.exp(l2 - m)
    den = e0 + e1 + e2
    a = [_load_halves(r, 0, rows) * (e / den) for r, e in zip(o_refs, (e0, e1, e2))]
    ssq = sum(jnp.sum(ai * ai, axis=-1, keepdims=True) for ai in a)
    inv = lax.rsqrt(ssq * (1.0 / ATTN_WIDTH) + EPS)
    an = [(a[g] * inv * ga_ref[:, g * GROUP_W:(g + 1) * GROUP_W]).astype(BF16) for g in range(N_GROUPS)]
    return zs, glu, an


def _mix_post(x, zs, glu, an, gs_ref, bglu_ref, wout_ref):
    acc = x
    for g in range(N_GROUPS):
        acc = acc + _dot(an[g], wout_ref[g * GROUP_W:(g + 1) * GROUP_W, :])
    gate = jax.nn.sigmoid(glu + bglu_ref[...])
    so = _rms_rows(zs * gate, gs_ref[...]).astype(BF16)
    return acc + _dot(so, wout_ref[ATTN_WIDTH:, :])


def _mixout_value(o_refs, l_refs, y, x_ref, ga_ref, gs_ref, wglu_ref, bglu_ref, wout_ref):
    zs, glu, an = _mix_pre(o_refs, l_refs, y, ga_ref, wglu_ref, slice(None))
    return _mix_post(x_ref[...], zs, glu, an, gs_ref, bglu_ref, wout_ref)


def _mixout_kernel(o0_ref, o1_ref, o2_ref, l0_ref, l1_ref, l2_ref, y_ref, x_ref,
                   ga_ref, gs_ref, wglu_ref, bglu_ref, wout_ref, h_ref):
    h_ref[...] = _mixout_value((o0_ref, o1_ref, o2_ref), (l0_ref, l1_ref, l2_ref),
                               _load_halves(y_ref, 0, slice(None)), x_ref,
                               ga_ref, gs_ref, wglu_ref, bglu_ref, wout_ref)


def _mix_out(os_, ls_, y, x2d, g_attn, g_ssm, wglu_bf, bglu, wout_bf, tm):
    n = x2d.shape[0]
    row = lambda i: (i, 0)
    fixed = lambda i: (0, 0)
    return pl.pallas_call(
        _mixout_kernel,
        grid=(n // tm,),
        in_specs=[_halves_spec(LANE_SPLIT, tm, lambda i: (0, i, 0))] * 7 + [pl.BlockSpec((tm, D_MODEL), row)]
                 + [pl.BlockSpec((1, ATTN_WIDTH), fixed), pl.BlockSpec((1, SSM_WIDTH), fixed),
                    pl.BlockSpec((SSM_WIDTH, SSM_WIDTH), fixed), pl.BlockSpec((1, SSM_WIDTH), fixed),
                    pl.BlockSpec((D_MODEL, D_MODEL), fixed)],
        out_specs=pl.BlockSpec((tm, D_MODEL), row),
        out_shape=jax.ShapeDtypeStruct((n, D_MODEL), F32),
        compiler_params=_cparams(("arbitrary",)),
        name="mix_out",
    )(*os_, *ls_, y, x2d, g_attn, g_ssm, wglu_bf, bglu, wout_bf)


def _memkv_kernel(x_ref, g_ref, w_ref, kn_ref, o_ref, rows_ref):
    xn = _rms_rows(x_ref[...], g_ref[...]).astype(BF16)
    kn = kn_ref[...]
    for kv in range(2):
        for h in range(X_HEADS):
            cols = slice(kv * D_MODEL + h * X_HEAD_DIM, kv * D_MODEL + (h + 1) * X_HEAD_DIM)
            z = _dot(xn, w_ref[:, cols])
            if kv == 0:
                z = _rms_rows(z, kn)
            o_ref[:, cols] = z
            for half in range(MEM_HALVES):
                first = (kv * MEM_HALVES + half) * X_HEADS + h
                rows_ref[pl.ds(first, N_MEM, stride=MEM_ROW_STRIDE), :] = z[:, half * LANES:(half + 1) * LANES]


def _memory_kv(mem2d, gain, w_bf, kn, n_seq):
    fixed = lambda i: (0, 0)
    sds = jax.ShapeDtypeStruct
    return pl.pallas_call(
        _memkv_kernel,
        grid=(n_seq,),
        in_specs=[pl.BlockSpec((N_MEM, D_MODEL), lambda i: (i, 0)), pl.BlockSpec((1, D_MODEL), fixed),
                  pl.BlockSpec((D_MODEL, 2 * D_MODEL), fixed), pl.BlockSpec((1, X_HEAD_DIM), fixed)],
        out_specs=[pl.BlockSpec((N_MEM, 2 * D_MODEL), lambda i: (i, 0)),
                   pl.BlockSpec((None, N_MEM * MEM_ROW_STRIDE, LANES), lambda i: (i, 0, 0))],
        out_shape=[sds((n_seq * N_MEM, 2 * D_MODEL), F32), sds((n_seq, N_MEM * MEM_ROW_STRIDE, LANES), F32)],
        compiler_params=_cparams(("arbitrary",)),
        name="memory_kv",
    )(mem2d, gain, w_bf, kn)


def _mix_cross_kernel(o0_ref, o1_ref, o2_ref, l0_ref, l1_ref, l2_ref, y_ref, x_ref,
                      ga_ref, gs_ref, wglu_ref, bglu_ref, wout_ref,
                      mkv_ref, gx_ref, wq_ref, qn_ref, wo_ref, o_ref, att, *, n_seq, blocks_per_seq):
    tm = x_ref.shape[0]
    seq = pl.program_id(0) // blocks_per_seq
    h = _mixout_value((o0_ref, o1_ref, o2_ref), (l0_ref, l1_ref, l2_ref),
                      _load_halves(y_ref, 0, pl.ds(seq, tm, stride=n_seq)), x_ref,
                      ga_ref, gs_ref, wglu_ref, bglu_ref, wout_ref)
    hn = _rms_rows(h, gx_ref[...]).astype(BF16)
    qn = qn_ref[...]
    heads = [slice(hd * X_HEAD_DIM, (hd + 1) * X_HEAD_DIM) for hd in range(X_HEADS)]
    qz = [_dot(hn, wq_ref[:, cols]) for cols in heads]
    qs = [(_rms_rows(z, qn) * (X_HEAD_DIM ** -0.5)).astype(BF16) for z in qz]
    scores = [_dot_t(q, mkv_ref[:, cols].astype(BF16)) for q, cols in zip(qs, heads)]
    probs = []
    for s in scores:
        p = jnp.exp(s - jnp.max(s, axis=-1, keepdims=True))
        probs.append((p.astype(BF16), jnp.sum(p, axis=-1, keepdims=True)))
    outs = [_dot(p, mkv_ref[:, D_MODEL + cols.start:D_MODEL + cols.stop].astype(BF16)) for (p, _), cols in zip(probs, heads)]
    for pv, (_, l), cols in zip(outs, probs, heads):
        att[:, cols] = (pv / l).astype(BF16)
    o_ref[...] = h + _dot(att[...], wo_ref[...])


def _mix_cross(os_, ls_, y, x2d, mkv, p, rows_per_seq, tm):
    n = x2d.shape[0]
    blocks_per_seq = rows_per_seq // tm
    n_seq = n // rows_per_seq
    row = lambda i: (i, 0)
    fixed = lambda i: (0, 0)
    return pl.pallas_call(
        functools.partial(_mix_cross_kernel, n_seq=n_seq, blocks_per_seq=blocks_per_seq),
        grid=(n // tm,),
        in_specs=[_halves_spec(LANE_SPLIT, tm, lambda i: (0, i, 0))] * 6
                 + [_halves_spec(LANE_SPLIT, tm * n_seq, lambda i: (0, i % blocks_per_seq, 0)),
                    pl.BlockSpec((tm, D_MODEL), row)]
                 + [pl.BlockSpec((1, ATTN_WIDTH), fixed), pl.BlockSpec((1, SSM_WIDTH), fixed),
                    pl.BlockSpec((SSM_WIDTH, SSM_WIDTH), fixed), pl.BlockSpec((1, SSM_WIDTH), fixed),
                    pl.BlockSpec((D_MODEL, D_MODEL), fixed)]
                 + [pl.BlockSpec((None, N_MEM, 2 * D_MODEL), lambda i: (i // blocks_per_seq, 0, 0)),
                    pl.BlockSpec((1, D_MODEL), fixed), pl.BlockSpec((D_MODEL, D_MODEL), fixed),
                    pl.BlockSpec((1, X_HEAD_DIM), fixed), pl.BlockSpec((D_MODEL, D_MODEL), fixed)],
        out_specs=pl.BlockSpec((tm, D_MODEL), row),
        out_shape=jax.ShapeDtypeStruct((n, D_MODEL), F32),
        scratch_shapes=[pltpu.VMEM((tm, D_MODEL), BF16)],
        compiler_params=_cparams(("arbitrary",)),
        name="mix_cross",
    )(*os_, *ls_, y, x2d, p['attn_out_norm'], p['ssm_out_norm'], p['w_glu'], p['b_glu'], p['w_out'],
      mkv, p['norm_x'], p['w_xq'], p['xq_norm'], p['w_xo'])


MEM_HALVES = X_HEAD_DIM // 128
MEM_ROW_STRIDE = 2 * MEM_HALVES * X_HEADS


def _cross_sample_kernel(h_ref, mem_ref, g_ref, wq_ref, qn_ref, wo_ref, o_ref, qsc, qm, osel, att, *, bb_n, t_n):
    _drain(_cross_sample_phases(h_ref, mem_ref, g_ref, wq_ref, qn_ref, wo_ref, o_ref, qsc, qm, osel, att,
                                bb_n=bb_n, t_n=t_n))


def _cross_sample_phases(h_ref, mem_ref, g_ref, wq_ref, qn_ref, wo_ref, o_ref, qsc, qm, osel, att, *, bb_n, t_n):
    h = h_ref[...]
    hn = _rms_rows(h, g_ref[...]).astype(BF16)
    qn = qn_ref[...]
    for hd in range(X_HEADS):
        cols = slice(hd * X_HEAD_DIM, (hd + 1) * X_HEAD_DIM)
        qsc[:, cols] = _rms_rows(_dot(hn, wq_ref[:, cols]), qn) * (X_HEAD_DIM ** -0.5)
    xh = X_HEADS
    head = lax.broadcasted_iota(jnp.int32, (1, D_MODEL), 1) // X_HEAD_DIM
    sel = head == lax.broadcasted_iota(jnp.int32, (xh, 1), 0)
    sel_rows = head == lax.broadcasted_iota(jnp.int32, (t_n * xh, 1), 0) % xh

    def gather(bb, kv):
        parts = [mem_ref[bb, pl.ds((kv * MEM_HALVES + half) * xh + hd, N_MEM, stride=MEM_ROW_STRIDE), :]
                 for hd in range(xh) for half in range(MEM_HALVES)]
        return jnp.concatenate(parts, axis=1).astype(BF16)

    scores = []
    for bb in range(bb_n):
        for t in range(t_n):
            r = bb * t_n + t
            qm[bb, t * xh:(t + 1) * xh, :] = jnp.where(sel, qsc[r:r + 1, :], 0.0)
        scores.append(_dot_t(qm[bb].astype(BF16), gather(bb, 0)))
    yield
    probs = []
    for s in scores:
        p = jnp.exp(s - jnp.max(s, axis=-1, keepdims=True))
        probs.append((p.astype(BF16), jnp.sum(p, axis=-1, keepdims=True)))
    yield
    outs = [_dot(probs[bb][0], gather(bb, 1)) for bb in range(bb_n)]
    yield
    for bb in range(bb_n):
        osel[bb] = jnp.where(sel_rows, outs[bb] / probs[bb][1], 0.0)
        for t in range(t_n):
            r = bb * t_n + t
            att[r:r + 1, :] = jnp.sum(osel[bb, t * xh:(t + 1) * xh, :], axis=0, keepdims=True)
    o_ref[...] = h + _dot(att[...].astype(BF16), wo_ref[...])


def _cross_sample(h2d, mem_kv, gain, wq_bf, qn, wo_bf, batch, t_n, bb_n):
    mem = mem_kv.reshape(batch, N_MEM, 2, X_HEADS, MEM_HALVES, 128)
    mem = jnp.transpose(mem, (0, 1, 2, 4, 3, 5)).reshape(batch, N_MEM * MEM_ROW_STRIDE, 128)
    rows = bb_n * t_n
    fixed = lambda i: (0, 0)
    return pl.pallas_call(
        functools.partial(_cross_sample_kernel, bb_n=bb_n, t_n=t_n),
        grid=(batch // bb_n,),
        in_specs=[pl.BlockSpec((rows, D_MODEL), lambda i: (i, 0)),
                  pl.BlockSpec((bb_n, N_MEM * MEM_ROW_STRIDE, 128), lambda i: (i, 0, 0)),
                  pl.BlockSpec((1, D_MODEL), fixed), pl.BlockSpec((D_MODEL, D_MODEL), fixed),
                  pl.BlockSpec((1, X_HEAD_DIM), fixed), pl.BlockSpec((D_MODEL, D_MODEL), fixed)],
        out_specs=pl.BlockSpec((rows, D_MODEL), lambda i: (i, 0)),
        out_shape=jax.ShapeDtypeStruct((batch * t_n, D_MODEL), F32),
        scratch_shapes=[pltpu.VMEM((rows, D_MODEL), F32), pltpu.VMEM((bb_n, t_n * X_HEADS, D_MODEL), F32),
                        pltpu.VMEM((bb_n, t_n * X_HEADS, D_MODEL), F32), pltpu.VMEM((rows, D_MODEL), F32)],
        compiler_params=_cparams(("arbitrary",)),
        name="cross_attn_sample",
    )(h2d, mem, gain, wq_bf, qn, wo_bf)


FFN_CHUNKS = 2
FFN_TF = D_FF // FFN_CHUNKS
CARRY_ROWS = 8


def _conv_gate(up_g, up_v, prev_g, prev_v, bufg, bufv, cwg_ref, cwv_ref, cbg_ref, cbv_ref, tm, lag):
    hist = prev_g.shape[0]
    bufg[0:hist] = prev_g
    bufg[hist:hist + tm] = up_g
    bufv[0:hist] = prev_v
    bufv[hist:hist + tm] = up_v
    cg = cbg_ref[...] + (cwg_ref[0:1, :] * bufg[hist - 2 * lag:hist - 2 * lag + tm]
                         + cwg_ref[1:2, :] * bufg[hist - lag:hist - lag + tm] + cwg_ref[2:3, :] * up_g)
    cv = cbv_ref[...] + (cwv_ref[0:1, :] * bufv[hist - 2 * lag:hist - 2 * lag + tm]
                         + cwv_ref[1:2, :] * bufv[hist - lag:hist - lag + tm] + cwv_ref[2:3, :] * up_v)
    return (cg * jax.nn.sigmoid(cg) * cv).astype(BF16)


FFN_COLS = 256
FFN_NCHUNK = D_FF // FFN_COLS


def _ffn_prompt_kernel(h_ref, g_ref, wup_ref, cw_ref, cb_ref, wd_ref, o_ref, tail_ref, act, carry, *, tm):
    i = pl.program_id(1)
    h = h_ref[...]
    xn = _rms_rows(h, g_ref[...]).astype(BF16)

    @pl.when(i == 0)
    def _():
        carry[...] = jnp.zeros(carry.shape, F32)

    for n in range(FFN_NCHUNK):
        conv = []
        for part in range(2):
            cols = slice(part * D_FF + n * FFN_COLS, part * D_FF + (n + 1) * FFN_COLS)
            up = _dot(xn, wup_ref[:, cols])
            ext = jnp.concatenate([carry[part, n], up], axis=0)
            conv.append(cb_ref[:, cols] + (cw_ref[0:1, cols] * ext[CARRY_ROWS - 2:CARRY_ROWS - 2 + tm]
                                           + cw_ref[1:2, cols] * ext[CARRY_ROWS - 1:CARRY_ROWS - 1 + tm]
                                           + cw_ref[2:3, cols] * up))
            carry[part, n] = up[tm - CARRY_ROWS:]
            tail_ref[:, cols] = up[tm - CARRY_ROWS:]
        cg, cv = conv
        act[:, n * FFN_COLS:(n + 1) * FFN_COLS] = (cg * jax.nn.sigmoid(cg) * cv).astype(BF16)
    split = (FFN_NCHUNK // 2) * FFN_COLS
    o_ref[...] = h + (_dot(act[:, :split], wd_ref[:split, :]) + _dot(act[:, split:], wd_ref[split:, :]))


def _ffn_prompt(h2d, gain, wup_bf, conv_w, conv_b, wdown_bf, batch, seq, tm):
    n_i = seq // tm
    row = lambda b, i: (b * n_i + i, 0)
    fixed = lambda b, i: (0, 0)
    once = pl.Buffered(1)
    sds = jax.ShapeDtypeStruct
    return pl.pallas_call(
        functools.partial(_ffn_prompt_kernel, tm=tm),
        grid=(batch, n_i),
        in_specs=[pl.BlockSpec((tm, D_MODEL), row), pl.BlockSpec((1, D_MODEL), fixed),
                  pl.BlockSpec((D_MODEL, 2 * D_FF), fixed, pipeline_mode=once),
                  pl.BlockSpec((CONV_W, 2 * D_FF), fixed), pl.BlockSpec((1, 2 * D_FF), fixed),
                  pl.BlockSpec((D_FF, D_MODEL), fixed, pipeline_mode=once)],
        out_specs=[pl.BlockSpec((tm, D_MODEL), row),
                   pl.BlockSpec((None, None, CARRY_ROWS, 2 * D_FF), lambda b, i: (b, i, 0, 0))],
        out_shape=[sds((batch * seq, D_MODEL), F32), sds((batch, n_i, CARRY_ROWS, 2 * D_FF), F32)],
        scratch_shapes=[pltpu.VMEM((tm, D_FF), BF16), pltpu.VMEM((2, FFN_NCHUNK, CARRY_ROWS, FFN_COLS), F32)],
        compiler_params=_cparams(("arbitrary",) * 2),
        name="ffn_prompt",
    )(h2d, gain, wup_bf, conv_w, conv_b, wdown_bf)


def _ffn_sample_kernel(h_ref, g_ref, sg_ref, sv_ref, wg_ref, wv_ref, cwg_ref, cwv_ref, cbg_ref, cbv_ref, wd_ref,
                       o_ref, ng_ref, nv_ref, xn, bufg, bufv, *, n_seq, rows):
    c = pl.program_id(0)

    @pl.when(c == 0)
    def _():
        xn[...] = _rms_rows(h_ref[...], g_ref[...]).astype(BF16)

    up_g = _dot(xn[...], wg_ref[...])
    up_v = _dot(xn[...], wv_ref[...])
    act = _conv_gate(up_g, up_v, sg_ref[...], sv_ref[...], bufg, bufv, cwg_ref, cwv_ref, cbg_ref, cbv_ref,
                     rows, n_seq)
    hist = (CONV_W - 1) * n_seq
    ng_ref[...] = bufg[rows:rows + hist]
    nv_ref[...] = bufv[rows:rows + hist]
    contrib = _dot(act, wd_ref[...])

    @pl.when(c == 0)
    def _():
        o_ref[...] = h_ref[...] + contrib

    @pl.when(c > 0)
    def _():
        o_ref[...] += contrib


def _ffn_sample(h_tm, gain, state_tm, wup_bf, conv_w, conv_b, wdown_bf, n_seq, n_steps):
    rows = n_seq * n_steps
    hist = (CONV_W - 1) * n_seq
    nc, tf = FFN_CHUNKS, FFN_TF
    gcol = lambda c: (0, c)
    vcol = lambda c: (0, nc + c)
    fixed = lambda c: (0, 0)
    sds = jax.ShapeDtypeStruct
    return pl.pallas_call(
        functools.partial(_ffn_sample_kernel, n_seq=n_seq, rows=rows),
        grid=(nc,),
        in_specs=[pl.BlockSpec((rows, D_MODEL), fixed), pl.BlockSpec((1, D_MODEL), fixed),
                  pl.BlockSpec((hist, tf), gcol), pl.BlockSpec((hist, tf), vcol),
                  pl.BlockSpec((D_MODEL, tf), gcol), pl.BlockSpec((D_MODEL, tf), vcol),
                  pl.BlockSpec((CONV_W, tf), gcol), pl.BlockSpec((CONV_W, tf), vcol),
                  pl.BlockSpec((1, tf), gcol), pl.BlockSpec((1, tf), vcol),
                  pl.BlockSpec((tf, D_MODEL), lambda c: (c, 0))],
        out_specs=[pl.BlockSpec((rows, D_MODEL), fixed), pl.BlockSpec((hist, tf), gcol),
                   pl.BlockSpec((hist, tf), gcol)],
        out_shape=[sds((rows, D_MODEL), F32), sds((hist, D_FF), F32), sds((hist, D_FF), F32)],
        scratch_shapes=[pltpu.VMEM((rows, D_MODEL), BF16), pltpu.VMEM((rows + hist, tf), F32),
                        pltpu.VMEM((rows + hist, tf), F32)],
        compiler_params=_cparams(("arbitrary",)),
        name="ffn_sample",
    )(h_tm, gain, state_tm, state_tm, wup_bf, wup_bf, conv_w, conv_w, conv_b, conv_b, wdown_bf)


def _rope_tables(pos):
    half = ROT_DIM // 2
    inv = jnp.power(jnp.float32(ROPE_THETA), -jnp.arange(half, dtype=F32) * 2.0 / ROT_DIM)
    ang = pos.astype(F32)[:, None] * inv[None, :]
    cos, sin = jnp.cos(ang), jnp.sin(ang)
    ones = jnp.ones((pos.shape[0], HEAD_DIM - ROT_DIM), F32)
    zeros = jnp.zeros((pos.shape[0], HEAD_DIM - ROT_DIM), F32)
    zh = jnp.zeros_like(sin)
    c_head = jnp.concatenate([cos, cos, ones], axis=1)
    s1_head = jnp.concatenate([-sin, zh, zeros], axis=1)
    s2_head = jnp.concatenate([zh, sin, zeros], axis=1)
    tile = lambda a: jnp.tile(a, (1, HEADS_PER_GROUP))
    return tile(c_head), tile(s1_head), tile(s2_head)


def _block_diag(w):
    g, a, b = w.shape
    eye = jnp.eye(g, dtype=w.dtype)
    return (eye[:, None, :, None] * w[:, :, None, :]).reshape(g * a, g * b)


def _layer_params(l, norm_mix, w_in, q_norm, k_norm, ssm_lam_re, ssm_lam_im, ssm_log_dt, ssm_b_re, ssm_b_im,
                  ssm_c_re, ssm_c_im, ssm_d, w_glu, b_glu, attn_out_norm, ssm_out_norm, w_out, norm_x, norm_mem,
                  w_xq, w_xkv, xq_norm, xk_norm, w_xo, norm_ffn, w_up, conv_w, conv_b, w_down):
    row = lambda a: a.reshape(1, -1).astype(F32)
    return dict(
        norm_mix=row(norm_mix[l]), w_in=w_in[l].astype(BF16),
        q_norm=row(jnp.tile(q_norm[l], HEADS_PER_GROUP)), k_norm=row(jnp.tile(k_norm[l], HEADS_PER_GROUP)),
        lam_re=row(ssm_lam_re[l]), lam_im=row(ssm_lam_im[l]),
        log_dt=row(jnp.repeat(ssm_log_dt[l], SSM_STATE)),
        b_re=_block_diag(jnp.swapaxes(ssm_b_re[l], 1, 2)).astype(BF16),
        b_im=_block_diag(jnp.swapaxes(ssm_b_im[l], 1, 2)).astype(BF16),
        c_re=_block_diag(jnp.swapaxes(ssm_c_re[l], 1, 2)).astype(BF16),
        c_im=_block_diag(jnp.swapaxes(ssm_c_im[l], 1, 2)).astype(BF16),
        ssm_d=row(ssm_d[l]), w_glu=w_glu[l].astype(BF16), b_glu=row(b_glu[l]),
        attn_out_norm=row(attn_out_norm[l]), ssm_out_norm=row(ssm_out_norm[l]), w_out=w_out[l].astype(BF16),
        norm_x=row(norm_x[l]), norm_mem=row(norm_mem[l]), w_xq=w_xq[l].astype(BF16),
        w_xkv=w_xkv[l].astype(BF16), xq_norm=row(xq_norm[l]), xk_norm=row(xk_norm[l]),
        w_xo=w_xo[l].astype(BF16), norm_ffn=row(norm_ffn[l]), w_up=w_up[l].astype(BF16),
        conv_w=conv_w[l].astype(F32), conv_b=row(conv_b[l]), w_down=w_down[l].astype(BF16))


def _kv_rows(kv_halves, n_seq, n_steps):
    heads_per_half = LANES // HEAD_DIM
    a = kv_halves.reshape(2, LANE_SPLIT, n_seq, n_steps, heads_per_half, HEAD_DIM)
    return jnp.transpose(a, (2, 3, 0, 1, 4, 5)).reshape(n_seq, n_steps, 2, HEADS_PER_GROUP, HEAD_DIM)


def _to_time_major(a, n_seq, n_steps):
    return a.reshape(n_seq, n_steps, -1).swapaxes(0, 1).reshape(n_steps * n_seq, -1)


def _to_seq_major(a, n_seq, n_steps):
    return a.reshape(n_steps, n_seq, -1).swapaxes(0, 1).reshape(n_seq * n_steps, -1)


def _mixer_front(x2d, p, tables, tm, feature_major_last=False, sample_attn=None):
    return _in_proj(x2d, p['norm_mix'], p['w_in'], p['q_norm'], p['k_norm'], *tables, tm, feature_major_last,
                    sample_attn)


def _run_ssm(u_tm, s0r, s0i, p, n_seq, n_steps, chunk, sample_attn=None):
    return _ssm(u_tm, s0r, s0i, p['lam_re'], p['lam_im'], p['log_dt'], p['b_re'], p['b_im'],
                p['c_re'], p['c_im'], p['ssm_d'], n_seq, n_steps, chunk, sample_attn)


def _halves_to_time_major(a, n_seq, n_steps):
    return a.reshape(LANE_SPLIT, n_seq, n_steps, LANES).swapaxes(1, 2).reshape(LANE_SPLIT, n_steps * n_seq, LANES)


def _halves_to_seq_major(a, n_seq, n_steps):
    return a.reshape(LANE_SPLIT, n_steps, n_seq, LANES).swapaxes(1, 2).reshape(LANE_SPLIT, n_seq * n_steps, LANES)


def kernel(x_prompt, x_sample, mem_prompt, cache_kv_dil1, cache_kv_dil4, cache_kv_dil16, state_ssm_re, state_ssm_im, state_ffn_conv, cache_mem_kv, norm_mix, w_in, q_norm, k_norm, ssm_lam_re, ssm_lam_im, ssm_log_dt, ssm_b_re, ssm_b_im, ssm_c_re, ssm_c_im, ssm_d, w_glu, b_glu, attn_out_norm, ssm_out_norm, w_out, norm_x, norm_mem, w_xq, w_xkv, xq_norm, xk_norm, w_xo, norm_ffn, w_up, conv_w, conv_b, w_down):
    bp, lp, _ = x_prompt.shape
    bs, ls, _ = x_sample.shape
    depth = w_in.shape[0]
    past_len = cache_kv_dil16.shape[2]
    caches = (cache_kv_dil1, cache_kv_dil4, cache_kv_dil16)
    tables_p = _rope_tables(jnp.arange(lp))
    tables_s = tuple(jnp.tile(t, (bs, 1)) for t in _rope_tables(past_len + jnp.arange(ls)))

    yp = x_prompt.reshape(bp * lp, D_MODEL)
    ys = x_sample.reshape(bs * ls, D_MODEL)
    kv_p = [[] for _ in range(N_GROUPS)]
    kv_s = [[] for _ in range(N_GROUPS)]
    re_p, im_p, conv_p, mem_p, re_s, im_s, conv_s = [], [], [], [], [], [], []
    for l in range(depth):
        p = _layer_params(l, norm_mix, w_in, q_norm, k_norm, ssm_lam_re, ssm_lam_im, ssm_log_dt, ssm_b_re,
                          ssm_b_im, ssm_c_re, ssm_c_im, ssm_d, w_glu, b_glu, attn_out_norm, ssm_out_norm, w_out,
                          norm_x, norm_mem, w_xq, w_xkv, xq_norm, xk_norm, w_xo, norm_ffn, w_up, conv_w, conv_b,
                          w_down)
        sample_front = _mixer_front(ys, p, tables_s, bs * ls)
        mkv, mkv_rows = _memory_kv(mem_prompt.reshape(bp * N_MEM, D_MODEL), p['norm_mem'], p['w_xkv'],
                                   p['xk_norm'], bp)
        mkv = mkv.reshape(bp, N_MEM, 2 * D_MODEL)
        mkv_rows = mkv_rows.reshape(bp, N_MEM, 2, MEM_HALVES, X_HEADS, LANES)
        mem_p.append(jnp.transpose(mkv_rows, (0, 1, 2, 4, 3, 5)).reshape(bp, N_MEM, 2, X_HEADS, X_HEAD_DIM))
        assert ATTN_GROUPS[-1][0] >= lp
        mid, wide = N_GROUPS - 2, N_GROUPS - 1
        q0, q1, q2, kv0, kv1, kv2, u, kvt, o_mid, lse_mid = _mixer_front(
            yp, p, tables_p, 512, True,
            sample_attn=(sample_front[mid], sample_front[N_GROUPS + mid], caches[mid][l], bs, ls, ATTN_GROUPS[mid][1]))
        zero_state = jnp.zeros((bp, SSM_LANES), F32)
        y_p, fr, fi, o_wide, lse_wide = _run_ssm(
            u, zero_state, zero_state, p, bp, lp, 64,
            sample_attn=(sample_front[wide], sample_front[N_GROUPS + wide], caches[wide][l], bs, ls,
                         ATTN_GROUPS[wide][1]))
        re_p.append(fr.reshape(bp, SSM_GROUPS, SSM_STATE))
        im_p.append(fi.reshape(bp, SSM_GROUPS, SSM_STATE))
        os_, ls_ = [], []
        for g in range(N_GROUPS):
            qg, kvg = sample_front[g], sample_front[N_GROUPS + g]
            if g == wide:
                o, lse = o_wide, lse_wide
            elif g == mid:
                o, lse = o_mid, lse_mid
            else:
                o, lse = _sample_attn(qg, kvg, caches[g][l], bs, ls, ATTN_GROUPS[g][1], 32)
            os_.append(o)
            ls_.append(lse)
            kv_s[g].append(_kv_rows(kvg, bs, ls))
        y, fr, fi = _run_ssm(_halves_to_time_major(sample_front[2 * N_GROUPS], bs, ls),
                             state_ssm_re[l].reshape(bs, SSM_LANES), state_ssm_im[l].reshape(bs, SSM_LANES),
                             p, bs, ls, ls)
        re_s.append(fr.reshape(bs, SSM_GROUPS, SSM_STATE))
        im_s.append(fi.reshape(bs, SSM_GROUPS, SSM_STATE))
        h_s = _mix_out(os_, ls_, _halves_to_seq_major(y, bs, ls), ys, p['attn_out_norm'], p['ssm_out_norm'],
                       p['w_glu'], p['b_glu'], p['w_out'], bs * ls)
        os_, ls_ = [], []
        for g, (qg, kvg) in enumerate(((q0, kv0), (q1, kv1), (q2, kv2))):
            win, dil = ATTN_GROUPS[g]
            if g == 0:
                o, lse, h = _prompt_attn(qg, kvg, bp, lp, dil, cross_sample=(
                    h_s, cache_mem_kv[l], p['norm_x'], p['w_xq'], p['xq_norm'], p['w_xo'], bs, ls))
            else:
                o, lse = _prompt_attn(qg, kvg, bp, lp, dil)
            os_.append(o)
            ls_.append(lse)
            keep = min(win, lp)
            if g == N_GROUPS - 1:
                kv_p[g].append(jnp.transpose(kvt.reshape(bp, 2, HEADS_PER_GROUP, HEAD_DIM, lp), (0, 4, 1, 2, 3)))
            else:
                tail = kvg.reshape(2 * LANE_SPLIT, bp, lp, LANES)[:, :, lp - keep:].reshape(2 * LANE_SPLIT, bp * keep, LANES)
                kv_p[g].append(_kv_rows(tail, bp, keep))
        hp_ = _mix_cross(os_, ls_, y_p, yp, mkv, p, lp, 512)
        yp, tail = _ffn_prompt(hp_, p['norm_ffn'], p['w_up'], p['conv_w'], p['conv_b'], p['w_down'], bp, lp, 512)
        conv_p.append(tail[:, -1, CARRY_ROWS - (CONV_W - 1):])
        state_tm = state_ffn_conv[l].swapaxes(0, 1).reshape((CONV_W - 1) * bs, 2 * D_FF)
        ys_tm, ng, nv = _ffn_sample(_to_time_major(h, bs, ls), p['norm_ffn'], state_tm, p['w_up'], p['conv_w'],
                                    p['conv_b'], p['w_down'], bs, ls)
        ys = _to_seq_major(ys_tm, bs, ls)
        conv_s.append(jnp.concatenate([ng, nv], axis=-1).reshape(CONV_W - 1, bs, 2 * D_FF).swapaxes(0, 1))

    st = jnp.stack
    return (yp.reshape(bp, lp, D_MODEL), ys.reshape(bs, ls, D_MODEL),
            st(kv_p[0]), st(kv_p[1]), st(kv_p[2]), st(re_p), st(im_p), st(conv_p), st(mem_p),
            st(kv_s[0]), st(kv_s[1]), st(kv_s[2]), st(re_s), st(im_s), st(conv_s))
```

```python
import functools
import math

import jax
import jax.numpy as jnp
from jax import lax
from jax.experimental import pallas as pl
from jax.experimental.pallas import tpu as pltpu

F32 = jnp.float32
BF16 = jnp.bfloat16

D_MODEL = 1024
HEAD_DIM = 64
ATTN_GROUPS = ((128, 1), (512, 4), (2048, 16))
HEADS_PER_GROUP = 4
GROUP_W = HEADS_PER_GROUP * HEAD_DIM
N_GROUPS = len(ATTN_GROUPS)
ATTN_WIDTH = N_GROUPS * GROUP_W
SSM_WIDTH = D_MODEL - ATTN_WIDTH
SSM_GROUP = 16
SSM_GROUPS = SSM_WIDTH // SSM_GROUP
SSM_STATE = 64
SSM_LANES = SSM_GROUPS * SSM_STATE
IN_WIDTH = 3 * ATTN_WIDTH + SSM_WIDTH
ROT_DIM = HEAD_DIM // 4
ROPE_THETA = 500000.0
N_MEM = 256
X_HEADS = 4
X_HEAD_DIM = D_MODEL // X_HEADS
D_FF = 2816
CONV_W = 3
EPS = 1e-6
ATTN_BLK = 128
NEG = -1e30

VMEM_LIMIT = 56 * 1024 * 1024


def _cparams(sem):
    return pltpu.CompilerParams(dimension_semantics=sem, vmem_limit_bytes=VMEM_LIMIT)


def _dot(a, b):
    return jnp.dot(a, b, preferred_element_type=F32)


def _dot_t(a, b):
    return lax.dot_general(a, b, (((1,), (1,)), ((), ())), preferred_element_type=F32)


def _rms_rows(x, gain):
    y = x * lax.rsqrt(jnp.mean(x * x, axis=-1, keepdims=True) + EPS)
    return y * gain


LANES = 128
LANE_SPLIT = GROUP_W // LANES


def _halves_spec(n_halves, rows, index_map):
    return pl.BlockSpec((n_halves, rows, LANES), index_map)


def _load_halves(ref, first, rows):
    return jnp.concatenate([ref[first + c, rows, :] for c in range(LANE_SPLIT)], axis=1)


def _store_halves(ref, first, value, rows=slice(None)):
    for c in range(LANE_SPLIT):
        ref[first + c, rows, :] = value[:, c * LANES:(c + 1) * LANES]


def _head_ones():
    r = lax.broadcasted_iota(jnp.int32, (GROUP_W, GROUP_W), 0) // HEAD_DIM
    c = lax.broadcasted_iota(jnp.int32, (GROUP_W, GROUP_W), 1) // HEAD_DIM
    return (r == c).astype(BF16)


def _head_inv_rms(z, ones_head):
    sq = z * z
    hi = sq.astype(BF16)
    lo = (sq - hi.astype(F32)).astype(BF16)
    ssum = _dot(hi, ones_head) + _dot(lo, ones_head)
    return lax.rsqrt(ssum * (1.0 / HEAD_DIM) + EPS)


def _inproj_sattn_kernel(x_ref, g_ref, w_ref, qn_ref, kn_ref, c_ref, s1_ref, s2_ref, sq_ref, skvn_ref, ct_ref,
                         q0_ref, q1_ref, q2_ref, kv0_ref, kv1_ref, kv2_ref, u_ref, kvt_ref, so_ref, slse_ref,
                         qm, osel, lsel, *, n_seq, dil, bb_n, t_n):
    ctx = _sattn_scores(sq_ref, ct_ref, qm, dil, bb_n, t_n)
    _inproj_kernel(x_ref, g_ref, w_ref, qn_ref, kn_ref, c_ref, s1_ref, s2_ref,
                   q0_ref, q1_ref, q2_ref, kv0_ref, kv1_ref, kv2_ref, u_ref, kvt_ref, n_seq=n_seq,
                   after_first_group=lambda: _sattn_finish(ctx, skvn_ref, ct_ref, so_ref, slse_ref, osel, lsel))


def _inproj_kernel(x_ref, g_ref, w_ref, qn_ref, kn_ref, c_ref, s1_ref, s2_ref,
                   q0_ref, q1_ref, q2_ref, kv0_ref, kv1_ref, kv2_ref, u_ref, *kvt_refs, n_seq, after_first_group=None):
    seq = pl.program_id(1)
    xn = _rms_rows(x_ref[...], g_ref[...]).astype(BF16)
    ones_head = _head_ones()
    cosw, sin1, sin2 = c_ref[...], s1_ref[...], s2_ref[...]
    qn, kn = qn_ref[...], kn_ref[...]
    half = ROT_DIM // 2

    def rope(z):
        return z * cosw + pltpu.roll(z, GROUP_W - half, 1) * sin1 + pltpu.roll(z, half, 1) * sin2

    q_refs = (q0_ref, q1_ref, q2_ref)
    kv_refs = (kv0_ref, kv1_ref, kv2_ref)
    proj = lambda start: _dot(xn, w_ref[:, start:start + GROUP_W])
    for g in range(N_GROUPS):
        zq, zk, zv = (proj(part * ATTN_WIDTH + g * GROUP_W) for part in range(3))
        _store_halves(q_refs[g], 0, rope(zq * _head_inv_rms(zq, ones_head) * qn) * (HEAD_DIM ** -0.5))
        k = rope(zk * _head_inv_rms(zk, ones_head) * kn)
        _store_halves(kv_refs[g], 0, k)
        _store_halves(kv_refs[g], LANE_SPLIT, zv)
        if kvt_refs and g == N_GROUPS - 1:
            kvt_refs[0][0] = k.T
            kvt_refs[0][1] = zv.T
        if g == 0 and after_first_group is not None:
            after_first_group()
    tm = x_ref.shape[0]
    u_rows = pl.ds(seq, tm, stride=n_seq) if n_seq > 1 else slice(None)
    _store_halves(u_ref, 0, proj(3 * ATTN_WIDTH), u_rows)


def _in_proj(x2d, gain, w_bf, qn, kn, cos_t, sin1_t, sin2_t, tm, feature_major_last, sample_attn=None):
    n = x2d.shape[0]
    seq_rows = cos_t.shape[0]
    n_pos, n_seq = seq_rows // tm, n // seq_rows
    row = lambda c, b: (b * n_pos + c, 0)
    row3 = lambda c, b: (0, b * n_pos + c, 0)
    fixed = lambda c, b: (0, 0)
    pos = lambda c, b: (c, 0)
    sds = jax.ShapeDtypeStruct
    in_specs = [pl.BlockSpec((tm, D_MODEL), row), pl.BlockSpec((1, D_MODEL), fixed),
                pl.BlockSpec((D_MODEL, IN_WIDTH), fixed),
                pl.BlockSpec((1, GROUP_W), fixed), pl.BlockSpec((1, GROUP_W), fixed),
                pl.BlockSpec((tm, GROUP_W), pos), pl.BlockSpec((tm, GROUP_W), pos), pl.BlockSpec((tm, GROUP_W), pos)]
    out_specs = ([_halves_spec(LANE_SPLIT, tm, row3)] * 3 + [_halves_spec(2 * LANE_SPLIT, tm, row3)] * 3
                 + [_halves_spec(LANE_SPLIT, tm * n_seq, lambda c, b: (0, c, 0))])
    out_shape = ([sds((LANE_SPLIT, n, LANES), F32)] * 3 + [sds((2 * LANE_SPLIT, n, LANES), F32)] * 3
                 + [sds((LANE_SPLIT, n, LANES), F32)])
    if feature_major_last:
        out_specs.append(pl.BlockSpec((None, 2, GROUP_W, tm), lambda c, b: (b, 0, 0, c)))
        out_shape.append(sds((n_seq, 2, GROUP_W, seq_rows), F32))
    args = [x2d, gain, w_bf, qn, kn, cos_t, sin1_t, sin2_t]
    scratch = []
    if sample_attn is None:
        body, name = functools.partial(_inproj_kernel, n_seq=n_seq), "in_proj"
    else:
        assert feature_major_last
        q, kvn, cache, batch, t_n, dil = sample_attn
        steps = n_pos * n_seq
        n_buf = cache.shape[1]
        assert n_buf == ATTN_BLK * dil and t_n <= dil and batch % steps == 0
        bb_n = batch // steps
        ct = jnp.transpose(cache, (0, 2, 3, 4, 1)).reshape(batch, 2, GROUP_W, n_buf)
        blk = lambda c, b: (0, c * n_seq + b, 0)
        in_specs += [_halves_spec(LANE_SPLIT, bb_n * t_n, blk), _halves_spec(2 * LANE_SPLIT, bb_n * t_n, blk),
                     pl.BlockSpec((bb_n, 2, GROUP_W, n_buf), lambda c, b: (c * n_seq + b, 0, 0, 0))]
        out_specs += [_halves_spec(LANE_SPLIT, bb_n * t_n, blk)] * 2
        out_shape += [sds((LANE_SPLIT, batch * t_n, LANES), F32)] * 2
        scratch = [pltpu.VMEM((bb_n, t_n * HEADS_PER_GROUP, GROUP_W), F32)] * 3
        args += [q, kvn, ct]
        body = functools.partial(_inproj_sattn_kernel, n_seq=n_seq, dil=dil, bb_n=bb_n, t_n=t_n)
        name = f"in_proj_sample_attn_d{dil}"
    return pl.pallas_call(
        body,
        grid=(n_pos, n_seq),
        in_specs=in_specs,
        out_specs=out_specs,
        out_shape=out_shape,
        scratch_shapes=scratch,
        compiler_params=_cparams(("arbitrary",) * 2),
        name=name,
    )(*args)


def _drain(*phase_generators):
    live = list(phase_generators)
    while live:
        for gen in list(live):
            try:
                next(gen)
            except StopIteration:
                live.remove(gen)


def _pattn_kernel(*refs, dil, n_sub, has_prev):
    _drain(_pattn_phases(*refs, dil=dil, n_sub=n_sub, has_prev=has_prev))


def _pattn_cross_kernel(q_ref, kvc_ref, kvp_ref, h_ref, mem_ref, g_ref, wq_ref, qn_ref, wo_ref,
                        o_ref, lse_ref, xo_ref, qsc, qm, osel, att, *, dil, n_sub, bb_n, t_n):
    _drain(_cross_sample_phases(h_ref, mem_ref, g_ref, wq_ref, qn_ref, wo_ref, xo_ref, qsc, qm, osel, att,
                                bb_n=bb_n, t_n=t_n),
           _pattn_phases(q_ref, kvc_ref, kvp_ref, o_ref, lse_ref, dil=dil, n_sub=n_sub, has_prev=True))


def _pattn_phases(*refs, dil, n_sub, has_prev):
    if has_prev:
        q_ref, kvc_ref, kvp_ref, o_ref, lse_ref = refs
    else:
        q_ref, kvc_ref, o_ref, lse_ref = refs
    n = pl.program_id(1)
    blk = ATTN_BLK
    span = blk * dil
    head = lax.broadcasted_iota(jnp.int32, (1, GROUP_W), 1) // HEAD_DIM
    nk = 2 * blk if has_prev else blk
    qi = lax.broadcasted_iota(jnp.int32, (HEADS_PER_GROUP * blk, nk), 0) % blk
    ki = lax.broadcasted_iota(jnp.int32, (HEADS_PER_GROUP * blk, nk), 1)
    band = ((ki >= qi) & (ki <= qi + blk)) if has_prev else (ki <= qi)
    band_first = (band & ((n > 0) | (ki >= blk))) if has_prev else band

    def sub(j, r):
        rows = pl.ds(j * span + r, blk, stride=dil) if dil > 1 else pl.ds(j * span, blk)
        q = _load_halves(q_ref, 0, rows)
        kc, vc = _load_halves(kvc_ref, 0, rows), _load_halves(kvc_ref, LANE_SPLIT, rows)
        if has_prev:
            if j > 0:
                prow = pl.ds((j - 1) * span + r, blk, stride=dil) if dil > 1 else pl.ds((j - 1) * span, blk)
                kp, vp = _load_halves(kvc_ref, 0, prow), _load_halves(kvc_ref, LANE_SPLIT, prow)
                mask = band
            else:
                prow = pl.ds(r, blk, stride=dil) if dil > 1 else pl.ds(0, blk)
                kp, vp = _load_halves(kvp_ref, 0, prow), _load_halves(kvp_ref, LANE_SPLIT, prow)
                mask = band_first
            k = jnp.concatenate([kp, kc], axis=0).astype(BF16)
            v = jnp.concatenate([vp, vc], axis=0).astype(BF16)
        else:
            k, v, mask = kc.astype(BF16), vc.astype(BF16), band
        qs = jnp.concatenate([jnp.where(head == h, q, 0.0) for h in range(HEADS_PER_GROUP)], axis=0).astype(BF16)
        return rows, qs, k, v, mask

    def pick(x):
        out = x[(HEADS_PER_GROUP - 1) * blk:]
        for h in range(HEADS_PER_GROUP - 2, -1, -1):
            out = jnp.where(head == h, x[h * blk:(h + 1) * blk], out)
        return out

    def run(subs):
        loaded = [sub(j, r) for j, r in subs]
        scores = [jnp.where(mask, _dot_t(qs, k), NEG) for _, qs, k, _, mask in loaded]
        yield
        stats = []
        for s in scores:
            m = jnp.max(s, axis=-1, keepdims=True)
            p = jnp.exp(s - m)
            stats.append((m, jnp.sum(p, axis=-1, keepdims=True), p.astype(BF16)))
        yield
        outs = [_dot(p, v) for (_, _, p), (_, _, _, v, _) in zip(stats, loaded)]
        yield
        for (rows, _, _, _, _), (m, l, _), pv in zip(loaded, stats, outs):
            l_c = pick(l)
            _store_halves(o_ref, 0, pick(pv) / l_c, rows)
            _store_halves(lse_ref, 0, pick(m) + jnp.log(l_c), rows)

    if dil * n_sub <= 4:
        yield from run([(j, r) for j in range(n_sub) for r in range(dil)])
    else:
        per_trip = 4
        for j in range(n_sub):
            def body(it, carry, j=j):
                _drain(run([(j, it * per_trip + u) for u in range(per_trip)]))
                return carry

            lax.fori_loop(0, dil // per_trip, body, 0)


def _prompt_attn(q, kv, batch, seq, dil, cross_sample=None):
    span = ATTN_BLK * dil
    has_prev = seq > span
    n_sub = max(1, 512 // span) if has_prev else 1
    rows = n_sub * span
    steps = seq // rows
    cur = lambda b, n: (0, b * steps + n, 0)
    sds = jax.ShapeDtypeStruct
    in_specs = [_halves_spec(LANE_SPLIT, rows, cur), _halves_spec(2 * LANE_SPLIT, rows, cur)]
    args = [q, kv]
    if has_prev:
        in_specs.append(_halves_spec(2 * LANE_SPLIT, span,
                                     lambda b, n: (0, b * (seq // span) + jnp.maximum(n * n_sub - 1, 0), 0)))
        args.append(kv)
    out_specs = [_halves_spec(LANE_SPLIT, rows, cur)] * 2
    out_shape = [sds((LANE_SPLIT, batch * seq, LANES), F32)] * 2
    scratch = []
    if cross_sample is None:
        body, name = functools.partial(_pattn_kernel, dil=dil, n_sub=n_sub, has_prev=has_prev), f"prompt_attn_d{dil}"
    else:
        assert has_prev
        h2d, mem_kv, gain, wq_bf, qn, wo_bf, n_seq, t_n = cross_sample
        assert n_seq % (batch * steps) == 0
        bb_n = n_seq // (batch * steps)
        xrows = bb_n * t_n
        mem = mem_kv.reshape(n_seq, N_MEM, 2, X_HEADS, MEM_HALVES, LANES)
        mem = jnp.transpose(mem, (0, 1, 2, 4, 3, 5)).reshape(n_seq, N_MEM * MEM_ROW_STRIDE, LANES)
        fixed = lambda b, n: (0, 0)
        in_specs += [pl.BlockSpec((xrows, D_MODEL), lambda b, n: (b * steps + n, 0)),
                     pl.BlockSpec((bb_n, N_MEM * MEM_ROW_STRIDE, LANES), lambda b, n: (b * steps + n, 0, 0)),
                     pl.BlockSpec((1, D_MODEL), fixed), pl.BlockSpec((D_MODEL, D_MODEL), fixed),
                     pl.BlockSpec((1, X_HEAD_DIM), fixed), pl.BlockSpec((D_MODEL, D_MODEL), fixed)]
        args += [h2d, mem, gain, wq_bf, qn, wo_bf]
        out_specs.append(pl.BlockSpec((xrows, D_MODEL), lambda b, n: (b * steps + n, 0)))
        out_shape.append(sds((n_seq * t_n, D_MODEL), F32))
        scratch = [pltpu.VMEM((xrows, D_MODEL), F32), pltpu.VMEM((bb_n, t_n * X_HEADS, D_MODEL), F32),
                   pltpu.VMEM((bb_n, t_n * X_HEADS, D_MODEL), F32), pltpu.VMEM((xrows, D_MODEL), F32)]
        body = functools.partial(_pattn_cross_kernel, dil=dil, n_sub=n_sub, bb_n=bb_n, t_n=t_n)
        name = f"prompt_attn_d{dil}_cross_attn_sample"
    return pl.pallas_call(
        body,
        grid=(batch, steps),
        in_specs=in_specs,
        out_specs=out_specs,
        out_shape=out_shape,
        scratch_shapes=scratch,
        compiler_params=_cparams(("arbitrary",) * 2),
        name=name,
    )(*args)


def _sattn_kernel(q_ref, kvn_ref, ct_ref, o_ref, lse_ref, qm, osel, lsel, *, dil, bb_n, t_n):
    _sattn_finish(_sattn_scores(q_ref, ct_ref, qm, dil, bb_n, t_n), kvn_ref, ct_ref, o_ref, lse_ref, osel, lsel)


def _sattn_scores(q_ref, ct_ref, qm, dil, bb_n, t_n):
    n_buf = ct_ref.shape[-1]
    hp = HEADS_PER_GROUP
    head = lax.broadcasted_iota(jnp.int32, (1, GROUP_W), 1) // HEAD_DIM
    sel = head == lax.broadcasted_iota(jnp.int32, (hp, 1), 0)
    row = lax.broadcasted_iota(jnp.int32, (t_n * hp, 1), 0)
    trow = row // hp
    sel_rows = head == row % hp
    pos = lax.broadcasted_iota(jnp.int32, (1, n_buf), 1)
    cmask = (pos >= trow) if dil == 1 else ((pos & (dil - 1)) == trow)
    qmbs, scores = [], []
    for bb in range(bb_n):
        for t in range(t_n):
            r = bb * t_n + t
            qm[bb, t * hp:(t + 1) * hp, :] = jnp.where(sel, _load_halves(q_ref, 0, slice(r, r + 1)), 0.0)
        qmbs.append(qm[bb].astype(BF16))
        scores.append(jnp.where(cmask, _dot(qmbs[bb], ct_ref[bb, 0].astype(BF16)), NEG))
    return dict(qmbs=qmbs, scores=scores, trow=trow, sel_rows=sel_rows, dil=dil, bb_n=bb_n, t_n=t_n)


def _sattn_finish(ctx, kvn_ref, ct_ref, o_ref, lse_ref, osel, lsel):
    qmbs, scores, trow, sel_rows = ctx['qmbs'], ctx['scores'], ctx['trow'], ctx['sel_rows']
    dil, bb_n, t_n, hp = ctx['dil'], ctx['bb_n'], ctx['t_n'], HEADS_PER_GROUP
    stats = []
    for bb in range(bb_n):
        qmr = qmbs[bb].astype(F32)
        m = jnp.max(scores[bb], axis=-1, keepdims=True)
        s_n = []
        for u in range(t_n):
            r = bb * t_n + u
            kn = _load_halves(kvn_ref, 0, slice(r, r + 1)).astype(BF16).astype(F32)
            valid = (trow >= u) if dil == 1 else (trow == u)
            sn = jnp.where(valid, jnp.sum(qmr * kn, axis=-1, keepdims=True), NEG)
            s_n.append(sn)
            m = jnp.maximum(m, sn)
        p_c = jnp.exp(scores[bb] - m)
        stats.append((m, s_n, jnp.sum(p_c, axis=-1, keepdims=True), p_c.astype(BF16)))
    accs = [_dot_t(stats[bb][3], ct_ref[bb, 1].astype(BF16)) for bb in range(bb_n)]
    for bb in range(bb_n):
        m, s_n, l, _ = stats[bb]
        acc = accs[bb]
        for u in range(t_n):
            p_n = jnp.exp(s_n[u] - m)
            l = l + p_n
            r = bb * t_n + u
            vn = _load_halves(kvn_ref, LANE_SPLIT, slice(r, r + 1)).astype(BF16).astype(F32)
            acc = acc + p_n.astype(BF16).astype(F32) * vn
        osel[bb] = jnp.where(sel_rows, acc / l, 0.0)
        lsel[bb] = jnp.where(sel_rows, m + jnp.log(l), 0.0)
        for t in range(t_n):
            r = slice(bb * t_n + t, bb * t_n + t + 1)
            _store_halves(o_ref, 0, jnp.sum(osel[bb, t * hp:(t + 1) * hp, :], axis=0, keepdims=True), r)
            _store_halves(lse_ref, 0, jnp.sum(lsel[bb, t * hp:(t + 1) * hp, :], axis=0, keepdims=True), r)


def _sample_attn(q, kvn, cache, batch, t_n, dil, bb_n):
    n_buf = cache.shape[1]
    assert n_buf == ATTN_BLK * dil and (dil == 1 or t_n <= dil)
    ct = jnp.transpose(cache, (0, 2, 3, 4, 1)).reshape(batch, 2, GROUP_W, n_buf)
    rows_n = t_n * HEADS_PER_GROUP
    blk = lambda i: (0, i, 0)
    return pl.pallas_call(
        functools.partial(_sattn_kernel, dil=dil, bb_n=bb_n, t_n=t_n),
        grid=(batch // bb_n,),
        in_specs=[_halves_spec(LANE_SPLIT, bb_n * t_n, blk), _halves_spec(2 * LANE_SPLIT, bb_n * t_n, blk),
                  pl.BlockSpec((bb_n, 2, GROUP_W, n_buf), lambda i: (i, 0, 0, 0))],
        out_specs=[_halves_spec(LANE_SPLIT, bb_n * t_n, blk)] * 2,
        out_shape=[jax.ShapeDtypeStruct((LANE_SPLIT, batch * t_n, LANES), F32)] * 2,
        scratch_shapes=[pltpu.VMEM((bb_n, rows_n, GROUP_W), F32)] * 3,
        compiler_params=_cparams(("arbitrary",)),
        name=f"sample_attn_d{dil}",
    )(q, kvn, ct)


def _ssm_sattn_kernel(u_ref, s0r_ref, s0i_ref, lre_ref, lim_ref, ldt_ref, bre_ref, bim_ref, cre_ref, cim_ref, d_ref,
                      sq_ref, skvn_ref, ct_ref, y_ref, fr_ref, fi_ref, so_ref, slse_ref,
                      hre, him, str_, sti, qm, osel, lsel, *, n_seq, steps, dil, bb_n, t_n):
    ctx = _sattn_scores(sq_ref, ct_ref, qm, dil, bb_n, t_n)
    _ssm_kernel(u_ref, s0r_ref, s0i_ref, lre_ref, lim_ref, ldt_ref, bre_ref, bim_ref, cre_ref, cim_ref, d_ref,
                y_ref, fr_ref, fi_ref, hre, him, str_, sti, n_seq=n_seq, steps=steps,
                before_scan=lambda: _sattn_finish(ctx, skvn_ref, ct_ref, so_ref, slse_ref, osel, lsel))


def _ssm_kernel(u_ref, s0r_ref, s0i_ref, lre_ref, lim_ref, ldt_ref, bre_ref, bim_ref, cre_ref, cim_ref, d_ref,
                y_ref, fr_ref, fi_ref, hre, him, str_, sti, *, n_seq, steps, before_scan=None):
    i = pl.program_id(0)

    @pl.when(i == 0)
    def _():
        str_[...] = s0r_ref[...]
        sti[...] = s0i_ref[...]

    lam_re, lam_im = lre_ref[...], lim_ref[...]
    dt = jnp.exp(ldt_ref[...])
    mag = jnp.exp(lam_re * dt)
    a_re = mag * jnp.cos(lam_im * dt)
    a_im = mag * jnp.sin(lam_im * dt)
    den = lam_re * lam_re + lam_im * lam_im
    nr, ni = a_re - 1.0, a_im
    c_re = (nr * lam_re + ni * lam_im) / den
    c_im = (ni * lam_re - nr * lam_im) / den

    u = _load_halves(u_ref, 0, slice(None))
    ub = u.astype(BF16)
    bu_re = _dot(ub, bre_ref[...])
    bu_im = _dot(ub, bim_ref[...])
    hre[...] = c_re * bu_re - c_im * bu_im
    him[...] = c_re * bu_im + c_im * bu_re
    if before_scan is not None:
        before_scan()

    if n_seq <= 8:
        def step(t, carry):
            sr, si = carry
            rows = pl.ds(pl.multiple_of(t * n_seq, n_seq), n_seq)
            nr_ = a_re * sr - a_im * si + hre[rows, :]
            ni_ = a_re * si + a_im * sr + him[rows, :]
            hre[rows, :] = nr_
            him[rows, :] = ni_
            return nr_, ni_

        sr, si = lax.fori_loop(0, steps, step, (str_[...], sti[...]))
        str_[...] = sr
        sti[...] = si
    else:
        for t in range(steps):
            rows = slice(t * n_seq, (t + 1) * n_seq)
            sr, si = str_[...], sti[...]
            nr_ = a_re * sr - a_im * si + hre[rows, :]
            ni_ = a_re * si + a_im * sr + him[rows, :]
            hre[rows, :] = nr_
            him[rows, :] = ni_
            str_[...] = nr_
            sti[...] = ni_

    y = _dot(hre[...].astype(BF16), cre_ref[...]) - _dot(him[...].astype(BF16), cim_ref[...])
    _store_halves(y_ref, 0, y + d_ref[...] * u)
    fr_ref[...] = str_[...]
    fi_ref[...] = sti[...]


def _ssm(u_tm, s0r, s0i, lam_re, lam_im, log_dt, b_re, b_im, c_re, c_im, dvec, n_seq, n_steps, chunk,
         sample_attn=None):
    rows = chunk * n_seq
    steps = n_steps // chunk
    fixed = lambda i: (0, 0)
    blk = lambda i: (0, i, 0)
    sds = jax.ShapeDtypeStruct
    in_specs = [_halves_spec(LANE_SPLIT, rows, blk),
                pl.BlockSpec((n_seq, SSM_LANES), fixed), pl.BlockSpec((n_seq, SSM_LANES), fixed),
                pl.BlockSpec((1, SSM_LANES), fixed), pl.BlockSpec((1, SSM_LANES), fixed),
                pl.BlockSpec((1, SSM_LANES), fixed),
                pl.BlockSpec((SSM_WIDTH, SSM_LANES), fixed), pl.BlockSpec((SSM_WIDTH, SSM_LANES), fixed),
                pl.BlockSpec((SSM_LANES, SSM_WIDTH), fixed), pl.BlockSpec((SSM_LANES, SSM_WIDTH), fixed),
                pl.BlockSpec((1, SSM_WIDTH), fixed)]
    out_specs = [_halves_spec(LANE_SPLIT, rows, blk),
                 pl.BlockSpec((n_seq, SSM_LANES), fixed), pl.BlockSpec((n_seq, SSM_LANES), fixed)]
    out_shape = [sds((LANE_SPLIT, n_steps * n_seq, LANES), F32), sds((n_seq, SSM_LANES), F32),
                 sds((n_seq, SSM_LANES), F32)]
    scratch = [pltpu.VMEM((rows, SSM_LANES), F32), pltpu.VMEM((rows, SSM_LANES), F32),
               pltpu.VMEM((n_seq, SSM_LANES), F32), pltpu.VMEM((n_seq, SSM_LANES), F32)]
    args = [u_tm, s0r, s0i, lam_re, lam_im, log_dt, b_re, b_im, c_re, c_im, dvec]
    if sample_attn is None:
        body = functools.partial(_ssm_kernel, n_seq=n_seq, steps=chunk)
        name = f"ssm_scan_{n_seq}"
    else:
        q, kvn, cache, batch, t_n, dil = sample_attn
        n_buf = cache.shape[1]
        assert n_buf == ATTN_BLK * dil and t_n <= dil and batch % steps == 0
        bb_n = batch // steps
        ct = jnp.transpose(cache, (0, 2, 3, 4, 1)).reshape(batch, 2, GROUP_W, n_buf)
        in_specs += [_halves_spec(LANE_SPLIT, bb_n * t_n, blk), _halves_spec(2 * LANE_SPLIT, bb_n * t_n, blk),
                     pl.BlockSpec((bb_n, 2, GROUP_W, n_buf), lambda i: (i, 0, 0, 0))]
        out_specs += [_halves_spec(LANE_SPLIT, bb_n * t_n, blk)] * 2
        out_shape += [sds((LANE_SPLIT, batch * t_n, LANES), F32)] * 2
        scratch += [pltpu.VMEM((bb_n, t_n * HEADS_PER_GROUP, GROUP_W), F32)] * 3
        args += [q, kvn, ct]
        body = functools.partial(_ssm_sattn_kernel, n_seq=n_seq, steps=chunk, dil=dil, bb_n=bb_n, t_n=t_n)
        name = f"ssm_scan_{n_seq}_sample_attn_d{dil}"
    return pl.pallas_call(
        body,
        grid=(steps,),
        in_specs=in_specs,
        out_specs=out_specs,
        out_shape=out_shape,
        scratch_shapes=scratch,
        compiler_params=_cparams(("arbitrary",)),
        name=name,
    )(*args)


def _mix_pre(o_refs, l_refs, y, ga_ref, wglu_ref, rows):
    zs = 0.5 * y * (1.0 + jnp.tanh(math.sqrt(2.0 / math.pi) * (y + 0.044715 * (y * y * y))))
    glu = _dot(zs.astype(BF16), wglu_ref[...])
    l0, l1, l2 = (_load_halves(r, 0, rows) for r in l_refs)
    m = jnp.maximum(jnp.maximum(l0, l1), l2)
    e0, e1, e2 = jnp.exp(l0 - m), jnp.exp(l1 - m), jnp.exp(l2 - m)
    den = e0 + e1 + e2
    a = [_load_halves(r, 0, rows) * (e / den) for r, e in zip(o_refs, (e0, e1, e2))]
    ssq = sum(jnp.sum(ai * ai, axis=-1, keepdims=True) for ai in a)
    inv = lax.rsqrt(ssq * (1.0 / ATTN_WIDTH) + EPS)
    an = [(a[g] * inv * ga_ref[:, g * GROUP_W:(g + 1) * GROUP_W]).astype(BF16) for g in range(N_GROUPS)]
    return zs, glu, an


def _mix_post(x, zs, glu, an, gs_ref, bglu_ref, wout_ref):
    acc = x
    for g in range(N_GROUPS):
        acc = acc + _dot(an[g], wout_ref[g * GROUP_W:(g + 1) * GROUP_W, :])
    gate = jax.nn.sigmoid(glu + bglu_ref[...])
    so = _rms_rows(zs * gate, gs_ref[...]).astype(BF16)
    return acc + _dot(so, wout_ref[ATTN_WIDTH:, :])


def _mixout_value(o_refs, l_refs, y, x_ref, ga_ref, gs_ref, wglu_ref, bglu_ref, wout_ref):
    zs, glu, an = _mix_pre(o_refs, l_refs, y, ga_ref, wglu_ref, slice(None))
    return _mix_post(x_ref[...], zs, glu, an, gs_ref, bglu_ref, wout_ref)


def _mixout_kernel(o0_ref, o1_ref, o2_ref, l0_ref, l1_ref, l2_ref, y_ref, x_ref,
                   ga_ref, gs_ref, wglu_ref, bglu_ref, wout_ref, h_ref):
    h_ref[...] = _mixout_value((o0_ref, o1_ref, o2_ref), (l0_ref, l1_ref, l2_ref),
                               _load_halves(y_ref, 0, slice(None)), x_ref,
                               ga_ref, gs_ref, wglu_ref, bglu_ref, wout_ref)


def _mix_out(os_, ls_, y, x2d, g_attn, g_ssm, wglu_bf, bglu, wout_bf, tm):
    n = x2d.shape[0]
    row = lambda i: (i, 0)
    fixed = lambda i: (0, 0)
    return pl.pallas_call(
        _mixout_kernel,
        grid=(n // tm,),
        in_specs=[_halves_spec(LANE_SPLIT, tm, lambda i: (0, i, 0))] * 7 + [pl.BlockSpec((tm, D_MODEL), row)]
                 + [pl.BlockSpec((1, ATTN_WIDTH), fixed), pl.BlockSpec((1, SSM_WIDTH), fixed),
                    pl.BlockSpec((SSM_WIDTH, SSM_WIDTH), fixed), pl.BlockSpec((1, SSM_WIDTH), fixed),
                    pl.BlockSpec((D_MODEL, D_MODEL), fixed)],
        out_specs=pl.BlockSpec((tm, D_MODEL), row),
        out_shape=jax.ShapeDtypeStruct((n, D_MODEL), F32),
        compiler_params=_cparams(("arbitrary",)),
        name="mix_out",
    )(*os_, *ls_, y, x2d, g_attn, g_ssm, wglu_bf, bglu, wout_bf)


def _memkv_kernel(x_ref, g_ref, w_ref, kn_ref, o_ref, rows_ref):
    xn = _rms_rows(x_ref[...], g_ref[...]).astype(BF16)
    kn = kn_ref[...]
    for kv in range(2):
        for h in range(X_HEADS):
            cols = slice(kv * D_MODEL + h * X_HEAD_DIM, kv * D_MODEL + (h + 1) * X_HEAD_DIM)
            z = _dot(xn, w_ref[:, cols])
            if kv == 0:
                z = _rms_rows(z, kn)
            o_ref[:, cols] = z
            for half in range(MEM_HALVES):
                first = (kv * MEM_HALVES + half) * X_HEADS + h
                rows_ref[pl.ds(first, N_MEM, stride=MEM_ROW_STRIDE), :] = z[:, half * LANES:(half + 1) * LANES]


def _memory_kv(mem2d, gain, w_bf, kn, n_seq):
    fixed = lambda i: (0, 0)
    sds = jax.ShapeDtypeStruct
    return pl.pallas_call(
        _memkv_kernel,
        grid=(n_seq,),
        in_specs=[pl.BlockSpec((N_MEM, D_MODEL), lambda i: (i, 0)), pl.BlockSpec((1, D_MODEL), fixed),
                  pl.BlockSpec((D_MODEL, 2 * D_MODEL), fixed), pl.BlockSpec((1, X_HEAD_DIM), fixed)],
        out_specs=[pl.BlockSpec((N_MEM, 2 * D_MODEL), lambda i: (i, 0)),
                   pl.BlockSpec((None, N_MEM * MEM_ROW_STRIDE, LANES), lambda i: (i, 0, 0))],
        out_shape=[sds((n_seq * N_MEM, 2 * D_MODEL), F32), sds((n_seq, N_MEM * MEM_ROW_STRIDE, LANES), F32)],
        compiler_params=_cparams(("arbitrary",)),
        name="memory_kv",
    )(mem2d, gain, w_bf, kn)


def _mix_cross_kernel(o0_ref, o1_ref, o2_ref, l0_ref, l1_ref, l2_ref, y_ref, x_ref,
                      ga_ref, gs_ref, wglu_ref, bglu_ref, wout_ref,
                      mkv_ref, gx_ref, wq_ref, qn_ref, wo_ref, o_ref, att, *, n_seq, blocks_per_seq):
    tm = x_ref.shape[0]
    seq = pl.program_id(0) // blocks_per_seq
    h = _mixout_value((o0_ref, o1_ref, o2_ref), (l0_ref, l1_ref, l2_ref),
                      _load_halves(y_ref, 0, pl.ds(seq, tm, stride=n_seq)), x_ref,
                      ga_ref, gs_ref, wglu_ref, bglu_ref, wout_ref)
    hn = _rms_rows(h, gx_ref[...]).astype(BF16)
    qn = qn_ref[...]
    heads = [slice(hd * X_HEAD_DIM, (hd + 1) * X_HEAD_DIM) for hd in range(X_HEADS)]
    qz = [_dot(hn, wq_ref[:, cols]) for cols in heads]
    qs = [(_rms_rows(z, qn) * (X_HEAD_DIM ** -0.5)).astype(BF16) for z in qz]
    scores = [_dot_t(q, mkv_ref[:, cols].astype(BF16)) for q, cols in zip(qs, heads)]
    probs = []
    for s in scores:
        p = jnp.exp(s - jnp.max(s, axis=-1, keepdims=True))
        probs.append((p.astype(BF16), jnp.sum(p, axis=-1, keepdims=True)))
    outs = [_dot(p, mkv_ref[:, D_MODEL + cols.start:D_MODEL + cols.stop].astype(BF16)) for (p, _), cols in zip(probs, heads)]
    for pv, (_, l), cols in zip(outs, probs, heads):
        att[:, cols] = (pv / l).astype(BF16)
    o_ref[...] = h + _dot(att[...], wo_ref[...])


def _mix_cross(os_, ls_, y, x2d, mkv, p, rows_per_seq, tm):
    n = x2d.shape[0]
    blocks_per_seq = rows_per_seq // tm
    n_seq = n // rows_per_seq
    row = lambda i: (i, 0)
    fixed = lambda i: (0, 0)
    return pl.pallas_call(
        functools.partial(_mix_cross_kernel, n_seq=n_seq, blocks_per_seq=blocks_per_seq),
        grid=(n // tm,),
        in_specs=[_halves_spec(LANE_SPLIT, tm, lambda i: (0, i, 0))] * 6
                 + [_halves_spec(LANE_SPLIT, tm * n_seq, lambda i: (0, i % blocks_per_seq, 0)),
                    pl.BlockSpec((tm, D_MODEL), row)]
                 + [pl.BlockSpec((1, ATTN_WIDTH), fixed), pl.BlockSpec((1, SSM_WIDTH), fixed),
                    pl.BlockSpec((SSM_WIDTH, SSM_WIDTH), fixed), pl.BlockSpec((1, SSM_WIDTH), fixed),
                    pl.BlockSpec((D_MODEL, D_MODEL), fixed)]
                 + [pl.BlockSpec((None, N_MEM, 2 * D_MODEL), lambda i: (i // blocks_per_seq, 0, 0)),
                    pl.BlockSpec((1, D_MODEL), fixed), pl.BlockSpec((D_MODEL, D_MODEL), fixed),
                    pl.BlockSpec((1, X_HEAD_DIM), fixed), pl.BlockSpec((D_MODEL, D_MODEL), fixed)],
        out_specs=pl.BlockSpec((tm, D_MODEL), row),
        out_shape=jax.ShapeDtypeStruct((n, D_MODEL), F32),
        scratch_shapes=[pltpu.VMEM((tm, D_MODEL), BF16)],
        compiler_params=_cparams(("arbitrary",)),
        name="mix_cross",
    )(*os_, *ls_, y, x2d, p['attn_out_norm'], p['ssm_out_norm'], p['w_glu'], p['b_glu'], p['w_out'],
      mkv, p['norm_x'], p['w_xq'], p['xq_norm'], p['w_xo'])


MEM_HALVES = X_HEAD_DIM // 128
MEM_ROW_STRIDE = 2 * MEM_HALVES * X_HEADS


def _cross_sample_phases(h_ref, mem_ref, g_ref, wq_ref, qn_ref, wo_ref, o_ref, qsc, qm, osel, att, *, bb_n, t_n):
    h = h_ref[...]
    hn = _rms_rows(h, g_ref[...]).astype(BF16)
    qn = qn_ref[...]
    for hd in range(X_HEADS):
        cols = slice(hd * X_HEAD_DIM, (hd + 1) * X_HEAD_DIM)
        qsc[:, cols] = _rms_rows(_dot(hn, wq_ref[:, cols]), qn) * (X_HEAD_DIM ** -0.5)
    xh = X_HEADS
    head = lax.broadcasted_iota(jnp.int32, (1, D_MODEL), 1) // X_HEAD_DIM
    sel = head == lax.broadcasted_iota(jnp.int32, (xh, 1), 0)
    sel_rows = head == lax.broadcasted_iota(jnp.int32, (t_n * xh, 1), 0) % xh

    def gather(bb, kv):
        parts = [mem_ref[bb, pl.ds((kv * MEM_HALVES + half) * xh + hd, N_MEM, stride=MEM_ROW_STRIDE), :]
                 for hd in range(xh) for half in range(MEM_HALVES)]
        return jnp.concatenate(parts, axis=1).astype(BF16)

    scores = []
    for bb in range(bb_n):
        for t in range(t_n):
            r = bb * t_n + t
            qm[bb, t * xh:(t + 1) * xh, :] = jnp.where(sel, qsc[r:r + 1, :], 0.0)
        scores.append(_dot_t(qm[bb].astype(BF16), gather(bb, 0)))
    yield
    probs = []
    for s in scores:
        p = jnp.exp(s - jnp.max(s, axis=-1, keepdims=True))
        probs.append((p.astype(BF16), jnp.sum(p, axis=-1, keepdims=True)))
    yield
    outs = [_dot(probs[bb][0], gather(bb, 1)) for bb in range(bb_n)]
    yield
    for bb in range(bb_n):
        osel[bb] = jnp.where(sel_rows, outs[bb] / probs[bb][1], 0.0)
        for t in range(t_n):
            r = bb * t_n + t
            att[r:r + 1, :] = jnp.sum(osel[bb, t * xh:(t + 1) * xh, :], axis=0, keepdims=True)
    o_ref[...] = h + _dot(att[...].astype(BF16), wo_ref[...])


FFN_CHUNKS = 2
FFN_TF = D_FF // FFN_CHUNKS
CARRY_ROWS = 8


def _silu(x):
    half = 0.5 * x
    return half * (1.0 + jnp.tanh(half))


def _conv_gate(up_g, up_v, prev_g, prev_v, bufg, bufv, cwg_ref, cwv_ref, cbg_ref, cbv_ref, tm, lag):
    hist = prev_g.shape[0]
    bufg[0:hist] = prev_g
    bufg[hist:hist + tm] = up_g
    bufv[0:hist] = prev_v
    bufv[hist:hist + tm] = up_v
    cg = cbg_ref[...] + (cwg_ref[0:1, :] * bufg[hist - 2 * lag:hist - 2 * lag + tm]
                         + cwg_ref[1:2, :] * bufg[hist - lag:hist - lag + tm] + cwg_ref[2:3, :] * up_g)
    cv = cbv_ref[...] + (cwv_ref[0:1, :] * bufv[hist - 2 * lag:hist - 2 * lag + tm]
                         + cwv_ref[1:2, :] * bufv[hist - lag:hist - lag + tm] + cwv_ref[2:3, :] * up_v)
    return (_silu(cg) * cv).astype(BF16)


FFN_COLS = 256
FFN_NCHUNK = D_FF // FFN_COLS


def _ffn_prompt_kernel(h_ref, g_ref, wup_ref, cw_ref, cb_ref, wd_ref, o_ref, tail_ref, act, carry, *, tm):
    i = pl.program_id(1)
    h = h_ref[...]
    xn = _rms_rows(h, g_ref[...]).astype(BF16)

    @pl.when(i == 0)
    def _():
        carry[...] = jnp.zeros(carry.shape, F32)

    for n in range(FFN_NCHUNK):
        conv = []
        for part in range(2):
            cols = slice(part * D_FF + n * FFN_COLS, part * D_FF + (n + 1) * FFN_COLS)
            up = _dot(xn, wup_ref[:, cols])
            ext = jnp.concatenate([carry[part, n], up], axis=0)
            conv.append(cb_ref[:, cols] + (cw_ref[0:1, cols] * ext[CARRY_ROWS - 2:CARRY_ROWS - 2 + tm]
                                           + cw_ref[1:2, cols] * ext[CARRY_ROWS - 1:CARRY_ROWS - 1 + tm]
                                           + cw_ref[2:3, cols] * up))
            carry[part, n] = up[tm - CARRY_ROWS:]
            tail_ref[:, cols] = up[tm - CARRY_ROWS:]
        cg, cv = conv
        act[:, n * FFN_COLS:(n + 1) * FFN_COLS] = (_silu(cg) * cv).astype(BF16)
    split = (FFN_NCHUNK // 2) * FFN_COLS
    o_ref[...] = h + (_dot(act[:, :split], wd_ref[:split, :]) + _dot(act[:, split:], wd_ref[split:, :]))


def _ffn_prompt(h2d, gain, wup_bf, conv_w, conv_b, wdown_bf, batch, seq, tm):
    n_i = seq // tm
    row = lambda b, i: (b * n_i + i, 0)
    fixed = lambda b, i: (0, 0)
    once = pl.Buffered(1)
    sds = jax.ShapeDtypeStruct
    return pl.pallas_call(
        functools.partial(_ffn_prompt_kernel, tm=tm),
        grid=(batch, n_i),
        in_specs=[pl.BlockSpec((tm, D_MODEL), row), pl.BlockSpec((1, D_MODEL), fixed),
                  pl.BlockSpec((D_MODEL, 2 * D_FF), fixed, pipeline_mode=once),
                  pl.BlockSpec((CONV_W, 2 * D_FF), fixed), pl.BlockSpec((1, 2 * D_FF), fixed),
                  pl.BlockSpec((D_FF, D_MODEL), fixed, pipeline_mode=once)],
        out_specs=[pl.BlockSpec((tm, D_MODEL), row),
                   pl.BlockSpec((None, None, CARRY_ROWS, 2 * D_FF), lambda b, i: (b, i, 0, 0))],
        out_shape=[sds((batch * seq, D_MODEL), F32), sds((batch, n_i, CARRY_ROWS, 2 * D_FF), F32)],
        scratch_shapes=[pltpu.VMEM((tm, D_FF), BF16), pltpu.VMEM((2, FFN_NCHUNK, CARRY_ROWS, FFN_COLS), F32)],
        compiler_params=_cparams(("arbitrary",) * 2),
        name="ffn_prompt",
    )(h2d, gain, wup_bf, conv_w, conv_b, wdown_bf)


def _ffn_sample_kernel(h_ref, g_ref, sg_ref, sv_ref, wg_ref, wv_ref, cwg_ref, cwv_ref, cbg_ref, cbv_ref, wd_ref,
                       o_ref, ng_ref, nv_ref, xn, bufg, bufv, *, n_seq, rows):
    c = pl.program_id(0)

    @pl.when(c == 0)
    def _():
        xn[...] = _rms_rows(h_ref[...], g_ref[...]).astype(BF16)

    up_g = _dot(xn[...], wg_ref[...])
    up_v = _dot(xn[...], wv_ref[...])
    act = _conv_gate(up_g, up_v, sg_ref[...], sv_ref[...], bufg, bufv, cwg_ref, cwv_ref, cbg_ref, cbv_ref,
                     rows, n_seq)
    hist = (CONV_W - 1) * n_seq
    ng_ref[...] = bufg[rows:rows + hist]
    nv_ref[...] = bufv[rows:rows + hist]
    contrib = _dot(act, wd_ref[...])

    @pl.when(c == 0)
    def _():
        o_ref[...] = h_ref[...] + contrib

    @pl.when(c > 0)
    def _():
        o_ref[...] += contrib


def _ffn_sample(h_tm, gain, state_tm, wup_bf, conv_w, conv_b, wdown_bf, n_seq, n_steps):
    rows = n_seq * n_steps
    hist = (CONV_W - 1) * n_seq
    nc, tf = FFN_CHUNKS, FFN_TF
    gcol = lambda c: (0, c)
    vcol = lambda c: (0, nc + c)
    fixed = lambda c: (0, 0)
    sds = jax.ShapeDtypeStruct
    return pl.pallas_call(
        functools.partial(_ffn_sample_kernel, n_seq=n_seq, rows=rows),
        grid=(nc,),
        in_specs=[pl.BlockSpec((rows, D_MODEL), fixed), pl.BlockSpec((1, D_MODEL), fixed),
                  pl.BlockSpec((hist, tf), gcol), pl.BlockSpec((hist, tf), vcol),
                  pl.BlockSpec((D_MODEL, tf), gcol), pl.BlockSpec((D_MODEL, tf), vcol),
                  pl.BlockSpec((CONV_W, tf), gcol), pl.BlockSpec((CONV_W, tf), vcol),
                  pl.BlockSpec((1, tf), gcol), pl.BlockSpec((1, tf), vcol),
                  pl.BlockSpec((tf, D_MODEL), lambda c: (c, 0))],
        out_specs=[pl.BlockSpec((rows, D_MODEL), fixed), pl.BlockSpec((hist, tf), gcol),
                   pl.BlockSpec((hist, tf), gcol)],
        out_shape=[sds((rows, D_MODEL), F32), sds((hist, D_FF), F32), sds((hist, D_FF), F32)],
        scratch_shapes=[pltpu.VMEM((rows, D_MODEL), BF16), pltpu.VMEM((rows + hist, tf), F32),
                        pltpu.VMEM((rows + hist, tf), F32)],
        compiler_params=_cparams(("arbitrary",)),
        name="ffn_sample",
    )(h_tm, gain, state_tm, state_tm, wup_bf, wup_bf, conv_w, conv_w, conv_b, conv_b, wdown_bf)


def _rope_tables(pos):
    half = ROT_DIM // 2
    inv = jnp.power(jnp.float32(ROPE_THETA), -jnp.arange(half, dtype=F32) * 2.0 / ROT_DIM)
    ang = pos.astype(F32)[:, None] * inv[None, :]
    cos, sin = jnp.cos(ang), jnp.sin(ang)
    ones = jnp.ones((pos.shape[0], HEAD_DIM - ROT_DIM), F32)
    zeros = jnp.zeros((pos.shape[0], HEAD_DIM - ROT_DIM), F32)
    zh = jnp.zeros_like(sin)
    c_head = jnp.concatenate([cos, cos, ones], axis=1)
    s1_head = jnp.concatenate([-sin, zh, zeros], axis=1)
    s2_head = jnp.concatenate([zh, sin, zeros], axis=1)
    tile = lambda a: jnp.tile(a, (1, HEADS_PER_GROUP))
    return tile(c_head), tile(s1_head), tile(s2_head)


def _block_diag(w):
    g, a, b = w.shape
    eye = jnp.eye(g, dtype=w.dtype)
    return (eye[:, None, :, None] * w[:, :, None, :]).reshape(g * a, g * b)


def _layer_params(l, norm_mix, w_in, q_norm, k_norm, ssm_lam_re, ssm_lam_im, ssm_log_dt, ssm_b_re, ssm_b_im,
                  ssm_c_re, ssm_c_im, ssm_d, w_glu, b_glu, attn_out_norm, ssm_out_norm, w_out, norm_x, norm_mem,
                  w_xq, w_xkv, xq_norm, xk_norm, w_xo, norm_ffn, w_up, conv_w, conv_b, w_down):
    row = lambda a: a.reshape(1, -1).astype(F32)
    return dict(
        norm_mix=row(norm_mix[l]), w_in=w_in[l].astype(BF16),
        q_norm=row(jnp.tile(q_norm[l], HEADS_PER_GROUP)), k_norm=row(jnp.tile(k_norm[l], HEADS_PER_GROUP)),
        lam_re=row(ssm_lam_re[l]), lam_im=row(ssm_lam_im[l]),
        log_dt=row(jnp.repeat(ssm_log_dt[l], SSM_STATE)),
        b_re=_block_diag(jnp.swapaxes(ssm_b_re[l], 1, 2)).astype(BF16),
        b_im=_block_diag(jnp.swapaxes(ssm_b_im[l], 1, 2)).astype(BF16),
        c_re=_block_diag(jnp.swapaxes(ssm_c_re[l], 1, 2)).astype(BF16),
        c_im=_block_diag(jnp.swapaxes(ssm_c_im[l], 1, 2)).astype(BF16),
        ssm_d=row(ssm_d[l]), w_glu=w_glu[l].astype(BF16), b_glu=row(b_glu[l]),
        attn_out_norm=row(attn_out_norm[l]), ssm_out_norm=row(ssm_out_norm[l]), w_out=w_out[l].astype(BF16),
        norm_x=row(norm_x[l]), norm_mem=row(norm_mem[l]), w_xq=w_xq[l].astype(BF16),
        w_xkv=w_xkv[l].astype(BF16), xq_norm=row(xq_norm[l]), xk_norm=row(xk_norm[l]),
        w_xo=w_xo[l].astype(BF16), norm_ffn=row(norm_ffn[l]), w_up=w_up[l].astype(BF16),
        conv_w=conv_w[l].astype(F32), conv_b=row(conv_b[l]), w_down=w_down[l].astype(BF16))


def _kv_rows(kv_halves, n_seq, n_steps):
    heads_per_half = LANES // HEAD_DIM
    a = kv_halves.reshape(2, LANE_SPLIT, n_seq, n_steps, heads_per_half, HEAD_DIM)
    return jnp.transpose(a, (2, 3, 0, 1, 4, 5)).reshape(n_seq, n_steps, 2, HEADS_PER_GROUP, HEAD_DIM)


def _to_time_major(a, n_seq, n_steps):
    return a.reshape(n_seq, n_steps, -1).swapaxes(0, 1).reshape(n_steps * n_seq, -1)


def _to_seq_major(a, n_seq, n_steps):
    return a.reshape(n_steps, n_seq, -1).swapaxes(0, 1).reshape(n_seq * n_steps, -1)


def _mixer_front(x2d, p, tables, tm, feature_major_last=False, sample_attn=None):
    return _in_proj(x2d, p['norm_mix'], p['w_in'], p['q_norm'], p['k_norm'], *tables, tm, feature_major_last,
                    sample_attn)


def _run_ssm(u_tm, s0r, s0i, p, n_seq, n_steps, chunk, sample_attn=None):
    return _ssm(u_tm, s0r, s0i, p['lam_re'], p['lam_im'], p['log_dt'], p['b_re'], p['b_im'],
                p['c_re'], p['c_im'], p['ssm_d'], n_seq, n_steps, chunk, sample_attn)


def _halves_to_time_major(a, n_seq, n_steps):
    return a.reshape(LANE_SPLIT, n_seq, n_steps, LANES).swapaxes(1, 2).reshape(LANE_SPLIT, n_steps * n_seq, LANES)


def _halves_to_seq_major(a, n_seq, n_steps):
    return a.reshape(LANE_SPLIT, n_steps, n_seq, LANES).swapaxes(1, 2).reshape(LANE_SPLIT, n_seq * n_steps, LANES)


def kernel(x_prompt, x_sample, mem_prompt, cache_kv_dil1, cache_kv_dil4, cache_kv_dil16, state_ssm_re, state_ssm_im, state_ffn_conv, cache_mem_kv, norm_mix, w_in, q_norm, k_norm, ssm_lam_re, ssm_lam_im, ssm_log_dt, ssm_b_re, ssm_b_im, ssm_c_re, ssm_c_im, ssm_d, w_glu, b_glu, attn_out_norm, ssm_out_norm, w_out, norm_x, norm_mem, w_xq, w_xkv, xq_norm, xk_norm, w_xo, norm_ffn, w_up, conv_w, conv_b, w_down):
    bp, lp, _ = x_prompt.shape
    bs, ls, _ = x_sample.shape
    depth = w_in.shape[0]
    past_len = cache_kv_dil16.shape[2]
    caches = (cache_kv_dil1, cache_kv_dil4, cache_kv_dil16)
    tables_p = _rope_tables(jnp.arange(lp))
    tables_s = tuple(jnp.tile(t, (bs, 1)) for t in _rope_tables(past_len + jnp.arange(ls)))

    yp = x_prompt.reshape(bp * lp, D_MODEL)
    ys = x_sample.reshape(bs * ls, D_MODEL)
    kv_p = [[] for _ in range(N_GROUPS)]
    kv_s = [[] for _ in range(N_GROUPS)]
    re_p, im_p, conv_p, mem_p, re_s, im_s, conv_s = [], [], [], [], [], [], []
    for l in range(depth):
        p = _layer_params(l, norm_mix, w_in, q_norm, k_norm, ssm_lam_re, ssm_lam_im, ssm_log_dt, ssm_b_re,
                          ssm_b_im, ssm_c_re, ssm_c_im, ssm_d, w_glu, b_glu, attn_out_norm, ssm_out_norm, w_out,
                          norm_x, norm_mem, w_xq, w_xkv, xq_norm, xk_norm, w_xo, norm_ffn, w_up, conv_w, conv_b,
                          w_down)
        sample_front = _mixer_front(ys, p, tables_s, bs * ls)
        mkv, mkv_rows = _memory_kv(mem_prompt.reshape(bp * N_MEM, D_MODEL), p['norm_mem'], p['w_xkv'],
                                   p['xk_norm'], bp)
        mkv = mkv.reshape(bp, N_MEM, 2 * D_MODEL)
        mkv_rows = mkv_rows.reshape(bp, N_MEM, 2, MEM_HALVES, X_HEADS, LANES)
        mem_p.append(jnp.transpose(mkv_rows, (0, 1, 2, 4, 3, 5)).reshape(bp, N_MEM, 2, X_HEADS, X_HEAD_DIM))
        assert ATTN_GROUPS[-1][0] >= lp
        mid, wide = N_GROUPS - 2, N_GROUPS - 1
        q0, q1, q2, kv0, kv1, kv2, u, kvt, o_mid, lse_mid = _mixer_front(
            yp, p, tables_p, 512, True,
            sample_attn=(sample_front[mid], sample_front[N_GROUPS + mid], caches[mid][l], bs, ls, ATTN_GROUPS[mid][1]))
        zero_state = jnp.zeros((bp, SSM_LANES), F32)
        y_p, fr, fi, o_wide, lse_wide = _run_ssm(
            u, zero_state, zero_state, p, bp, lp, 64,
            sample_attn=(sample_front[wide], sample_front[N_GROUPS + wide], caches[wide][l], bs, ls,
                         ATTN_GROUPS[wide][1]))
        re_p.append(fr.reshape(bp, SSM_GROUPS, SSM_STATE))
        im_p.append(fi.reshape(bp, SSM_GROUPS, SSM_STATE))
        os_, ls_ = [], []
        for g in range(N_GROUPS):
            qg, kvg = sample_front[g], sample_front[N_GROUPS + g]
            if g == wide:
                o, lse = o_wide, lse_wide
            elif g == mid:
                o, lse = o_mid, lse_mid
            else:
                o, lse = _sample_attn(qg, kvg, caches[g][l], bs, ls, ATTN_GROUPS[g][1], 32)
            os_.append(o)
            ls_.append(lse)
            kv_s[g].append(_kv_rows(kvg, bs, ls))
        y, fr, fi = _run_ssm(_halves_to_time_major(sample_front[2 * N_GROUPS], bs, ls),
                             state_ssm_re[l].reshape(bs, SSM_LANES), state_ssm_im[l].reshape(bs, SSM_LANES),
                             p, bs, ls, ls)
        re_s.append(fr.reshape(bs, SSM_GROUPS, SSM_STATE))
        im_s.append(fi.reshape(bs, SSM_GROUPS, SSM_STATE))
        h_s = _mix_out(os_, ls_, _halves_to_seq_major(y, bs, ls), ys, p['attn_out_norm'], p['ssm_out_norm'],
                       p['w_glu'], p['b_glu'], p['w_out'], bs * ls)
        os_, ls_ = [], []
        for g, (qg, kvg) in enumerate(((q0, kv0), (q1, kv1), (q2, kv2))):
            win, dil = ATTN_GROUPS[g]
            if g == 0:
                o, lse, h = _prompt_attn(qg, kvg, bp, lp, dil, cross_sample=(
                    h_s, cache_mem_kv[l], p['norm_x'], p['w_xq'], p['xq_norm'], p['w_xo'], bs, ls))
            else:
                o, lse = _prompt_attn(qg, kvg, bp, lp, dil)
            os_.append(o)
            ls_.append(lse)
            keep = min(win, lp)
            if g == N_GROUPS - 1:
                kv_p[g].append(jnp.transpose(kvt.reshape(bp, 2, HEADS_PER_GROUP, HEAD_DIM, lp), (0, 4, 1, 2, 3)))
            else:
                tail = kvg.reshape(2 * LANE_SPLIT, bp, lp, LANES)[:, :, lp - keep:].reshape(2 * LANE_SPLIT, bp * keep, LANES)
                kv_p[g].append(_kv_rows(tail, bp, keep))
        hp_ = _mix_cross(os_, ls_, y_p, yp, mkv, p, lp, 512)
        yp, tail = _ffn_prompt(hp_, p['norm_ffn'], p['w_up'], p['conv_w'], p['conv_b'], p['w_down'], bp, lp, 512)
        conv_p.append(tail[:, -1, CARRY_ROWS - (CONV_W - 1):])
        state_tm = state_ffn_conv[l].swapaxes(0, 1).reshape((CONV_W - 1) * bs, 2 * D_FF)
        ys_tm, ng, nv = _ffn_sample(_to_time_major(h, bs, ls), p['norm_ffn'], state_tm, p['w_up'], p['conv_w'],
                                    p['conv_b'], p['w_down'], bs, ls)
        ys = _to_seq_major(ys_tm, bs, ls)
        conv_s.append(jnp.concatenate([ng, nv], axis=-1).reshape(CONV_W - 1, bs, 2 * D_FF).swapaxes(0, 1))

    st = jnp.stack
    return (yp.reshape(bp, lp, D_MODEL), ys.reshape(bs, ls, D_MODEL),
            st(kv_p[0]), st(kv_p[1]), st(kv_p[2]), st(re_p), st(im_p), st(conv_p), st(mem_p),
            st(kv_s[0]), st(kv_s[1]), st(kv_s[2]), st(re_s), st(im_s), st(conv_s))
```
